```python
import jax, jax.numpy as jnp
from jax import lax
import numpy as np

D_MODEL = 4096
BATCH = 4
SEQ = 2048
DEPTH = 1

CHUNK = 64
Q_BLOCK = 128
N_MEM = 256
MIX_WIDTH = D_MODEL
FOX_WIDTH = MIX_WIDTH // 2
FOX_HEAD_DIM = 128
FOX_HEADS = FOX_WIDTH // FOX_HEAD_DIM
RWKV_WIDTH = MIX_WIDTH - FOX_WIDTH
RWKV_HEAD_DIM = 64
RWKV_HEADS = RWKV_WIDTH // RWKV_HEAD_DIM
DECAY_LORA = 96
ICLR_LORA = 96
GATE_LORA = 256
FOX_COLS = 3 * FOX_WIDTH + FOX_HEADS
RWKV_SHIFT_COLS = 3 * RWKV_WIDTH + DECAY_LORA + ICLR_LORA + GATE_LORA
IN_COLS = FOX_COLS + RWKV_SHIFT_COLS
XATTN_HEADS = 4
XATTN_HEAD_DIM = D_MODEL // XATTN_HEADS
D_FF = 11008
RMS_EPS = 1e-6
RWKV_GN_EPS = 64e-5
NEG_INF = -1e30

kernel_name = 'fox_rwkv7_macaron_sandwich_memory_layer'


def rms_norm(x, g):
    xf = x.astype(jnp.float32)
    y = xf * lax.rsqrt(jnp.mean(xf * xf, axis=-1, keepdims=True) + RMS_EPS)
    return (y * g.astype(jnp.float32)).astype(x.dtype)


def swiglu(h, w_gate, w_up, w_down):
    u = jax.nn.silu(jnp.einsum('bsd,df->bsf', h, w_gate)) * jnp.einsum('bsd,df->bsf', h, w_up)
    return jnp.einsum('bsf,fd->bsd', u, w_down)


def fox_attention(q, k, v, logf):
    seq = q.shape[1]
    c = jnp.cumsum(logf, axis=1).transpose(0, 2, 1)
    scale = FOX_HEAD_DIM ** -0.5
    outs = []
    for blk in range(seq // Q_BLOCK):
        q0, q1 = blk * Q_BLOCK, (blk + 1) * Q_BLOCK
        s = jnp.einsum('bqhd,bkhd->bhqk', q[:, q0:q1], k[:, :q1]).astype(jnp.float32) * scale
        s = s + c[:, :, q0:q1, None] - c[:, :, None, :q1]
        causal = (q0 + jnp.arange(Q_BLOCK))[:, None] >= jnp.arange(q1)[None, :]
        p = jax.nn.softmax(jnp.where(causal, s, NEG_INF), axis=-1)
        outs.append(jnp.einsum('bhqk,bkhd->bqhd', p.astype(v.dtype), v[:, :q1]))
    return jnp.concatenate(outs, axis=1)


def rwkv7_recurrence(r, decay, k, v, kk, a):
    b, _, h, n = r.shape

    def step(state, inp):
        r_t, w_t, k_t, v_t, kk_t, a_t = inp
        s_kk = jnp.einsum('bhvk,bhk->bhv', state, kk_t)
        state = (state * w_t[:, :, None, :]
                 - s_kk[..., None] * (kk_t * a_t)[:, :, None, :]
                 + v_t[..., None] * k_t[:, :, None, :])
        return state, jnp.einsum('bhvk,bhk->bhv', state, r_t)

    xs = tuple(jnp.swapaxes(t.astype(jnp.float32), 0, 1) for t in (r, decay, k, v, kk, a))
    state0 = jnp.zeros((b, h, n, n), jnp.float32)
    _, y = lax.scan(step, state0, xs)
    return jnp.swapaxes(y, 0, 1)


def parallel_mixer(h, w_in, fox_f_bias, rwkv_mu, rwkv_w0, rwkv_w_up, rwkv_a0, rwkv_a_up,
                   rwkv_g_up, rwkv_k_k, rwkv_k_a, rwkv_r_k, rwkv_ln_w, rwkv_ln_b, w_out):
    b, s, _ = h.shape
    proj = jnp.einsum('bsd,dc->bsc', h, w_in)
    fox_proj, rwkv_proj = proj[..., :FOX_COLS], proj[..., FOX_COLS:]

    q, k, v, f_logit = jnp.split(fox_proj, [FOX_WIDTH, 2 * FOX_WIDTH, 3 * FOX_WIDTH], axis=-1)
    fox_heads = lambda t: t.reshape(b, s, FOX_HEADS, FOX_HEAD_DIM)
    logf = jax.nn.log_sigmoid((f_logit + fox_f_bias).astype(jnp.float32))
    y_fox = fox_attention(fox_heads(q), fox_heads(k), fox_heads(v), logf).reshape(b, s, FOX_WIDTH)

    prev = jnp.pad(rwkv_proj, ((0, 0), (1, 0), (0, 0)))[:, :-1]
    z = rwkv_proj + rwkv_mu * (prev - rwkv_proj)
    r, kr, vr, xw, xa, xg = jnp.split(
        z, [RWKV_WIDTH, 2 * RWKV_WIDTH, 3 * RWKV_WIDTH, 3 * RWKV_WIDTH + DECAY_LORA,
            3 * RWKV_WIDTH + DECAY_LORA + ICLR_LORA], axis=-1)
    w_log = -jax.nn.softplus(-(rwkv_w0 + jnp.tanh(xw) @ rwkv_w_up).astype(jnp.float32)) - 0.5
    decay = jnp.exp(-jnp.exp(w_log))
    a = jax.nn.sigmoid((rwkv_a0 + xa @ rwkv_a_up).astype(jnp.float32))
    g = (jax.nn.sigmoid(xg) @ rwkv_g_up).astype(jnp.float32)
    hn = lambda t: t.astype(jnp.float32).reshape(b, s, RWKV_HEADS, RWKV_HEAD_DIM)
    pn = lambda p: p.astype(jnp.float32).reshape(RWKV_HEADS, RWKV_HEAD_DIM)
    rf, kf, vf, decay, a = hn(r), hn(kr), hn(vr), hn(decay), hn(a)
    kk = kf * pn(rwkv_k_k)
    kk = kk / jnp.maximum(jnp.sqrt(jnp.sum(kk * kk, axis=-1, keepdims=True)), 1e-12)
    kf = kf * (1.0 + (a - 1.0) * pn(rwkv_k_a))
    y = rwkv7_recurrence(rf, decay, kf, vf, kk, a)
    mean = jnp.mean(y, axis=-1, keepdims=True)
    var = jnp.mean(jnp.square(y - mean), axis=-1, keepdims=True)
    yn = ((y - mean) * lax.rsqrt(var + RWKV_GN_EPS)).reshape(b, s, RWKV_WIDTH)
    yn = yn * rwkv_ln_w.astype(jnp.float32) + rwkv_ln_b.astype(jnp.float32)
    bonus = (jnp.sum(rf * kf * pn(rwkv_r_k), axis=-1, keepdims=True) * vf).reshape(b, s, RWKV_WIDTH)
    y_rwkv = ((yn + bonus) * g).astype(h.dtype)

    return jnp.einsum('bsc,cd->bsd', jnp.concatenate([y_fox, y_rwkv], axis=-1), w_out)


def cross_attention(h, m, wq, wk, wv, wo):
    b, s, _ = h.shape
    q = jnp.einsum('bsd,dc->bsc', h, wq).reshape(b, s, XATTN_HEADS, XATTN_HEAD_DIM)
    k = jnp.einsum('bmd,dc->bmc', m, wk).reshape(b, m.shape[1], XATTN_HEADS, XATTN_HEAD_DIM)
    v = jnp.einsum('bmd,dc->bmc', m, wv).reshape(b, m.shape[1], XATTN_HEADS, XATTN_HEAD_DIM)
    sc = jnp.einsum('bqhd,bkhd->bhqk', q, k).astype(jnp.float32) * (XATTN_HEAD_DIM ** -0.5)
    p = jax.nn.softmax(sc, axis=-1).astype(v.dtype)
    o = jnp.einsum('bhqk,bkhd->bqhd', p, v).reshape(b, s, D_MODEL)
    return jnp.einsum('bsc,cd->bsd', o, wo)


def setup_inputs(seed: int = 0) -> dict:
    key = jax.random.key(seed)
    ks = iter(jax.random.split(key, 48))
    nrm = lambda shape, scale: scale * jax.random.normal(next(ks), shape, jnp.float32)
    uni = lambda shape, lo, hi: jax.random.uniform(next(ks), shape, jnp.float32, lo, hi)
    gain = lambda width: 1.0 + nrm((DEPTH, width), 0.05)
    L, D = DEPTH, D_MODEL
    return {
        'x': nrm((BATCH, SEQ, D), 1.0),
        'mem': nrm((BATCH, N_MEM, D), 1.0),
        'ffn1_pre_g': gain(D),
        'ffn1_w_gate': nrm((L, D, D_FF), D ** -0.5),
        'ffn1_w_up': nrm((L, D, D_FF), D ** -0.5),
        'ffn1_w_down': nrm((L, D_FF, D), D_FF ** -0.5),
        'ffn1_post_g': gain(D),
        'mix_pre_g': gain(D),
        'w_in': nrm((L, D, IN_COLS), D ** -0.5),
        'fox_f_bias': uni((L, FOX_HEADS), 1.0, 5.0),
        'rwkv_mu': uni((L, RWKV_SHIFT_COLS), 0.0, 1.0),
        'rwkv_w0': uni((L, RWKV_WIDTH), -6.0, -1.0),
        'rwkv_w_up': nrm((L, DECAY_LORA, RWKV_WIDTH), 0.1),
        'rwkv_a0': nrm((L, RWKV_WIDTH), 0.1),
        'rwkv_a_up': nrm((L, ICLR_LORA, RWKV_WIDTH), ICLR_LORA ** -0.5),
        'rwkv_g_up': nrm((L, GATE_LORA, RWKV_WIDTH), GATE_LORA ** -0.5),
        'rwkv_k_k': 0.85 + nrm((L, RWKV_WIDTH), 0.02),
        'rwkv_k_a': 1.0 + nrm((L, RWKV_WIDTH), 0.02),
        'rwkv_r_k': nrm((L, RWKV_WIDTH), 0.1),
        'rwkv_ln_w': gain(RWKV_WIDTH),
        'rwkv_ln_b': nrm((L, RWKV_WIDTH), 0.01),
        'w_out': nrm((L, MIX_WIDTH, D), MIX_WIDTH ** -0.5),
        'mix_post_g': gain(D),
        'xattn_pre_g': gain(D),
        'mem_norm_g': gain(D),
        'xattn_wq': nrm((L, D, D), D ** -0.5),
        'xattn_wk': nrm((L, D, D), D ** -0.5),
        'xattn_wv': nrm((L, D, D), D ** -0.5),
        'xattn_wo': nrm((L, D, D), D ** -0.5),
        'xattn_post_g': gain(D),
        'ffn2_pre_g': gain(D),
        'ffn2_w_gate': nrm((L, D, D_FF), D ** -0.5),
        'ffn2_w_up': nrm((L, D, D_FF), D ** -0.5),
        'ffn2_w_down': nrm((L, D_FF, D), D_FF ** -0.5),
        'ffn2_post_g': gain(D),
    }


def reference(x, mem, ffn1_pre_g, ffn1_w_gate, ffn1_w_up, ffn1_w_down, ffn1_post_g,
              mix_pre_g, w_in, fox_f_bias, rwkv_mu, rwkv_w0, rwkv_w_up, rwkv_a0, rwkv_a_up,
              rwkv_g_up, rwkv_k_k, rwkv_k_a, rwkv_r_k, rwkv_ln_w, rwkv_ln_b, w_out, mix_post_g,
              xattn_pre_g, mem_norm_g, xattn_wq, xattn_wk, xattn_wv, xattn_wo, xattn_post_g,
              ffn2_pre_g, ffn2_w_gate, ffn2_w_up, ffn2_w_down, ffn2_post_g):
    for l in range(DEPTH):
        x = x + 0.5 * rms_norm(swiglu(rms_norm(x, ffn1_pre_g[l]), ffn1_w_gate[l], ffn1_w_up[l],
                                      ffn1_w_down[l]), ffn1_post_g[l])
        mixed = parallel_mixer(rms_norm(x, mix_pre_g[l]), w_in[l], fox_f_bias[l], rwkv_mu[l],
                               rwkv_w0[l], rwkv_w_up[l], rwkv_a0[l], rwkv_a_up[l], rwkv_g_up[l],
                               rwkv_k_k[l], rwkv_k_a[l], rwkv_r_k[l], rwkv_ln_w[l], rwkv_ln_b[l],
                               w_out[l])
        x = x + rms_norm(mixed, mix_post_g[l])
        xa = cross_attention(rms_norm(x, xattn_pre_g[l]), rms_norm(mem, mem_norm_g[l]),
                             xattn_wq[l], xattn_wk[l], xattn_wv[l], xattn_wo[l])
        x = x + rms_norm(xa, xattn_post_g[l])
        x = x + 0.5 * rms_norm(swiglu(rms_norm(x, ffn2_pre_g[l]), ffn2_w_gate[l], ffn2_w_up[l],
                                      ffn2_w_down[l]), ffn2_post_g[l])
    return x
```

```python
import functools

import jax
import jax.numpy as jnp
from jax import lax
from jax.experimental import pallas as pl
from jax.experimental.pallas import tpu as pltpu

F32 = jnp.float32
BF16 = jnp.bfloat16

SEQ = 2048
FOX_HEAD_DIM = 128
RWKV_HEAD_DIM = 64
RWKV_WIDTH = 2048
FOX_WIDTH = 2048
DECAY_LORA = 96
ICLR_LORA = 96
GATE_LORA = 256
XATTN_HEADS = 4
RMS_EPS = 1e-6
RWKV_GN_EPS = 64e-5
NEG_INF = -1e30

LANES = 128
CHUNK = 64
VMEM_LIMIT = 56 * 1024 * 1024


def _cparams(sem, vmem=VMEM_LIMIT):
    return pltpu.CompilerParams(dimension_semantics=sem, vmem_limit_bytes=vmem)


NN = (((1,), (0,)), ((), ()))
NT = (((1,), (1,)), ((), ()))
TN = (((0,), (0,)), ((), ()))


def _split2(x):
    hi = x.astype(BF16)
    lo = (x - hi.astype(F32)).astype(BF16)
    return hi, lo


def _split3(x):
    hi = x.astype(BF16)
    r = x - hi.astype(F32)
    mid = r.astype(BF16)
    lo = (r - mid.astype(F32)).astype(BF16)
    return hi, mid, lo


def _dg(a, b, dims):
    return lax.dot_general(a, b, dims, preferred_element_type=F32)


def _dot3(a, b, dims=NN):
    ah, al = _split2(a)
    bh, bl = _split2(b)
    return _dg(ah, bh, dims) + (_dg(ah, bl, dims) + _dg(al, bh, dims))


def _dot_exact_lhs(a_bf16, b, dims=NN):
    b0, b1, b2 = _split3(b)
    return _dg(a_bf16, b0, dims) + (_dg(a_bf16, b1, dims) + _dg(a_bf16, b2, dims))


def _dot_exact_rhs(a, b_bf16, dims=NN):
    a0, a1, a2 = _split3(a)
    return _dg(a0, b_bf16, dims) + (_dg(a1, b_bf16, dims) + _dg(a2, b_bf16, dims))


def _softplus(x):
    return jnp.maximum(x, 0.0) + jnp.log1p(jnp.exp(-jnp.abs(x)))


def _rmsnorm_kernel(x_ref, g_ref, o_ref):
    x = x_ref[...]
    ms = jnp.mean(x * x, axis=-1, keepdims=True)
    o_ref[...] = (x * lax.rsqrt(ms + RMS_EPS) * g_ref[...]).astype(o_ref.dtype)


def rmsnorm(x, g, tm=256):
    m, d = x.shape
    tm = min(tm, m)
    return pl.pallas_call(
        _rmsnorm_kernel,
        grid=(m // tm,),
        in_specs=[pl.BlockSpec((tm, d), lambda i: (i, 0)),
                  pl.BlockSpec((1, d), lambda i: (0, 0))],
        out_specs=pl.BlockSpec((tm, d), lambda i: (i, 0)),
        out_shape=jax.ShapeDtypeStruct((m, d), BF16),
        compiler_params=_cparams(("parallel",)),
        name="rmsnorm",
    )(x, g.reshape(1, d))


def _resnorm_kernel(x_ref, y_ref, gp_ref, gn_ref, xo_ref, h_ref, *, scale):
    y = y_ref[...]
    ms = jnp.mean(y * y, axis=-1, keepdims=True)
    xn = x_ref[...] + scale * (y * lax.rsqrt(ms + RMS_EPS) * gp_ref[...])
    xo_ref[...] = xn
    ms2 = jnp.mean(xn * xn, axis=-1, keepdims=True)
    h_ref[...] = (xn * lax.rsqrt(ms2 + RMS_EPS) * gn_ref[...]).astype(h_ref.dtype)


def _resnorm_last_kernel(x_ref, y_ref, gp_ref, xo_ref, *, scale):
    y = y_ref[...]
    ms = jnp.mean(y * y, axis=-1, keepdims=True)
    xo_ref[...] = x_ref[...] + scale * (y * lax.rsqrt(ms + RMS_EPS) * gp_ref[...])


def resnorm(x, y, g_post, scale, g_next=None, tm=256):
    m, d = x.shape
    row = pl.BlockSpec((tm, d), lambda i: (i, 0))
    vec = pl.BlockSpec((1, d), lambda i: (0, 0))
    if g_next is None:
        return pl.pallas_call(
            functools.partial(_resnorm_last_kernel, scale=scale),
            grid=(m // tm,),
            in_specs=[row, row, vec],
            out_specs=row,
            out_shape=jax.ShapeDtypeStruct((m, d), F32),
            compiler_params=_cparams(("parallel",)),
            name="resnorm_last",
        )(x, y, g_post.reshape(1, d))
    return pl.pallas_call(
        functools.partial(_resnorm_kernel, scale=scale),
        grid=(m // tm,),
        in_specs=[row, row, vec, vec],
        out_specs=[row, row],
        out_shape=[jax.ShapeDtypeStruct((m, d), F32),
                   jax.ShapeDtypeStruct((m, d), BF16)],
        compiler_params=_cparams(("parallel",)),
        name="resnorm",
    )(x, y, g_post.reshape(1, d), g_next.reshape(1, d))


def _mm_kernel(a_ref, w_ref, o_ref):
    o_ref[...] = jnp.dot(a_ref[...], w_ref[...],
                         preferred_element_type=F32).astype(o_ref.dtype)


def matmul(a, w, out_dtype, tm=1024, tn=1024):
    m, k = a.shape
    _, n = w.shape
    tm = min(tm, m)
    tn = min(tn, n)
    assert m % tm == 0 and n % tn == 0
    return pl.pallas_call(
        _mm_kernel,
        grid=(m // tm, n // tn),
        in_specs=[pl.BlockSpec((tm, k), lambda i, j: (i, 0)),
                  pl.BlockSpec((k, tn), lambda i, j: (0, j))],
        out_specs=pl.BlockSpec((tm, tn), lambda i, j: (i, j)),
        out_shape=jax.ShapeDtypeStruct((m, n), out_dtype),
        compiler_params=_cparams(("parallel", "arbitrary")),
        name="matmul",
    )(a, w)


def _ffn_kernel(h_ref, wg_ref, wu_ref, wd_ref, o_ref, *, n_chunk):
    f = pl.program_id(1)

    @pl.when(f == 0)
    def _():
        o_ref[...] = jnp.zeros_like(o_ref)

    h = h_ref[...]
    g = jnp.dot(h, wg_ref[...], preferred_element_type=F32)
    u = jnp.dot(h, wu_ref[...], preferred_element_type=F32)
    act = (g * jax.nn.sigmoid(g) * u).astype(BF16)
    d = o_ref.shape[1]
    for c in range(0, d, n_chunk):
        o_ref[:, c:c + n_chunk] += jnp.dot(act, wd_ref[:, c:c + n_chunk],
                                           preferred_element_type=F32)


def ffn(h, wg, wu, wd, tm=1024, tf=256, n_chunk=512):
    m, d = h.shape
    dff = wg.shape[1]
    tm = min(tm, m)
    tf = min(tf, dff)
    n_chunk = min(n_chunk, d)
    assert m % tm == 0 and dff % tf == 0 and d % n_chunk == 0
    return pl.pallas_call(
        functools.partial(_ffn_kernel, n_chunk=n_chunk),
        grid=(m // tm, dff // tf),
        in_specs=[pl.BlockSpec((tm, d), lambda i, f: (i, 0), pipeline_mode=pl.Buffered(1)),
                  pl.BlockSpec((d, tf), lambda i, f: (0, f)),
                  pl.BlockSpec((d, tf), lambda i, f: (0, f)),
                  pl.BlockSpec((tf, d), lambda i, f: (f, 0))],
        out_specs=pl.BlockSpec((tm, d), lambda i, f: (i, 0), pipeline_mode=pl.Buffered(1)),
        out_shape=jax.ShapeDtypeStruct((m, d), F32),
        compiler_params=_cparams(("parallel", "arbitrary")),
        name="ffn",
    )(h, wg, wu, wd)


def _fgate_kernel(f_ref, b_ref, c_ref, *, blk):
    s = f_ref.shape[0]
    row = lax.broadcasted_iota(jnp.int32, (blk, blk), 0)
    col = lax.broadcasted_iota(jnp.int32, (blk, blk), 1)
    tri = jnp.where(row >= col, 1.0, 0.0).astype(BF16)
    carry = jnp.zeros((1, f_ref.shape[1]), F32)
    for r0 in range(0, s, blk):
        x = f_ref[r0:r0 + blk, :] + b_ref[...]
        logf = -_softplus(-x)
        c = _dot_exact_lhs(tri, logf) + carry
        c_ref[r0:r0 + blk, :] = c
        carry = c[blk - 1:blk, :]


def fgate_cumsum(small, f_bias_row, f_block, seq, blk=256):
    m = small.shape[0]
    blk = min(blk, seq)
    return pl.pallas_call(
        functools.partial(_fgate_kernel, blk=blk),
        grid=(m // seq,),
        in_specs=[pl.BlockSpec((seq, LANES), lambda b: (b, f_block)),
                  pl.BlockSpec((1, LANES), lambda b: (0, 0))],
        out_specs=pl.BlockSpec((seq, LANES), lambda b: (b, 0)),
        out_shape=jax.ShapeDtypeStruct((m, LANES), F32),
        compiler_params=_cparams(("parallel",)),
        name="fgate_cumsum",
    )(small, f_bias_row)


def _fox_kernel(q_ref, k_ref, v_ref, cq_ref, ck_ref, o_ref, m_ref, l_ref, acc_ref,
                *, tq, tk, scale):
    qi = pl.program_id(2)
    ki = pl.program_id(3)

    @pl.when(ki == 0)
    def _():
        m_ref[...] = jnp.full_like(m_ref, NEG_INF)
        l_ref[...] = jnp.zeros_like(l_ref)
        acc_ref[...] = jnp.zeros_like(acc_ref)

    @pl.when(ki <= qi)
    def _():
        s = _dg(q_ref[...], k_ref[...], NT) * scale
        s = s + cq_ref[...] - ck_ref[...]
        row = lax.broadcasted_iota(jnp.int32, (tq, tk), 0) + qi * tq
        col = lax.broadcasted_iota(jnp.int32, (tq, tk), 1) + ki * tk
        s = jnp.where(row >= col, s, NEG_INF)
        m_prev = m_ref[...]
        m_new = jnp.maximum(m_prev, jnp.max(s, axis=-1, keepdims=True))
        alpha = jnp.exp(m_prev - m_new)
        p = jnp.exp(s - m_new)
        l_ref[...] = alpha * l_ref[...] + jnp.sum(p, axis=-1, keepdims=True)
        acc_ref[...] = alpha * acc_ref[...] + jnp.dot(
            p.astype(BF16), v_ref[...], preferred_element_type=F32)
        m_ref[...] = m_new

    @pl.when(ki == qi)
    def _():
        o_ref[...] = (acc_ref[...] / l_ref[...]).astype(o_ref.dtype)


def fox_attention(qkv, c_col, c_row, batch, seq, heads, tq=512):
    tq = min(tq, seq)
    tk = tq
    nq = seq // tq
    dh = FOX_HEAD_DIM
    kern = functools.partial(_fox_kernel, tq=tq, tk=tk, scale=dh ** -0.5)
    return pl.pallas_call(
        kern,
        grid=(batch, heads, nq, nq),
        in_specs=[
            pl.BlockSpec((tq, dh), lambda b, h, qi, ki: (b * nq + qi, h)),
            pl.BlockSpec((tk, dh), lambda b, h, qi, ki: (b * nq + jnp.minimum(ki, qi), heads + h)),
            pl.BlockSpec((tk, dh), lambda b, h, qi, ki: (b * nq + jnp.minimum(ki, qi), 2 * heads + h)),
            pl.BlockSpec((None, None, tq, 1), lambda b, h, qi, ki: (b, h, qi, 0)),
            pl.BlockSpec((None, None, 1, tk), lambda b, h, qi, ki: (b, h, 0, jnp.minimum(ki, qi))),
        ],
        out_specs=pl.BlockSpec((tq, dh), lambda b, h, qi, ki: (b * nq + qi, h)),
        out_shape=jax.ShapeDtypeStruct((batch * seq, heads * dh), BF16),
        scratch_shapes=[pltpu.VMEM((tq, 1), F32), pltpu.VMEM((tq, 1), F32),
                        pltpu.VMEM((tq, dh), F32)],
        compiler_params=_cparams(("parallel", "parallel", "arbitrary", "arbitrary")),
        name="fox_attention",
    )(qkv, qkv, qkv, c_col, c_row)


def _rwkv_prep_kernel(r_ref, k_ref, v_ref, rp_ref, kp_ref, vp_ref, sm_ref, smp_ref,
                      mu_r_ref, mu_k_ref, mu_v_ref, mu_s_ref,
                      w0_ref, wup_ref, a0_ref, aup_ref, gup_ref, kk_ref, ka_ref,
                      r_o, lw_o, k_o, v_o, kk_o, b_o, g_o, *, tiles_per_seq):
    i = pl.program_id(0)
    first = (i % tiles_per_seq) == 0
    tm = r_ref.shape[0]
    row0 = lax.broadcasted_iota(jnp.int32, (tm, 1), 0) == 0

    def mix(cur_ref, prev_ref, mu_ref):
        cur = cur_ref[...]
        last = jnp.where(first, 0.0, prev_ref[7:8, :])
        prev = jnp.where(row0, last, pltpu.roll(cur, 1, axis=0))
        return cur + mu_ref[...] * (prev - cur)

    zr = mix(r_ref, rp_ref, mu_r_ref)
    zk = mix(k_ref, kp_ref, mu_k_ref)
    zv = mix(v_ref, vp_ref, mu_v_ref)
    zs = mix(sm_ref, smp_ref, mu_s_ref)
    zw = zs[:, 0:LANES]
    za = zs[:, LANES:2 * LANES]
    zg = zs[:, 2 * LANES:2 * LANES + GATE_LORA]

    w_log = -_softplus(-(w0_ref[...] + _dot3(jnp.tanh(zw), wup_ref[...]))) - 0.5
    lw_o[...] = -jnp.exp(w_log)
    a = jax.nn.sigmoid(a0_ref[...] + _dot3(za, aup_ref[...]))
    g_o[...] = _dot3(jax.nn.sigmoid(zg), gup_ref[...])

    tc = zk.shape[1]
    hr = lax.broadcasted_iota(jnp.int32, (tc, tc), 0) // RWKV_HEAD_DIM
    hc = lax.broadcasted_iota(jnp.int32, (tc, tc), 1) // RWKV_HEAD_DIM
    head_ones = jnp.where(hr == hc, 1.0, 0.0).astype(BF16)
    kk = zk * kk_ref[...]
    ss = _dot_exact_rhs(kk * kk, head_ones)
    kk = kk / jnp.maximum(jnp.sqrt(ss), 1e-12)
    r_o[...] = zr
    v_o[...] = zv
    kk_o[...] = kk
    b_o[...] = kk * a
    k_o[...] = zk * (1.0 + (a - 1.0) * ka_ref[...])


def rwkv_prep(rkv, small, mu_r, mu_k, mu_v, mu_s, w0, w_up, a0, a_up, g_up, k_k, k_a,
              seq, tm=512, tc=512):
    m = rkv.shape[0]
    width = RWKV_WIDTH
    tm = min(tm, seq)
    nc = width // tc
    ns = small.shape[1]
    tps = seq // tm
    sub = tm // 8

    def cur(piece):
        return pl.BlockSpec((tm, tc), lambda i, j: (i, piece * nc + j))

    def prev(piece):
        return pl.BlockSpec((8, tc), lambda i, j: (jnp.maximum(i * sub - 1, 0), piece * nc + j))

    colv = pl.BlockSpec((1, tc), lambda i, j: (0, j))
    out = pl.BlockSpec((tm, tc), lambda i, j: (i, j))
    outs = [jax.ShapeDtypeStruct((m, width), F32)] * 7
    return pl.pallas_call(
        functools.partial(_rwkv_prep_kernel, tiles_per_seq=tps),
        grid=(m // tm, nc),
        in_specs=[cur(0), cur(1), cur(2), prev(0), prev(1), prev(2),
                  pl.BlockSpec((tm, ns), lambda i, j: (i, 0)),
                  pl.BlockSpec((8, ns), lambda i, j: (jnp.maximum(i * sub - 1, 0), 0)),
                  colv, colv, colv,
                  pl.BlockSpec((1, ns), lambda i, j: (0, 0)),
                  colv,
                  pl.BlockSpec((LANES, tc), lambda i, j: (0, j)),
                  colv,
                  pl.BlockSpec((LANES, tc), lambda i, j: (0, j)),
                  pl.BlockSpec((GATE_LORA, tc), lambda i, j: (0, j)),
                  colv, colv],
        out_specs=[out] * 7,
        out_shape=outs,
        compiler_params=_cparams(("parallel", "arbitrary")),
        name="rwkv_prep",
    )(rkv, rkv, rkv, rkv, rkv, rkv, small, small,
      mu_r, mu_k, mu_v, mu_s, w0, w_up, a0, a_up, g_up, k_k, k_a)


def _rwkv_pair_chunk(r, lw, k, v, kk, beta, s0, tri_incl, strict, incl, m_a):
    c = r.shape[0]
    big_l = _dot_exact_lhs(tri_incl, lw)
    l_prev = big_l - lw
    l_tot = big_l[c - 1:c, :]
    e_l = jnp.exp(big_l)
    e_nl = jnp.exp(-big_l)
    e_rem = jnp.exp(l_tot - big_l)
    at = -kk * jnp.exp(l_prev)
    rt = r * e_l
    kt = k * e_nl
    bt = beta * e_nl

    def stack2(x):
        return jnp.concatenate([jnp.where(m_a, x, 0.0), jnp.where(m_a, 0.0, x)], axis=0)

    at2, rt2, v2 = stack2(at), stack2(rt), stack2(v)
    b2 = jnp.concatenate([bt, bt], axis=0)
    k2 = jnp.concatenate([kt, kt], axis=0)
    a_ab = jnp.where(strict, _dot3(at2, b2, NT), 0.0)
    a_ak = jnp.where(strict, _dot3(at2, k2, NT), 0.0)
    a_rb = jnp.where(incl, _dot3(rt2, b2, NT), 0.0)
    a_rk = jnp.where(incl, _dot3(rt2, k2, NT), 0.0)

    u = _dot3(at2, s0, NT) + _dot3(a_ak, v2)
    p = a_ab
    n_sq = max(1, (c - 1).bit_length())
    for it in range(n_sq):
        u = u + _dot3(p, u)
        if it + 1 < n_sq:
            p = _dot3(p, p)
    y2 = _dot3(rt2, s0, NT) + _dot3(a_rb, u) + _dot3(a_rk, v2)
    y = y2[:c] + y2[c:]
    s_new = (s0 * jnp.exp(l_tot)
             + _dot3(u, stack2(beta * e_rem), TN)
             + _dot3(v2, stack2(k * e_rem), TN))
    return y, s_new


def _rwkv_kernel(r_ref, lw_ref, k_ref, v_ref, kk_ref, b_ref, g_ref,
                 lnw_ref, lnb_ref, rk_ref, o_ref, s_ref, *, pairs):
    ci = pl.program_id(2)
    c = r_ref.shape[0]

    @pl.when(ci == 0)
    def _():
        s_ref[...] = jnp.zeros_like(s_ref)

    r2 = lax.broadcasted_iota(jnp.int32, (2 * c, 2 * c), 0)
    c2 = lax.broadcasted_iota(jnp.int32, (2 * c, 2 * c), 1)
    same = (r2 < c) == (c2 < c)
    strict = jnp.logical_and(same, r2 > c2)
    incl = jnp.logical_and(same, r2 >= c2)
    rc = lax.broadcasted_iota(jnp.int32, (c, c), 0)
    cc = lax.broadcasted_iota(jnp.int32, (c, c), 1)
    tri_incl = jnp.where(rc >= cc, 1.0, 0.0).astype(BF16)
    lane = lax.broadcasted_iota(jnp.int32, (1, LANES), 1)
    m_a = lane < RWKV_HEAD_DIM
    hr = lax.broadcasted_iota(jnp.int32, (LANES, LANES), 0) < RWKV_HEAD_DIM
    hc = lax.broadcasted_iota(jnp.int32, (LANES, LANES), 1) < RWKV_HEAD_DIM
    head_ones = jnp.where(hr == hc, 1.0, 0.0).astype(BF16)
    inv_n = 1.0 / RWKV_HEAD_DIM

    for p in range(pairs):
        sl = slice(p * LANES, (p + 1) * LANES)
        r = r_ref[:, sl]
        k = k_ref[:, sl]
        v = v_ref[:, sl]
        y, s_new = _rwkv_pair_chunk(r, lw_ref[:, sl], k, v, kk_ref[:, sl], b_ref[:, sl],
                                    s_ref[p], tri_incl, strict, incl, m_a)
        s_ref[p] = s_new
        mean = _dot_exact_rhs(y, head_ones) * inv_n
        yc = y - mean
        var = _dot_exact_rhs(yc * yc, head_ones) * inv_n
        yn = yc * lax.rsqrt(var + RWKV_GN_EPS) * lnw_ref[:, sl] + lnb_ref[:, sl]
        bonus = _dot_exact_rhs(r * k * rk_ref[:, sl], head_ones) * v
        o_ref[:, sl] = ((yn + bonus) * g_ref[:, sl]).astype(o_ref.dtype)


def rwkv_recurrence(r, lw, k, v, kk, beta, g, ln_w, ln_b, r_k, batch, seq, pairs=4):
    m, width = r.shape
    c = min(CHUNK, seq)
    tc = pairs * LANES
    nchunk = seq // c
    blk = pl.BlockSpec((c, tc), lambda b, j, ci: (b * nchunk + ci, j))
    vec = pl.BlockSpec((1, tc), lambda b, j, ci: (0, j))
    return pl.pallas_call(
        functools.partial(_rwkv_kernel, pairs=pairs),
        grid=(batch, width // tc, nchunk),
        in_specs=[blk] * 7 + [vec] * 3,
        out_specs=blk,
        out_shape=jax.ShapeDtypeStruct((m, width), BF16),
        scratch_shapes=[pltpu.VMEM((pairs, LANES, LANES), F32)],
        compiler_params=_cparams(("parallel", "parallel", "arbitrary")),
        name="rwkv_recurrence",
    )(r, lw, k, v, kk, beta, g, ln_w, ln_b, r_k)


def _xattn_kernel(q_ref, k_ref, v_ref, o_ref, *, scale):
    s = _dg(q_ref[...], k_ref[...], NT) * scale
    m = jnp.max(s, axis=-1, keepdims=True)
    p = jnp.exp(s - m)
    l = jnp.sum(p, axis=-1, keepdims=True)
    o = jnp.dot(p.astype(BF16), v_ref[...], preferred_element_type=F32)
    o_ref[...] = (o / l).astype(o_ref.dtype)


def cross_attention(q, k, v, batch, seq, n_mem, heads, tq=512):
    m, d = q.shape
    dh = d // heads
    tq = min(tq, seq)
    nq = seq // tq
    return pl.pallas_call(
        functools.partial(_xattn_kernel, scale=dh ** -0.5),
        grid=(batch, heads, nq),
        in_specs=[pl.BlockSpec((tq, dh), lambda b, h, qi: (b * nq + qi, h)),
                  pl.BlockSpec((n_mem, dh), lambda b, h, qi: (b, h)),
                  pl.BlockSpec((n_mem, dh), lambda b, h, qi: (b, h))],
        out_specs=pl.BlockSpec((tq, dh), lambda b, h, qi: (b * nq + qi, h)),
        out_shape=jax.ShapeDtypeStruct((m, d), BF16),
        compiler_params=_cparams(("parallel", "parallel", "arbitrary")),
        name="cross_attention",
    )(q, k, v)


def _pad_cols(w, n):
    return jnp.pad(w, ((0, 0), (0, n - w.shape[1])))


def _pad_rows(w, n):
    return jnp.pad(w, ((0, n - w.shape[0]), (0, 0)))


def _bf(w):
    return w.astype(BF16)


def _row(t):
    return t.reshape(1, -1).astype(F32)


def _mixer(h, p, batch, seq):
    fox_w, rw = FOX_WIDTH, RWKV_WIDTH
    fox_heads = fox_w // FOX_HEAD_DIM
    bf, row = _bf, _row

    w_in = p['w_in']
    o_f = 3 * fox_w
    o_r = o_f + fox_heads
    o_w = o_r + 3 * rw
    o_a = o_w + DECAY_LORA
    o_g = o_a + ICLR_LORA
    w_qkv = bf(w_in[:, :o_f])
    w_rkv = bf(w_in[:, o_r:o_w])
    w_small = bf(jnp.concatenate([
        _pad_cols(w_in[:, o_w:o_a], LANES), _pad_cols(w_in[:, o_a:o_g], LANES),
        w_in[:, o_g:o_g + GATE_LORA], _pad_cols(w_in[:, o_f:o_r], LANES)], axis=1))
    mu = p['rwkv_mu']
    mu_small = jnp.concatenate([
        jnp.pad(mu[3 * rw:3 * rw + DECAY_LORA], (0, LANES - DECAY_LORA)),
        jnp.pad(mu[3 * rw + DECAY_LORA:3 * rw + DECAY_LORA + ICLR_LORA], (0, LANES - ICLR_LORA)),
        mu[3 * rw + DECAY_LORA + ICLR_LORA:], jnp.zeros((LANES,), F32)])

    qkv = matmul(h, w_qkv, BF16)
    rkv = matmul(h, w_rkv, F32)
    small = matmul(h, w_small, F32, tn=w_small.shape[1])

    f_bias = jnp.pad(p['fox_f_bias'], (0, LANES - fox_heads)).reshape(1, LANES)
    c = fgate_cumsum(small, f_bias, f_block=(2 * LANES + GATE_LORA) // LANES, seq=seq)
    c = c[:, :fox_heads].reshape(batch, seq, fox_heads).transpose(0, 2, 1)
    y_fox = fox_attention(qkv, c[..., None], c[:, :, None, :], batch, seq, fox_heads)

    r, lw, k, v, kk, beta, g = rwkv_prep(
        rkv, small, row(mu[:rw]), row(mu[rw:2 * rw]), row(mu[2 * rw:3 * rw]), row(mu_small),
        row(p['rwkv_w0']), _pad_rows(p['rwkv_w_up'], LANES), row(p['rwkv_a0']),
        _pad_rows(p['rwkv_a_up'], LANES), p['rwkv_g_up'], row(p['rwkv_k_k']),
        row(p['rwkv_k_a']), seq=seq)
    y_rwkv = rwkv_recurrence(r, lw, k, v, kk, beta, g, row(p['rwkv_ln_w']),
                             row(p['rwkv_ln_b']), row(p['rwkv_r_k']), batch, seq)

    return matmul(jnp.concatenate([y_fox, y_rwkv], axis=1), bf(p['w_out']), F32)


def _xattn(h, mn, p, batch, seq, n_mem):
    bf = _bf
    q = matmul(h, bf(p['xattn_wq']), BF16)
    km = matmul(mn, bf(p['xattn_wk']), BF16)
    vm = matmul(mn, bf(p['xattn_wv']), BF16)
    o = cross_attention(q, km, vm, batch, seq, n_mem, XATTN_HEADS)
    return matmul(o, bf(p['xattn_wo']), F32)


def _layer(x, mem, p, batch, seq, n_mem):
    bf = _bf

    h = rmsnorm(x, p['ffn1_pre_g'])
    y = ffn(h, bf(p['ffn1_w_gate']), bf(p['ffn1_w_up']), bf(p['ffn1_w_down']))
    x, h = resnorm(x, y, p['ffn1_post_g'], 0.5, p['mix_pre_g'])

    mixed = _mixer(h, p, batch, seq)
    x, h = resnorm(x, mixed, p['mix_post_g'], 1.0, p['xattn_pre_g'])

    mn = rmsnorm(mem, p['mem_norm_g'])
    xa = _xattn(h, mn, p, batch, seq, n_mem)
    x, h = resnorm(x, xa, p['xattn_post_g'], 1.0, p['ffn2_pre_g'])

    y = ffn(h, bf(p['ffn2_w_gate']), bf(p['ffn2_w_up']), bf(p['ffn2_w_down']))
    return resnorm(x, y, p['ffn2_post_g'], 0.5)


_PARAM_NAMES = (
    'ffn1_pre_g', 'ffn1_w_gate', 'ffn1_w_up', 'ffn1_w_down', 'ffn1_post_g',
    'mix_pre_g', 'w_in', 'fox_f_bias', 'rwkv_mu', 'rwkv_w0', 'rwkv_w_up', 'rwkv_a0',
    'rwkv_a_up', 'rwkv_g_up', 'rwkv_k_k', 'rwkv_k_a', 'rwkv_r_k', 'rwkv_ln_w', 'rwkv_ln_b',
    'w_out', 'mix_post_g', 'xattn_pre_g', 'mem_norm_g', 'xattn_wq', 'xattn_wk', 'xattn_wv',
    'xattn_wo', 'xattn_post_g', 'ffn2_pre_g', 'ffn2_w_gate', 'ffn2_w_up', 'ffn2_w_down',
    'ffn2_post_g')


def kernel(x, mem, ffn1_pre_g, ffn1_w_gate, ffn1_w_up, ffn1_w_down, ffn1_post_g, mix_pre_g, w_in, fox_f_bias, rwkv_mu, rwkv_w0, rwkv_w_up, rwkv_a0, rwkv_a_up, rwkv_g_up, rwkv_k_k, rwkv_k_a, rwkv_r_k, rwkv_ln_w, rwkv_ln_b, w_out, mix_post_g, xattn_pre_g, mem_norm_g, xattn_wq, xattn_wk, xattn_wv, xattn_wo, xattn_post_g, ffn2_pre_g, ffn2_w_gate, ffn2_w_up, ffn2_w_down, ffn2_post_g):
    weights = (ffn1_pre_g, ffn1_w_gate, ffn1_w_up, ffn1_w_down, ffn1_post_g, mix_pre_g, w_in,
               fox_f_bias, rwkv_mu, rwkv_w0, rwkv_w_up, rwkv_a0, rwkv_a_up, rwkv_g_up, rwkv_k_k,
               rwkv_k_a, rwkv_r_k, rwkv_ln_w, rwkv_ln_b, w_out, mix_post_g, xattn_pre_g,
               mem_norm_g, xattn_wq, xattn_wk, xattn_wv, xattn_wo, xattn_post_g, ffn2_pre_g,
               ffn2_w_gate, ffn2_w_up, ffn2_w_down, ffn2_post_g)
    batch, seq, d = x.shape
    n_mem = mem.shape[1]
    depth = ffn1_pre_g.shape[0]
    xf = x.reshape(batch * seq, d)
    mf = mem.reshape(batch * n_mem, d)
    for l in range(depth):
        p = {name: w[l] for name, w in zip(_PARAM_NAMES, weights)}
        xf = _layer(xf, mf, p, batch, seq, n_mem)
    return xf.reshape(batch, seq, d)
```

```python
import functools

import jax
import jax.numpy as jnp
from jax import lax
from jax.experimental import pallas as pl
from jax.experimental.pallas import tpu as pltpu

F32 = jnp.float32
BF16 = jnp.bfloat16

SEQ = 2048
FOX_HEAD_DIM = 128
RWKV_HEAD_DIM = 64
RWKV_WIDTH = 2048
FOX_WIDTH = 2048
DECAY_LORA = 96
ICLR_LORA = 96
GATE_LORA = 256
XATTN_HEADS = 4
RMS_EPS = 1e-6
RWKV_GN_EPS = 64e-5
NEG_INF = -1e30

LANES = 128
CHUNK = 64
VMEM_LIMIT = 56 * 1024 * 1024


def _cparams(sem, vmem=VMEM_LIMIT):
    return pltpu.CompilerParams(dimension_semantics=sem, vmem_limit_bytes=vmem)


NN = (((1,), (0,)), ((), ()))
NT = (((1,), (1,)), ((), ()))
TN = (((0,), (0,)), ((), ()))


def _split2(x):
    hi = x.astype(BF16)
    lo = (x - hi.astype(F32)).astype(BF16)
    return hi, lo


def _split3(x):
    hi = x.astype(BF16)
    r = x - hi.astype(F32)
    mid = r.astype(BF16)
    lo = (r - mid.astype(F32)).astype(BF16)
    return hi, mid, lo


def _dg(a, b, dims):
    return lax.dot_general(a, b, dims, preferred_element_type=F32)


def _dot3(a, b, dims=NN):
    ah, al = _split2(a)
    bh, bl = _split2(b)
    return _dg(ah, bh, dims) + (_dg(ah, bl, dims) + _dg(al, bh, dims))


def _dot_exact_lhs(a_bf16, b, dims=NN):
    b0, b1, b2 = _split3(b)
    return _dg(a_bf16, b0, dims) + (_dg(a_bf16, b1, dims) + _dg(a_bf16, b2, dims))


def _dot_exact_rhs(a, b_bf16, dims=NN):
    a0, a1, a2 = _split3(a)
    return _dg(a0, b_bf16, dims) + (_dg(a1, b_bf16, dims) + _dg(a2, b_bf16, dims))


def _softplus(x):
    return jnp.maximum(x, 0.0) + jnp.log1p(jnp.exp(-jnp.abs(x)))


def _rmsnorm_kernel(x_ref, g_ref, o_ref):
    x = x_ref[...]
    ms = jnp.mean(x * x, axis=-1, keepdims=True)
    o_ref[...] = (x * lax.rsqrt(ms + RMS_EPS) * g_ref[...]).astype(o_ref.dtype)


def rmsnorm(x, g, tm=256):
    m, d = x.shape
    tm = min(tm, m)
    return pl.pallas_call(
        _rmsnorm_kernel,
        grid=(m // tm,),
        in_specs=[pl.BlockSpec((tm, d), lambda i: (i, 0)),
                  pl.BlockSpec((1, d), lambda i: (0, 0))],
        out_specs=pl.BlockSpec((tm, d), lambda i: (i, 0)),
        out_shape=jax.ShapeDtypeStruct((m, d), BF16),
        compiler_params=_cparams(("parallel",)),
        name="rmsnorm",
    )(x, g.reshape(1, d))


def _resnorm_kernel(x_ref, y_ref, gp_ref, gn_ref, xo_ref, h_ref, *, scale):
    y = y_ref[...]
    ms = jnp.mean(y * y, axis=-1, keepdims=True)
    xn = x_ref[...] + scale * (y * lax.rsqrt(ms + RMS_EPS) * gp_ref[...])
    xo_ref[...] = xn
    ms2 = jnp.mean(xn * xn, axis=-1, keepdims=True)
    h_ref[...] = (xn * lax.rsqrt(ms2 + RMS_EPS) * gn_ref[...]).astype(h_ref.dtype)


def _resnorm_last_kernel(x_ref, y_ref, gp_ref, xo_ref, *, scale):
    y = y_ref[...]
    ms = jnp.mean(y * y, axis=-1, keepdims=True)
    xo_ref[...] = x_ref[...] + scale * (y * lax.rsqrt(ms + RMS_EPS) * gp_ref[...])


def resnorm(x, y, g_post, scale, g_next=None, tm=256):
    m, d = x.shape
    row = pl.BlockSpec((tm, d), lambda i: (i, 0))
    vec = pl.BlockSpec((1, d), lambda i: (0, 0))
    if g_next is None:
        return pl.pallas_call(
            functools.partial(_resnorm_last_kernel, scale=scale),
            grid=(m // tm,),
            in_specs=[row, row, vec],
            out_specs=row,
            out_shape=jax.ShapeDtypeStruct((m, d), F32),
            compiler_params=_cparams(("parallel",)),
            name="resnorm_last",
        )(x, y, g_post.reshape(1, d))
    return pl.pallas_call(
        functools.partial(_resnorm_kernel, scale=scale),
        grid=(m // tm,),
        in_specs=[row, row, vec, vec],
        out_specs=[row, row],
        out_shape=[jax.ShapeDtypeStruct((m, d), F32),
                   jax.ShapeDtypeStruct((m, d), BF16)],
        compiler_params=_cparams(("parallel",)),
        name="resnorm",
    )(x, y, g_post.reshape(1, d), g_next.reshape(1, d))


def _mm_kernel(a_ref, w_ref, o_ref):
    o_ref[...] = jnp.dot(a_ref[...], w_ref[...],
                         preferred_element_type=F32).astype(o_ref.dtype)


def matmul(a, w, out_dtype, tm=1024, tn=1024):
    m, k = a.shape
    _, n = w.shape
    tm = min(tm, m)
    tn = min(tn, n)
    assert m % tm == 0 and n % tn == 0
    return pl.pallas_call(
        _mm_kernel,
        grid=(m // tm, n // tn),
        in_specs=[pl.BlockSpec((tm, k), lambda i, j: (i, 0)),
                  pl.BlockSpec((k, tn), lambda i, j: (0, j))],
        out_specs=pl.BlockSpec((tm, tn), lambda i, j: (i, j)),
        out_shape=jax.ShapeDtypeStruct((m, n), out_dtype),
        compiler_params=_cparams(("parallel", "arbitrary")),
        name="matmul",
    )(a, w)


def _ffn_kernel(h_ref, wg_ref, wu_ref, wd_ref, o_ref, *, n_chunk):
    f = pl.program_id(1)

    @pl.when(f == 0)
    def _():
        o_ref[...] = jnp.zeros_like(o_ref)

    h = h_ref[...]
    g = jnp.dot(h, wg_ref[...], preferred_element_type=F32)
    u = jnp.dot(h, wu_ref[...], preferred_element_type=F32)
    act = (g * jax.nn.sigmoid(g) * u).astype(BF16)
    d = o_ref.shape[1]
    for c in range(0, d, n_chunk):
        o_ref[:, c:c + n_chunk] += jnp.dot(act, wd_ref[:, c:c + n_chunk],
                                           preferred_element_type=F32)


def ffn(h, wg, wu, wd, tm=1024, tf=256, n_chunk=512):
    m, d = h.shape
    dff = wg.shape[1]
    tm = min(tm, m)
    tf = min(tf, dff)
    n_chunk = min(n_chunk, d)
    assert m % tm == 0 and dff % tf == 0 and d % n_chunk == 0
    return pl.pallas_call(
        functools.partial(_ffn_kernel, n_chunk=n_chunk),
        grid=(m // tm, dff // tf),
        in_specs=[pl.BlockSpec((tm, d), lambda i, f: (i, 0), pipeline_mode=pl.Buffered(1)),
                  pl.BlockSpec((d, tf), lambda i, f: (0, f)),
                  pl.BlockSpec((d, tf), lambda i, f: (0, f)),
                  pl.BlockSpec((tf, d), lambda i, f: (f, 0))],
        out_specs=pl.BlockSpec((tm, d), lambda i, f: (i, 0), pipeline_mode=pl.Buffered(1)),
        out_shape=jax.ShapeDtypeStruct((m, d), F32),
        compiler_params=_cparams(("parallel", "arbitrary")),
        name="ffn",
    )(h, wg, wu, wd)


def _fgate_kernel(f_ref, b_ref, c_ref, *, blk):
    s = f_ref.shape[0]
    row = lax.broadcasted_iota(jnp.int32, (blk, blk), 0)
    col = lax.broadcasted_iota(jnp.int32, (blk, blk), 1)
    tri = jnp.where(row >= col, 1.0, 0.0).astype(BF16)
    carry = jnp.zeros((1, f_ref.shape[1]), F32)
    for r0 in range(0, s, blk):
        x = f_ref[r0:r0 + blk, :] + b_ref[...]
        logf = -_softplus(-x)
        c = _dot_exact_lhs(tri, logf) + carry
        c_ref[r0:r0 + blk, :] = c
        carry = c[blk - 1:blk, :]


def fgate_cumsum(small, f_bias_row, f_block, seq, blk=256):
    m = small.shape[0]
    blk = min(blk, seq)
    return pl.pallas_call(
        functools.partial(_fgate_kernel, blk=blk),
        grid=(m // seq,),
        in_specs=[pl.BlockSpec((seq, LANES), lambda b: (b, f_block)),
                  pl.BlockSpec((1, LANES), lambda b: (0, 0))],
        out_specs=pl.BlockSpec((seq, LANES), lambda b: (b, 0)),
        out_shape=jax.ShapeDtypeStruct((m, LANES), F32),
        compiler_params=_cparams(("parallel",)),
        name="fgate_cumsum",
    )(small, f_bias_row)


def _fox_kernel(q_ref, k_ref, v_ref, cq_ref, ck_ref, o_ref, m_ref, l_ref, acc_ref,
                *, tq, tk, scale):
    qi = pl.program_id(2)
    ki = pl.program_id(3)

    @pl.when(ki == 0)
    def _():
        m_ref[...] = jnp.full_like(m_ref, NEG_INF)
        l_ref[...] = jnp.zeros_like(l_ref)
        acc_ref[...] = jnp.zeros_like(acc_ref)

    @pl.when(ki <= qi)
    def _():
        s = _dg(q_ref[...], k_ref[...], NT) * scale
        s = s + cq_ref[...] - ck_ref[...]
        row = lax.broadcasted_iota(jnp.int32, (tq, tk), 0) + qi * tq
        col = lax.broadcasted_iota(jnp.int32, (tq, tk), 1) + ki * tk
        s = jnp.where(row >= col, s, NEG_INF)
        m_prev = m_ref[...]
        m_new = jnp.maximum(m_prev, jnp.max(s, axis=-1, keepdims=True))
        alpha = jnp.exp(m_prev - m_new)
        p = jnp.exp(s - m_new)
        l_ref[...] = alpha * l_ref[...] + jnp.sum(p, axis=-1, keepdims=True)
        acc_ref[...] = alpha * acc_ref[...] + jnp.dot(
            p.astype(BF16), v_ref[...], preferred_element_type=F32)
        m_ref[...] = m_new

    @pl.when(ki == qi)
    def _():
        o_ref[...] = (acc_ref[...] / l_ref[...]).astype(o_ref.dtype)


def fox_attention(qkv, c_col, c_row, batch, seq, heads, tq=512):
    tq = min(tq, seq)
    tk = tq
    nq = seq // tq
    dh = FOX_HEAD_DIM
    kern = functools.partial(_fox_kernel, tq=tq, tk=tk, scale=dh ** -0.5)
    return pl.pallas_call(
        kern,
        grid=(batch, heads, nq, nq),
        in_specs=[
            pl.BlockSpec((tq, dh), lambda b, h, qi, ki: (b * nq + qi, h)),
            pl.BlockSpec((tk, dh), lambda b, h, qi, ki: (b * nq + jnp.minimum(ki, qi), heads + h)),
            pl.BlockSpec((tk, dh), lambda b, h, qi, ki: (b * nq + jnp.minimum(ki, qi), 2 * heads + h)),
            pl.BlockSpec((None, None, tq, 1), lambda b, h, qi, ki: (b, h, qi, 0)),
            pl.BlockSpec((None, None, 1, tk), lambda b, h, qi, ki: (b, h, 0, jnp.minimum(ki, qi))),
        ],
        out_specs=pl.BlockSpec((tq, dh), lambda b, h, qi, ki: (b * nq + qi, h)),
        out_shape=jax.ShapeDtypeStruct((batch * seq, heads * dh), BF16),
        scratch_shapes=[pltpu.VMEM((tq, 1), F32), pltpu.VMEM((tq, 1), F32),
                        pltpu.VMEM((tq, dh), F32)],
        compiler_params=_cparams(("parallel", "parallel", "arbitrary", "arbitrary")),
        name="fox_attention",
    )(qkv, qkv, qkv, c_col, c_row)


def _rwkv_prep_kernel(r_ref, k_ref, v_ref, rp_ref, kp_ref, vp_ref, sm_ref, smp_ref,
                      mu_r_ref, mu_k_ref, mu_v_ref, mu_s_ref,
                      w0_ref, wup_ref, a0_ref, aup_ref, gup_ref, kk_ref, ka_ref,
                      r_o, lw_o, k_o, v_o, kk_o, b_o, g_o, *, tiles_per_seq):
    i = pl.program_id(0)
    first = (i % tiles_per_seq) == 0
    tm = r_ref.shape[0]
    row0 = lax.broadcasted_iota(jnp.int32, (tm, 1), 0) == 0

    def mix(cur_ref, prev_ref, mu_ref):
        cur = cur_ref[...]
        last = jnp.where(first, 0.0, prev_ref[7:8, :])
        prev = jnp.where(row0, last, pltpu.roll(cur, 1, axis=0))
        return cur + mu_ref[...] * (prev - cur)

    zr = mix(r_ref, rp_ref, mu_r_ref)
    zk = mix(k_ref, kp_ref, mu_k_ref)
    zv = mix(v_ref, vp_ref, mu_v_ref)
    zs = mix(sm_ref, smp_ref, mu_s_ref)
    zw = zs[:, 0:LANES]
    za = zs[:, LANES:2 * LANES]
    zg = zs[:, 2 * LANES:2 * LANES + GATE_LORA]

    w_log = -_softplus(-(w0_ref[...] + _dot3(jnp.tanh(zw), wup_ref[...]))) - 0.5
    lw_o[...] = -jnp.exp(w_log)
    a = jax.nn.sigmoid(a0_ref[...] + _dot3(za, aup_ref[...]))
    g_o[...] = _dot3(jax.nn.sigmoid(zg), gup_ref[...])

    tc = zk.shape[1]
    hr = lax.broadcasted_iota(jnp.int32, (tc, tc), 0) // RWKV_HEAD_DIM
    hc = lax.broadcasted_iota(jnp.int32, (tc, tc), 1) // RWKV_HEAD_DIM
    head_ones = jnp.where(hr == hc, 1.0, 0.0).astype(BF16)
    kk = zk * kk_ref[...]
    ss = _dot_exact_rhs(kk * kk, head_ones)
    kk = kk / jnp.maximum(jnp.sqrt(ss), 1e-12)
    r_o[...] = zr
    v_o[...] = zv
    kk_o[...] = kk
    b_o[...] = kk * a
    k_o[...] = zk * (1.0 + (a - 1.0) * ka_ref[...])


def rwkv_prep(rkv, small, mu_r, mu_k, mu_v, mu_s, w0, w_up, a0, a_up, g_up, k_k, k_a,
              seq, tm=512, tc=512):
    m = rkv.shape[0]
    width = RWKV_WIDTH
    tm = min(tm, seq)
    nc = width // tc
    ns = small.shape[1]
    tps = seq // tm
    sub = tm // 8

    def cur(piece):
        return pl.BlockSpec((tm, tc), lambda i, j: (i, piece * nc + j))

    def prev(piece):
        return pl.BlockSpec((8, tc), lambda i, j: (jnp.maximum(i * sub - 1, 0), piece * nc + j))

    colv = pl.BlockSpec((1, tc), lambda i, j: (0, j))
    out = pl.BlockSpec((tm, tc), lambda i, j: (i, j))
    outs = [jax.ShapeDtypeStruct((m, width), F32)] * 7
    return pl.pallas_call(
        functools.partial(_rwkv_prep_kernel, tiles_per_seq=tps),
        grid=(m // tm, nc),
        in_specs=[cur(0), cur(1), cur(2), prev(0), prev(1), prev(2),
                  pl.BlockSpec((tm, ns), lambda i, j: (i, 0)),
                  pl.BlockSpec((8, ns), lambda i, j: (jnp.maximum(i * sub - 1, 0), 0)),
                  colv, colv, colv,
                  pl.BlockSpec((1, ns), lambda i, j: (0, 0)),
                  colv,
                  pl.BlockSpec((LANES, tc), lambda i, j: (0, j)),
                  colv,
                  pl.BlockSpec((LANES, tc), lambda i, j: (0, j)),
                  pl.BlockSpec((GATE_LORA, tc), lambda i, j: (0, j)),
                  colv, colv],
        out_specs=[out] * 7,
        out_shape=outs,
        compiler_params=_cparams(("parallel", "arbitrary")),
        name="rwkv_prep",
    )(rkv, rkv, rkv, rkv, rkv, rkv, small, small,
      mu_r, mu_k, mu_v, mu_s, w0, w_up, a0, a_up, g_up, k_k, k_a)


def _rwkv_kernel(r_ref, lw_ref, k_ref, v_ref, kk_ref, b_ref, g_ref,
                 lnw_ref, lnb_ref, rk_ref, o_ref, s_ref, *, pairs):
    ci = pl.program_id(2)
    c = CHUNK
    hd = RWKV_HEAD_DIM

    @pl.when(ci == 0)
    def _():
        s_ref[...] = jnp.zeros_like(s_ref)

    r2 = lax.broadcasted_iota(jnp.int32, (LANES, LANES), 0)
    c2 = lax.broadcasted_iota(jnp.int32, (LANES, LANES), 1)
    same = (r2 < c) == (c2 < c)
    strict = jnp.logical_and(same, r2 > c2)
    incl = jnp.logical_and(same, r2 >= c2)
    head_ones = jnp.where((r2 < hd) == (c2 < hd), 1.0, 0.0).astype(BF16)
    rc = lax.broadcasted_iota(jnp.int32, (c, c), 0)
    cc = lax.broadcasted_iota(jnp.int32, (c, c), 1)
    tri_incl = jnp.where(rc >= cc, 1.0, 0.0).astype(BF16)
    m_a = lax.broadcasted_iota(jnp.int32, (1, LANES), 1) < hd
    inv_n = 1.0 / hd
    prs = range(pairs)
    cat = lambda xs: jnp.concatenate(xs, axis=0)

    def lanes(x, p):
        return x[:, p * LANES:(p + 1) * LANES]

    def stack2(x):
        return cat([jnp.where(m_a, x, 0.0), jnp.where(m_a, 0.0, x)])

    lw = lw_ref[...]
    r, k, v = r_ref[...], k_ref[...], v_ref[...]
    kk, beta = kk_ref[...], b_ref[...]
    big_l = _dot_exact_lhs(tri_incl, lw)
    l_tot = big_l[c - 1:c, :]
    e_nl = jnp.exp(-big_l)
    e_rem = jnp.exp(l_tot - big_l)
    e_tot = jnp.exp(l_tot)
    at = -kk * jnp.exp(big_l - lw)
    rt = r * jnp.exp(big_l)
    kt = k * e_nl
    bt = beta * e_nl
    bend = beta * e_rem
    kend = k * e_rem

    s0 = [s_ref[p] for p in prs]
    lhs = [cat([stack2(lanes(at, p)), stack2(lanes(rt, p))]).astype(BF16) for p in prs]
    wbk = [cat([lanes(bt, p), lanes(kt, p)]).astype(BF16) for p in prs]
    v2 = [stack2(lanes(v, p)).astype(BF16) for p in prs]
    m = [_dg(lhs[p], wbk[p], NT) for p in prs]
    sx = [_dg(lhs[p], s0[p].astype(BF16), NT) for p in prs]

    a_ab, a_kr, a_rb = [], [], []
    for p in prs:
        mp = m[p]
        sw = pltpu.roll(mp, c, axis=1)
        a_ab.append(jnp.where(strict, cat([mp[0:c], sw[c:2 * c]]), 0.0))
        a_ak = jnp.where(strict, cat([sw[0:c], mp[c:2 * c]]), 0.0)
        a_rb.append(jnp.where(incl, cat([mp[2 * c:3 * c], sw[3 * c:4 * c]]), 0.0))
        a_rk = jnp.where(incl, cat([sw[2 * c:3 * c], mp[3 * c:4 * c]]), 0.0)
        a_kr.append(cat([a_ak, a_rk]).astype(BF16))
    xy = [_dg(a_kr[p], v2[p], NN) for p in prs]
    u = [sx[p][:2 * c] + xy[p][:2 * c] for p in prs]
    ypart = [sx[p][2 * c:] + xy[p][2 * c:] for p in prs]

    pw = a_ab
    n_sq = (c - 1).bit_length()
    for it in range(n_sq):
        u = [u[p] + _dot3(pw[p], u[p]) for p in prs]
        if it + 1 < n_sq:
            pw = [_dot3(pw[p], pw[p]) for p in prs]

    ub = [u[p].astype(BF16) for p in prs]
    y2 = [ypart[p] + _dg(a_rb[p].astype(BF16), ub[p], NN) for p in prs]
    for p in prs:
        ends = cat([stack2(lanes(bend, p)), stack2(lanes(kend, p))]).astype(BF16)
        s_ref[p] = s0[p] * lanes(e_tot, p) + _dg(cat([ub[p], v2[p]]), ends, TN)

    y_all = cat([y2[p][:c] + y2[p][c:] for p in prs])
    mean = _dot_exact_rhs(y_all, head_ones) * inv_n
    yc = y_all - mean
    var = _dot_exact_rhs(yc * yc, head_ones) * inv_n
    yn = yc * lax.rsqrt(var + RWKV_GN_EPS)
    rkk = r * k * rk_ref[...]
    bonus = _dot_exact_rhs(cat([lanes(rkk, p) for p in prs]), head_ones)
    lnw, lnb, g = lnw_ref[...], lnb_ref[...], g_ref[...]
    for p in prs:
        rows = slice(p * c, (p + 1) * c)
        out = (yn[rows] * lanes(lnw, p) + lanes(lnb, p) + bonus[rows] * lanes(v, p)) * lanes(g, p)
        o_ref[:, p * LANES:(p + 1) * LANES] = out.astype(o_ref.dtype)


def rwkv_recurrence(r, lw, k, v, kk, beta, g, ln_w, ln_b, r_k, batch, seq, pairs=8):
    m, width = r.shape
    c = CHUNK
    assert seq % c == 0 and 2 * c == LANES and 2 * RWKV_HEAD_DIM == LANES
    tc = pairs * LANES
    nchunk = seq // c
    blk = pl.BlockSpec((c, tc), lambda b, j, ci: (b * nchunk + ci, j))
    vec = pl.BlockSpec((1, tc), lambda b, j, ci: (0, j))
    return pl.pallas_call(
        functools.partial(_rwkv_kernel, pairs=pairs),
        grid=(batch, width // tc, nchunk),
        in_specs=[blk] * 7 + [vec] * 3,
        out_specs=blk,
        out_shape=jax.ShapeDtypeStruct((m, width), BF16),
        scratch_shapes=[pltpu.VMEM((pairs, LANES, LANES), F32)],
        compiler_params=_cparams(("parallel", "parallel", "arbitrary")),
        name="rwkv_recurrence",
    )(r, lw, k, v, kk, beta, g, ln_w, ln_b, r_k)


def _xattn_kernel(q_ref, k_ref, v_ref, o_ref, *, scale):
    s = _dg(q_ref[...], k_ref[...], NT) * scale
    m = jnp.max(s, axis=-1, keepdims=True)
    p = jnp.exp(s - m)
    l = jnp.sum(p, axis=-1, keepdims=True)
    o = jnp.dot(p.astype(BF16), v_ref[...], preferred_element_type=F32)
    o_ref[...] = (o / l).astype(o_ref.dtype)


def cross_attention(q, k, v, batch, seq, n_mem, heads, tq=512):
    m, d = q.shape
    dh = d // heads
    tq = min(tq, seq)
    nq = seq // tq
    return pl.pallas_call(
        functools.partial(_xattn_kernel, scale=dh ** -0.5),
        grid=(batch, heads, nq),
        in_specs=[pl.BlockSpec((tq, dh), lambda b, h, qi: (b * nq + qi, h)),
                  pl.BlockSpec((n_mem, dh), lambda b, h, qi: (b, h)),
                  pl.BlockSpec((n_mem, dh), lambda b, h, qi: (b, h))],
        out_specs=pl.BlockSpec((tq, dh), lambda b, h, qi: (b * nq + qi, h)),
        out_shape=jax.ShapeDtypeStruct((m, d), BF16),
        compiler_params=_cparams(("parallel", "parallel", "arbitrary")),
        name="cross_attention",
    )(q, k, v)


def _pad_cols(w, n):
    return jnp.pad(w, ((0, 0), (0, n - w.shape[1])))


def _pad_rows(w, n):
    return jnp.pad(w, ((0, n - w.shape[0]), (0, 0)))


def _bf(w):
    return w.astype(BF16)


def _row(t):
    return t.reshape(1, -1).astype(F32)


def _mixer(h, p, batch, seq):
    fox_w, rw = FOX_WIDTH, RWKV_WIDTH
    fox_heads = fox_w // FOX_HEAD_DIM
    bf, row = _bf, _row

    w_in = p['w_in']
    o_f = 3 * fox_w
    o_r = o_f + fox_heads
    o_w = o_r + 3 * rw
    o_a = o_w + DECAY_LORA
    o_g = o_a + ICLR_LORA
    w_qkv = bf(w_in[:, :o_f])
    w_rkv = bf(w_in[:, o_r:o_w])
    w_small = bf(jnp.concatenate([
        _pad_cols(w_in[:, o_w:o_a], LANES), _pad_cols(w_in[:, o_a:o_g], LANES),
        w_in[:, o_g:o_g + GATE_LORA], _pad_cols(w_in[:, o_f:o_r], LANES)], axis=1))
    mu = p['rwkv_mu']
    mu_small = jnp.concatenate([
        jnp.pad(mu[3 * rw:3 * rw + DECAY_LORA], (0, LANES - DECAY_LORA)),
        jnp.pad(mu[3 * rw + DECAY_LORA:3 * rw + DECAY_LORA + ICLR_LORA], (0, LANES - ICLR_LORA)),
        mu[3 * rw + DECAY_LORA + ICLR_LORA:], jnp.zeros((LANES,), F32)])

    qkv = matmul(h, w_qkv, BF16)
    rkv = matmul(h, w_rkv, F32)
    small = matmul(h, w_small, F32, tn=w_small.shape[1])

    f_bias = jnp.pad(p['fox_f_bias'], (0, LANES - fox_heads)).reshape(1, LANES)
    c = fgate_cumsum(small, f_bias, f_block=(2 * LANES + GATE_LORA) // LANES, seq=seq)
    c = c[:, :fox_heads].reshape(batch, seq, fox_heads).transpose(0, 2, 1)
    y_fox = fox_attention(qkv, c[..., None], c[:, :, None, :], batch, seq, fox_heads)

    r, lw, k, v, kk, beta, g = rwkv_prep(
        rkv, small, row(mu[:rw]), row(mu[rw:2 * rw]), row(mu[2 * rw:3 * rw]), row(mu_small),
        row(p['rwkv_w0']), _pad_rows(p['rwkv_w_up'], LANES), row(p['rwkv_a0']),
        _pad_rows(p['rwkv_a_up'], LANES), p['rwkv_g_up'], row(p['rwkv_k_k']),
        row(p['rwkv_k_a']), seq=seq)
    y_rwkv = rwkv_recurrence(r, lw, k, v, kk, beta, g, row(p['rwkv_ln_w']),
                             row(p['rwkv_ln_b']), row(p['rwkv_r_k']), batch, seq)

    return matmul(jnp.concatenate([y_fox, y_rwkv], axis=1), bf(p['w_out']), F32)


def _xattn(h, mn, p, batch, seq, n_mem):
    bf = _bf
    q = matmul(h, bf(p['xattn_wq']), BF16)
    km = matmul(mn, bf(p['xattn_wk']), BF16)
    vm = matmul(mn, bf(p['xattn_wv']), BF16)
    o = cross_attention(q, km, vm, batch, seq, n_mem, XATTN_HEADS)
    return matmul(o, bf(p['xattn_wo']), F32)


def _layer(x, mem, p, batch, seq, n_mem):
    bf = _bf

    h = rmsnorm(x, p['ffn1_pre_g'])
    y = ffn(h, bf(p['ffn1_w_gate']), bf(p['ffn1_w_up']), bf(p['ffn1_w_down']))
    x, h = resnorm(x, y, p['ffn1_post_g'], 0.5, p['mix_pre_g'])

    mixed = _mixer(h, p, batch, seq)
    x, h = resnorm(x, mixed, p['mix_post_g'], 1.0, p['xattn_pre_g'])

    mn = rmsnorm(mem, p['mem_norm_g'])
    xa = _xattn(h, mn, p, batch, seq, n_mem)
    x, h = resnorm(x, xa, p['xattn_post_g'], 1.0, p['ffn2_pre_g'])

    y = ffn(h, bf(p['ffn2_w_gate']), bf(p['ffn2_w_up']), bf(p['ffn2_w_down']))
    return resnorm(x, y, p['ffn2_post_g'], 0.5)


_PARAM_NAMES = (
    'ffn1_pre_g', 'ffn1_w_gate', 'ffn1_w_up', 'ffn1_w_down', 'ffn1_post_g',
    'mix_pre_g', 'w_in', 'fox_f_bias', 'rwkv_mu', 'rwkv_w0', 'rwkv_w_up', 'rwkv_a0',
    'rwkv_a_up', 'rwkv_g_up', 'rwkv_k_k', 'rwkv_k_a', 'rwkv_r_k', 'rwkv_ln_w', 'rwkv_ln_b',
    'w_out', 'mix_post_g', 'xattn_pre_g', 'mem_norm_g', 'xattn_wq', 'xattn_wk', 'xattn_wv',
    'xattn_wo', 'xattn_post_g', 'ffn2_pre_g', 'ffn2_w_gate', 'ffn2_w_up', 'ffn2_w_down',
    'ffn2_post_g')


def kernel(x, mem, ffn1_pre_g, ffn1_w_gate, ffn1_w_up, ffn1_w_down, ffn1_post_g, mix_pre_g, w_in, fox_f_bias, rwkv_mu, rwkv_w0, rwkv_w_up, rwkv_a0, rwkv_a_up, rwkv_g_up, rwkv_k_k, rwkv_k_a, rwkv_r_k, rwkv_ln_w, rwkv_ln_b, w_out, mix_post_g, xattn_pre_g, mem_norm_g, xattn_wq, xattn_wk, xattn_wv, xattn_wo, xattn_post_g, ffn2_pre_g, ffn2_w_gate, ffn2_w_up, ffn2_w_down, ffn2_post_g):
    weights = (ffn1_pre_g, ffn1_w_gate, ffn1_w_up, ffn1_w_down, ffn1_post_g, mix_pre_g, w_in,
               fox_f_bias, rwkv_mu, rwkv_w0, rwkv_w_up, rwkv_a0, rwkv_a_up, rwkv_g_up, rwkv_k_k,
               rwkv_k_a, rwkv_r_k, rwkv_ln_w, rwkv_ln_b, w_out, mix_post_g, xattn_pre_g,
               mem_norm_g, xattn_wq, xattn_wk, xattn_wv, xattn_wo, xattn_post_g, ffn2_pre_g,
               ffn2_w_gate, ffn2_w_up, ffn2_w_down, ffn2_post_g)
    batch, seq, d = x.shape
    n_mem = mem.shape[1]
    depth = ffn1_pre_g.shape[0]
    xf = x.reshape(batch * seq, d)
    mf = mem.reshape(batch * n_mem, d)
    for l in range(depth):
        p = {name: w[l] for name, w in zip(_PARAM_NAMES, weights)}
        xf = _layer(xf, mf, p, batch, seq, n_mem)
    return xf.reshape(batch, seq, d)
```

```python
import functools

import jax
import jax.numpy as jnp
from jax import lax
from jax.experimental import pallas as pl
from jax.experimental.pallas import tpu as pltpu

F32 = jnp.float32
BF16 = jnp.bfloat16

SEQ = 2048
FOX_HEAD_DIM = 128
RWKV_HEAD_DIM = 64
RWKV_WIDTH = 2048
FOX_WIDTH = 2048
DECAY_LORA = 96
ICLR_LORA = 96
GATE_LORA = 256
XATTN_HEADS = 4
RMS_EPS = 1e-6
RWKV_GN_EPS = 64e-5
NEG_INF = -1e30

LANES = 128
CHUNK = 64
VMEM_LIMIT = 56 * 1024 * 1024


def _cparams(sem, vmem=VMEM_LIMIT):
    return pltpu.CompilerParams(dimension_semantics=sem, vmem_limit_bytes=vmem)


NN = (((1,), (0,)), ((), ()))
NT = (((1,), (1,)), ((), ()))
TN = (((0,), (0,)), ((), ()))


def _split2(x):
    hi = x.astype(BF16)
    lo = (x - hi.astype(F32)).astype(BF16)
    return hi, lo


def _split3(x):
    hi = x.astype(BF16)
    r = x - hi.astype(F32)
    mid = r.astype(BF16)
    lo = (r - mid.astype(F32)).astype(BF16)
    return hi, mid, lo


def _dg(a, b, dims):
    return lax.dot_general(a, b, dims, preferred_element_type=F32)


def _dot3(a, b, dims=NN):
    ah, al = _split2(a)
    bh, bl = _split2(b)
    return _dg(ah, bh, dims) + (_dg(ah, bl, dims) + _dg(al, bh, dims))


def _dot_exact_lhs(a_bf16, b, dims=NN):
    b0, b1, b2 = _split3(b)
    return _dg(a_bf16, b0, dims) + (_dg(a_bf16, b1, dims) + _dg(a_bf16, b2, dims))


def _dot_exact_rhs(a, b_bf16, dims=NN):
    a0, a1, a2 = _split3(a)
    return _dg(a0, b_bf16, dims) + (_dg(a1, b_bf16, dims) + _dg(a2, b_bf16, dims))


def _softplus(x):
    return jnp.maximum(x, 0.0) + jnp.log1p(jnp.exp(-jnp.abs(x)))


def _rmsnorm_kernel(x_ref, g_ref, o_ref):
    x = x_ref[...]
    ms = jnp.mean(x * x, axis=-1, keepdims=True)
    o_ref[...] = (x * lax.rsqrt(ms + RMS_EPS) * g_ref[...]).astype(o_ref.dtype)


def rmsnorm(x, g, tm=256):
    m, d = x.shape
    tm = min(tm, m)
    return pl.pallas_call(
        _rmsnorm_kernel,
        grid=(m // tm,),
        in_specs=[pl.BlockSpec((tm, d), lambda i: (i, 0)),
                  pl.BlockSpec((1, d), lambda i: (0, 0))],
        out_specs=pl.BlockSpec((tm, d), lambda i: (i, 0)),
        out_shape=jax.ShapeDtypeStruct((m, d), BF16),
        compiler_params=_cparams(("parallel",)),
        name="rmsnorm",
    )(x, g.reshape(1, d))


def _resnorm_kernel(x_ref, y_ref, gp_ref, gn_ref, xo_ref, h_ref, *, scale):
    y = y_ref[...]
    ms = jnp.mean(y * y, axis=-1, keepdims=True)
    xn = x_ref[...] + scale * (y * lax.rsqrt(ms + RMS_EPS) * gp_ref[...])
    xo_ref[...] = xn
    ms2 = jnp.mean(xn * xn, axis=-1, keepdims=True)
    h_ref[...] = (xn * lax.rsqrt(ms2 + RMS_EPS) * gn_ref[...]).astype(h_ref.dtype)


def _resnorm_last_kernel(x_ref, y_ref, gp_ref, xo_ref, *, scale):
    y = y_ref[...]
    ms = jnp.mean(y * y, axis=-1, keepdims=True)
    xo_ref[...] = x_ref[...] + scale * (y * lax.rsqrt(ms + RMS_EPS) * gp_ref[...])


def resnorm(x, y, g_post, scale, g_next=None, tm=256):
    m, d = x.shape
    row = pl.BlockSpec((tm, d), lambda i: (i, 0))
    vec = pl.BlockSpec((1, d), lambda i: (0, 0))
    if g_next is None:
        return pl.pallas_call(
            functools.partial(_resnorm_last_kernel, scale=scale),
            grid=(m // tm,),
            in_specs=[row, row, vec],
            out_specs=row,
            out_shape=jax.ShapeDtypeStruct((m, d), F32),
            compiler_params=_cparams(("parallel",)),
            name="resnorm_last",
        )(x, y, g_post.reshape(1, d))
    return pl.pallas_call(
        functools.partial(_resnorm_kernel, scale=scale),
        grid=(m // tm,),
        in_specs=[row, row, vec, vec],
        out_specs=[row, row],
        out_shape=[jax.ShapeDtypeStruct((m, d), F32),
                   jax.ShapeDtypeStruct((m, d), BF16)],
        compiler_params=_cparams(("parallel",)),
        name="resnorm",
    )(x, y, g_post.reshape(1, d), g_next.reshape(1, d))


def _mm_kernel(a_ref, w_ref, o_ref):
    o_ref[...] = jnp.dot(a_ref[...], w_ref[...],
                         preferred_element_type=F32).astype(o_ref.dtype)


def _mm_scaled_kernel(a_ref, w_ref, s_ref, o_ref):
    acc = jnp.dot(a_ref[...], w_ref[...], preferred_element_type=F32)
    o_ref[...] = (acc * s_ref[...]).astype(o_ref.dtype)


def _mm2_kernel(a1_ref, a2_ref, w1_ref, w2_ref, o_ref):
    acc = jnp.dot(a1_ref[...], w1_ref[...], preferred_element_type=F32)
    acc += jnp.dot(a2_ref[...], w2_ref[...], preferred_element_type=F32)
    o_ref[...] = acc.astype(o_ref.dtype)


def matmul(a, w, out_dtype, tm=1024, tn=1024, col_scale=None):
    m, k = a.shape
    _, n = w.shape
    tm = min(tm, m)
    tn = min(tn, n)
    assert m % tm == 0 and n % tn == 0
    in_specs = [pl.BlockSpec((tm, k), lambda i, j: (i, 0)),
                pl.BlockSpec((k, tn), lambda i, j: (0, j))]
    args = (a, w)
    kern = _mm_kernel
    if col_scale is not None:
        in_specs.append(pl.BlockSpec((1, tn), lambda i, j: (0, j)))
        args = (a, w, col_scale)
        kern = _mm_scaled_kernel
    return pl.pallas_call(
        kern,
        grid=(m // tm, n // tn),
        in_specs=in_specs,
        out_specs=pl.BlockSpec((tm, tn), lambda i, j: (i, j)),
        out_shape=jax.ShapeDtypeStruct((m, n), out_dtype),
        compiler_params=_cparams(("parallel", "arbitrary")),
        name="matmul",
    )(*args)


def matmul2(a1, a2, w, out_dtype, tm=1024, tn=1024):
    m, k1 = a1.shape
    k2 = a2.shape[1]
    n = w.shape[1]
    assert k1 == k2 and w.shape[0] == k1 + k2
    tm = min(tm, m)
    tn = min(tn, n)
    assert m % tm == 0 and n % tn == 0
    return pl.pallas_call(
        _mm2_kernel,
        grid=(m // tm, n // tn),
        in_specs=[pl.BlockSpec((tm, k1), lambda i, j: (i, 0)),
                  pl.BlockSpec((tm, k2), lambda i, j: (i, 0)),
                  pl.BlockSpec((k1, tn), lambda i, j: (0, j)),
                  pl.BlockSpec((k2, tn), lambda i, j: (1, j))],
        out_specs=pl.BlockSpec((tm, tn), lambda i, j: (i, j)),
        out_shape=jax.ShapeDtypeStruct((m, n), out_dtype),
        compiler_params=_cparams(("parallel", "arbitrary")),
        name="matmul2",
    )(a1, a2, w, w)


def _ffn_kernel(h_ref, wg_ref, wu_ref, wd_ref, o_ref, *, n_chunk):
    f = pl.program_id(1)

    @pl.when(f == 0)
    def _():
        o_ref[...] = jnp.zeros_like(o_ref)

    h = h_ref[...]
    g = jnp.dot(h, wg_ref[...], preferred_element_type=F32)
    u = jnp.dot(h, wu_ref[...], preferred_element_type=F32)
    act = (g * jax.nn.sigmoid(g) * u).astype(BF16)
    d = o_ref.shape[1]
    for c in range(0, d, n_chunk):
        o_ref[:, c:c + n_chunk] += jnp.dot(act, wd_ref[:, c:c + n_chunk],
                                           preferred_element_type=F32)


def ffn(h, wg, wu, wd, tm=1024, tf=256, n_chunk=512):
    m, d = h.shape
    dff = wg.shape[1]
    tm = min(tm, m)
    tf = min(tf, dff)
    n_chunk = min(n_chunk, d)
    assert m % tm == 0 and dff % tf == 0 and d % n_chunk == 0
    return pl.pallas_call(
        functools.partial(_ffn_kernel, n_chunk=n_chunk),
        grid=(m // tm, dff // tf),
        in_specs=[pl.BlockSpec((tm, d), lambda i, f: (i, 0), pipeline_mode=pl.Buffered(1)),
                  pl.BlockSpec((d, tf), lambda i, f: (0, f)),
                  pl.BlockSpec((d, tf), lambda i, f: (0, f)),
                  pl.BlockSpec((tf, d), lambda i, f: (f, 0))],
        out_specs=pl.BlockSpec((tm, d), lambda i, f: (i, 0), pipeline_mode=pl.Buffered(1)),
        out_shape=jax.ShapeDtypeStruct((m, d), F32),
        compiler_params=_cparams(("parallel", "arbitrary")),
        name="ffn",
    )(h, wg, wu, wd)


def _fgate_kernel(f_ref, b_ref, c_ref, *, blk):
    s = f_ref.shape[0]
    row = lax.broadcasted_iota(jnp.int32, (blk, blk), 0)
    col = lax.broadcasted_iota(jnp.int32, (blk, blk), 1)
    tri = jnp.where(row >= col, 1.0, 0.0).astype(BF16)
    carry = jnp.zeros((1, f_ref.shape[1]), F32)
    for r0 in range(0, s, blk):
        x = f_ref[r0:r0 + blk, :] + b_ref[...]
        logf = -_softplus(-x)
        c = _dot_exact_lhs(tri, logf) + carry
        c_ref[r0:r0 + blk, :] = c
        carry = c[blk - 1:blk, :]


def fgate_cumsum(small, f_bias_row, f_block, seq, blk=256):
    m = small.shape[0]
    blk = min(blk, seq)
    return pl.pallas_call(
        functools.partial(_fgate_kernel, blk=blk),
        grid=(m // seq,),
        in_specs=[pl.BlockSpec((seq, LANES), lambda b: (b, f_block)),
                  pl.BlockSpec((1, LANES), lambda b: (0, 0))],
        out_specs=pl.BlockSpec((seq, LANES), lambda b: (b, 0)),
        out_shape=jax.ShapeDtypeStruct((m, LANES), F32),
        compiler_params=_cparams(("parallel",)),
        name="fgate_cumsum",
    )(small, f_bias_row)


LOG2E = 1.4426950408889634
FOX_Q_SCALE = FOX_HEAD_DIM ** -0.5 * LOG2E


def _fox_kernel(q_ref, k_ref, v_ref, cq_ref, ck_ref, o_ref, s_scr, p_scr, m_scr, cq_scr,
                acc_scr, *, tq, tk, hg, strip):
    qi = pl.program_id(2)
    dh = FOX_HEAD_DIM
    hs = range(hg)
    nl = tk // dh
    m_scr[...] = jnp.full_like(m_scr, NEG_INF)
    acc_scr[...] = jnp.zeros_like(acc_scr)
    for g in hs:
        cq_scr[g] = jnp.broadcast_to(cq_ref[g] * LOG2E, (tq, dh))
    ones = jnp.ones((tk, dh), BF16)
    col = lax.broadcasted_iota(jnp.int32, (strip, tk), 1)
    row = lax.broadcasted_iota(jnp.int32, (strip, tk), 0)

    def head_cols(g):
        return slice(g * dh, (g + 1) * dh)

    def lane_tile(x, n):
        return x if n == 1 else jnp.concatenate([x] * n, axis=1)

    def block(ks, r0, mask_off):
        for g in hs:
            s_scr[g, r0:, :] = _dg(q_ref[r0:, head_cols(g)],
                                   k_ref[pl.ds(ks, tk), head_cols(g)], NT)
        for g in hs:
            ckg = ck_ref[g, :, pl.ds(ks, tk)] * LOG2E
            for r in range(r0, tq, strip):
                rows = slice(r, r + strip)
                s = s_scr[g, rows, :] - ckg
                if mask_off is not None:
                    s = jnp.where(row + (r - mask_off) >= col, s, NEG_INF)
                cqr = cq_scr[g, rows, :]
                m_prev = m_scr[g, rows, :]
                m_new = jnp.maximum(m_prev, jnp.max(s, axis=-1, keepdims=True) + cqr)
                p_scr[g, rows, :] = jnp.exp2(s + lane_tile(cqr - m_new, nl)).astype(BF16)
                m_scr[g, rows, :] = m_new
                acc_scr[g, rows, :] = lane_tile(jnp.exp2(m_prev - m_new), 2) * acc_scr[g, rows, :]
        for g in hs:
            v_aug = jnp.concatenate([v_ref[pl.ds(ks, tk), head_cols(g)], ones], axis=1)
            acc_scr[g, r0:, :] += _dg(p_scr[g, r0:, :], v_aug, NN)

    def body(ki, carry):
        block(pl.multiple_of(ki * tk, tk), 0, None)
        return carry

    n_sub = tq // tk
    lax.fori_loop(0, qi * n_sub, body, 0)
    for j in range(n_sub):
        block(pl.multiple_of(qi * tq + j * tk, tk), j * tk, j * tk)
    for g in hs:
        acc = acc_scr[g]
        o_ref[:, head_cols(g)] = (acc[:, :dh] / acc[:, dh:]).astype(o_ref.dtype)


def fox_attention(qkv, c_col, c_row, batch, seq, heads, tq=512, tk=256, hg=2, strip=64):
    tq = min(tq, seq)
    tk = min(tk, tq)
    strip = min(strip, tk)
    nq = seq // tq
    dh = FOX_HEAD_DIM
    ng = heads // hg
    assert heads % hg == 0 and tq % tk == 0 and tk % strip == 0 and seq % tq == 0
    return pl.pallas_call(
        functools.partial(_fox_kernel, tq=tq, tk=tk, hg=hg, strip=strip),
        grid=(batch, ng, nq),
        in_specs=[
            pl.BlockSpec((tq, hg * dh), lambda b, h, qi: (b * nq + qi, h)),
            pl.BlockSpec((seq, hg * dh), lambda b, h, qi: (b, ng + h)),
            pl.BlockSpec((seq, hg * dh), lambda b, h, qi: (b, 2 * ng + h)),
            pl.BlockSpec((None, hg, tq, 1), lambda b, h, qi: (b, h, qi, 0)),
            pl.BlockSpec((None, hg, 1, seq), lambda b, h, qi: (b, h, 0, 0)),
        ],
        out_specs=pl.BlockSpec((tq, hg * dh), lambda b, h, qi: (b * nq + qi, h)),
        out_shape=jax.ShapeDtypeStruct((batch * seq, heads * dh), BF16),
        scratch_shapes=[pltpu.VMEM((hg, tq, tk), F32), pltpu.VMEM((hg, tq, tk), BF16),
                        pltpu.VMEM((hg, tq, dh), F32), pltpu.VMEM((hg, tq, dh), F32),
                        pltpu.VMEM((hg, tq, 2 * dh), F32)],
        compiler_params=_cparams(("parallel", "parallel", "arbitrary")),
        name="fox_attention",
    )(qkv, qkv, qkv, c_col, c_row)


def _rwkv_prep_kernel(r_ref, k_ref, v_ref, rp_ref, kp_ref, vp_ref, sm_ref, smp_ref,
                      mu_r_ref, mu_k_ref, mu_v_ref, mu_s_ref,
                      w0_ref, wup_ref, a0_ref, aup_ref, gup_ref, kk_ref, ka_ref,
                      r_o, lw_o, k_o, v_o, kk_o, b_o, g_o, *, tiles_per_seq):
    i = pl.program_id(0)
    first = (i % tiles_per_seq) == 0
    tm = r_ref.shape[0]
    row0 = lax.broadcasted_iota(jnp.int32, (tm, 1), 0) == 0

    def mix(cur_ref, prev_ref, mu_ref):
        cur = cur_ref[...]
        last = jnp.where(first, 0.0, prev_ref[7:8, :])
        prev = jnp.where(row0, last, pltpu.roll(cur, 1, axis=0))
        return cur + mu_ref[...] * (prev - cur)

    zr = mix(r_ref, rp_ref, mu_r_ref)
    zk = mix(k_ref, kp_ref, mu_k_ref)
    zv = mix(v_ref, vp_ref, mu_v_ref)
    zs = mix(sm_ref, smp_ref, mu_s_ref)
    zw = zs[:, 0:LANES]
    za = zs[:, LANES:2 * LANES]
    zg = zs[:, 2 * LANES:2 * LANES + GATE_LORA]

    w_log = -_softplus(-(w0_ref[...] + _dot3(jnp.tanh(zw), wup_ref[...]))) - 0.5
    lw_o[...] = -jnp.exp(w_log)
    a = jax.nn.sigmoid(a0_ref[...] + _dot3(za, aup_ref[...]))
    g_o[...] = _dot3(jax.nn.sigmoid(zg), gup_ref[...])

    tc = zk.shape[1]
    hr = lax.broadcasted_iota(jnp.int32, (tc, tc), 0) // RWKV_HEAD_DIM
    hc = lax.broadcasted_iota(jnp.int32, (tc, tc), 1) // RWKV_HEAD_DIM
    head_ones = jnp.where(hr == hc, 1.0, 0.0).astype(BF16)
    kk = zk * kk_ref[...]
    ss = _dot_exact_rhs(kk * kk, head_ones)
    kk = kk / jnp.maximum(jnp.sqrt(ss), 1e-12)
    r_o[...] = zr
    v_o[...] = zv
    kk_o[...] = kk
    b_o[...] = kk * a
    k_o[...] = zk * (1.0 + (a - 1.0) * ka_ref[...])


def rwkv_prep(rkv, small, mu_r, mu_k, mu_v, mu_s, w0, w_up, a0, a_up, g_up, k_k, k_a,
              seq, tm=512, tc=512):
    m = rkv.shape[0]
    width = RWKV_WIDTH
    tm = min(tm, seq)
    nc = width // tc
    ns = small.shape[1]
    tps = seq // tm
    sub = tm // 8

    def cur(piece):
        return pl.BlockSpec((tm, tc), lambda i, j: (i, piece * nc + j))

    def prev(piece):
        return pl.BlockSpec((8, tc), lambda i, j: (jnp.maximum(i * sub - 1, 0), piece * nc + j))

    colv = pl.BlockSpec((1, tc), lambda i, j: (0, j))
    out = pl.BlockSpec((tm, tc), lambda i, j: (i, j))
    outs = [jax.ShapeDtypeStruct((m, width), F32)] * 7
    return pl.pallas_call(
        functools.partial(_rwkv_prep_kernel, tiles_per_seq=tps),
        grid=(m // tm, nc),
        in_specs=[cur(0), cur(1), cur(2), prev(0), prev(1), prev(2),
                  pl.BlockSpec((tm, ns), lambda i, j: (i, 0)),
                  pl.BlockSpec((8, ns), lambda i, j: (jnp.maximum(i * sub - 1, 0), 0)),
                  colv, colv, colv,
                  pl.BlockSpec((1, ns), lambda i, j: (0, 0)),
                  colv,
                  pl.BlockSpec((LANES, tc), lambda i, j: (0, j)),
                  colv,
                  pl.BlockSpec((LANES, tc), lambda i, j: (0, j)),
                  pl.BlockSpec((GATE_LORA, tc), lambda i, j: (0, j)),
                  colv, colv],
        out_specs=[out] * 7,
        out_shape=outs,
        compiler_params=_cparams(("parallel", "arbitrary")),
        name="rwkv_prep",
    )(rkv, rkv, rkv, rkv, rkv, rkv, small, small,
      mu_r, mu_k, mu_v, mu_s, w0, w_up, a0, a_up, g_up, k_k, k_a)


def _rwkv_kernel(r_ref, lw_ref, k_ref, v_ref, kk_ref, b_ref, g_ref,
                 lnw_ref, lnb_ref, rk_ref, o_ref, s_ref, *, pairs):
    ci = pl.program_id(2)
    c = CHUNK
    hd = RWKV_HEAD_DIM

    @pl.when(ci == 0)
    def _():
        s_ref[...] = jnp.zeros_like(s_ref)

    r2 = lax.broadcasted_iota(jnp.int32, (LANES, LANES), 0)
    c2 = lax.broadcasted_iota(jnp.int32, (LANES, LANES), 1)
    same = (r2 < c) == (c2 < c)
    strict = jnp.logical_and(same, r2 > c2)
    incl = jnp.logical_and(same, r2 >= c2)
    head_ones = jnp.where((r2 < hd) == (c2 < hd), 1.0, 0.0).astype(BF16)
    rc = lax.broadcasted_iota(jnp.int32, (c, c), 0)
    cc = lax.broadcasted_iota(jnp.int32, (c, c), 1)
    tri_incl = jnp.where(rc >= cc, 1.0, 0.0).astype(BF16)
    m_a = lax.broadcasted_iota(jnp.int32, (1, LANES), 1) < hd
    inv_n = 1.0 / hd
    prs = range(pairs)
    cat = lambda xs: jnp.concatenate(xs, axis=0)

    def lanes(x, p):
        return x[:, p * LANES:(p + 1) * LANES]

    def stack2(x):
        return cat([jnp.where(m_a, x, 0.0), jnp.where(m_a, 0.0, x)])

    lw = lw_ref[...]
    r, k, v = r_ref[...], k_ref[...], v_ref[...]
    kk, beta = kk_ref[...], b_ref[...]
    big_l = _dot_exact_lhs(tri_incl, lw)
    l_tot = big_l[c - 1:c, :]
    e_nl = jnp.exp(-big_l)
    e_rem = jnp.exp(l_tot - big_l)
    e_tot = jnp.exp(l_tot)
    at = -kk * jnp.exp(big_l - lw)
    rt = r * jnp.exp(big_l)
    kt = k * e_nl
    bt = beta * e_nl
    bend = beta * e_rem
    kend = k * e_rem

    s0 = [s_ref[p] for p in prs]
    lhs = [cat([stack2(lanes(at, p)), stack2(lanes(rt, p))]).astype(BF16) for p in prs]
    wbk = [cat([lanes(bt, p), lanes(kt, p)]).astype(BF16) for p in prs]
    v2 = [stack2(lanes(v, p)).astype(BF16) for p in prs]
    m = [_dg(lhs[p], wbk[p], NT) for p in prs]
    sx = [_dg(lhs[p], s0[p].astype(BF16), NT) for p in prs]

    a_ab, a_kr, a_rb = [], [], []
    for p in prs:
        mp = m[p]
        sw = pltpu.roll(mp, c, axis=1)
        a_ab.append(jnp.where(strict, cat([mp[0:c], sw[c:2 * c]]), 0.0))
        a_ak = jnp.where(strict, cat([sw[0:c], mp[c:2 * c]]), 0.0)
        a_rb.append(jnp.where(incl, cat([mp[2 * c:3 * c], sw[3 * c:4 * c]]), 0.0))
        a_rk = jnp.where(incl, cat([sw[2 * c:3 * c], mp[3 * c:4 * c]]), 0.0)
        a_kr.append(cat([a_ak, a_rk]).astype(BF16))
    xy = [_dg(a_kr[p], v2[p], NN) for p in prs]
    u = [sx[p][:2 * c] + xy[p][:2 * c] for p in prs]
    ypart = [sx[p][2 * c:] + xy[p][2 * c:] for p in prs]

    pw = a_ab
    n_sq = (c - 1).bit_length()
    for it in range(n_sq):
        u = [u[p] + _dot3(pw[p], u[p]) for p in prs]
        if it + 1 < n_sq:
            pw = [_dot3(pw[p], pw[p]) for p in prs]

    ub = [u[p].astype(BF16) for p in prs]
    y2 = [ypart[p] + _dg(a_rb[p].astype(BF16), ub[p], NN) for p in prs]
    for p in prs:
        ends = cat([stack2(lanes(bend, p)), stack2(lanes(kend, p))]).astype(BF16)
        s_ref[p] = s0[p] * lanes(e_tot, p) + _dg(cat([ub[p], v2[p]]), ends, TN)

    y_all = cat([y2[p][:c] + y2[p][c:] for p in prs])
    mean = _dot_exact_rhs(y_all, head_ones) * inv_n
    yc = y_all - mean
    var = _dot_exact_rhs(yc * yc, head_ones) * inv_n
    yn = yc * lax.rsqrt(var + RWKV_GN_EPS)
    rkk = r * k * rk_ref[...]
    bonus = _dot_exact_rhs(cat([lanes(rkk, p) for p in prs]), head_ones)
    lnw, lnb, g = lnw_ref[...], lnb_ref[...], g_ref[...]
    for p in prs:
        rows = slice(p * c, (p + 1) * c)
        out = (yn[rows] * lanes(lnw, p) + lanes(lnb, p) + bonus[rows] * lanes(v, p)) * lanes(g, p)
        o_ref[:, p * LANES:(p + 1) * LANES] = out.astype(o_ref.dtype)


def rwkv_recurrence(r, lw, k, v, kk, beta, g, ln_w, ln_b, r_k, batch, seq, pairs=8):
    m, width = r.shape
    c = CHUNK
    assert seq % c == 0 and 2 * c == LANES and 2 * RWKV_HEAD_DIM == LANES
    tc = pairs * LANES
    nchunk = seq // c
    blk = pl.BlockSpec((c, tc), lambda b, j, ci: (b * nchunk + ci, j))
    vec = pl.BlockSpec((1, tc), lambda b, j, ci: (0, j))
    return pl.pallas_call(
        functools.partial(_rwkv_kernel, pairs=pairs),
        grid=(batch, width // tc, nchunk),
        in_specs=[blk] * 7 + [vec] * 3,
        out_specs=blk,
        out_shape=jax.ShapeDtypeStruct((m, width), BF16),
        scratch_shapes=[pltpu.VMEM((pairs, LANES, LANES), F32)],
        compiler_params=_cparams(("parallel", "parallel", "arbitrary")),
        name="rwkv_recurrence",
    )(r, lw, k, v, kk, beta, g, ln_w, ln_b, r_k)


def _xattn_kernel(q_ref, k_ref, v_ref, o_ref, *, scale):
    s = _dg(q_ref[...], k_ref[...], NT) * scale
    m = jnp.max(s, axis=-1, keepdims=True)
    p = jnp.exp(s - m)
    l = jnp.sum(p, axis=-1, keepdims=True)
    o = jnp.dot(p.astype(BF16), v_ref[...], preferred_element_type=F32)
    o_ref[...] = (o / l).astype(o_ref.dtype)


def cross_attention(q, k, v, batch, seq, n_mem, heads, tq=512):
    m, d = q.shape
    dh = d // heads
    tq = min(tq, seq)
    nq = seq // tq
    return pl.pallas_call(
        functools.partial(_xattn_kernel, scale=dh ** -0.5),
        grid=(batch, heads, nq),
        in_specs=[pl.BlockSpec((tq, dh), lambda b, h, qi: (b * nq + qi, h)),
                  pl.BlockSpec((n_mem, dh), lambda b, h, qi: (b, h)),
                  pl.BlockSpec((n_mem, dh), lambda b, h, qi: (b, h))],
        out_specs=pl.BlockSpec((tq, dh), lambda b, h, qi: (b * nq + qi, h)),
        out_shape=jax.ShapeDtypeStruct((m, d), BF16),
        compiler_params=_cparams(("parallel", "parallel", "arbitrary")),
        name="cross_attention",
    )(q, k, v)


def _pad_cols(w, n):
    return jnp.pad(w, ((0, 0), (0, n - w.shape[1])))


def _pad_rows(w, n):
    return jnp.pad(w, ((0, n - w.shape[0]), (0, 0)))


def _bf(w):
    return w.astype(BF16)


def _row(t):
    return t.reshape(1, -1).astype(F32)


def _mixer(h, p, batch, seq):
    fox_w, rw = FOX_WIDTH, RWKV_WIDTH
    fox_heads = fox_w // FOX_HEAD_DIM
    bf, row = _bf, _row

    w_in = p['w_in']
    o_f = 3 * fox_w
    o_r = o_f + fox_heads
    o_w = o_r + 3 * rw
    o_a = o_w + DECAY_LORA
    o_g = o_a + ICLR_LORA
    w_qkv = bf(w_in[:, :o_f])
    w_rkv = bf(w_in[:, o_r:o_w])
    w_small = bf(jnp.concatenate([
        _pad_cols(w_in[:, o_w:o_a], LANES), _pad_cols(w_in[:, o_a:o_g], LANES),
        w_in[:, o_g:o_g + GATE_LORA], _pad_cols(w_in[:, o_f:o_r], LANES)], axis=1))
    mu = p['rwkv_mu']
    mu_small = jnp.concatenate([
        jnp.pad(mu[3 * rw:3 * rw + DECAY_LORA], (0, LANES - DECAY_LORA)),
        jnp.pad(mu[3 * rw + DECAY_LORA:3 * rw + DECAY_LORA + ICLR_LORA], (0, LANES - ICLR_LORA)),
        mu[3 * rw + DECAY_LORA + ICLR_LORA:], jnp.zeros((LANES,), F32)])

    q_scale = jnp.concatenate([jnp.full((1, fox_w), FOX_Q_SCALE, F32),
                               jnp.ones((1, 2 * fox_w), F32)], axis=1)
    qkv = matmul(h, w_qkv, BF16, col_scale=q_scale)
    rkv = matmul(h, w_rkv, F32)
    small = matmul(h, w_small, F32, tn=w_small.shape[1])

    f_bias = jnp.pad(p['fox_f_bias'], (0, LANES - fox_heads)).reshape(1, LANES)
    c = fgate_cumsum(small, f_bias, f_block=(2 * LANES + GATE_LORA) // LANES, seq=seq)
    c = c[:, :fox_heads].reshape(batch, seq, fox_heads).transpose(0, 2, 1)
    y_fox = fox_attention(qkv, c[..., None], c[:, :, None, :], batch, seq, fox_heads)

    r, lw, k, v, kk, beta, g = rwkv_prep(
        rkv, small, row(mu[:rw]), row(mu[rw:2 * rw]), row(mu[2 * rw:3 * rw]), row(mu_small),
        row(p['rwkv_w0']), _pad_rows(p['rwkv_w_up'], LANES), row(p['rwkv_a0']),
        _pad_rows(p['rwkv_a_up'], LANES), p['rwkv_g_up'], row(p['rwkv_k_k']),
        row(p['rwkv_k_a']), seq=seq)
    y_rwkv = rwkv_recurrence(r, lw, k, v, kk, beta, g, row(p['rwkv_ln_w']),
                             row(p['rwkv_ln_b']), row(p['rwkv_r_k']), batch, seq)

    return matmul2(y_fox, y_rwkv, bf(p['w_out']), F32)


def _xattn(h, mn, p, batch, seq, n_mem):
    bf = _bf
    q = matmul(h, bf(p['xattn_wq']), BF16)
    km = matmul(mn, bf(p['xattn_wk']), BF16)
    vm = matmul(mn, bf(p['xattn_wv']), BF16)
    o = cross_attention(q, km, vm, batch, seq, n_mem, XATTN_HEADS)
    return matmul(o, bf(p['xattn_wo']), F32)


def _layer(x, mem, p, batch, seq, n_mem):
    bf = _bf

    h = rmsnorm(x, p['ffn1_pre_g'])
    y = ffn(h, bf(p['ffn1_w_gate']), bf(p['ffn1_w_up']), bf(p['ffn1_w_down']))
    x, h = resnorm(x, y, p['ffn1_post_g'], 0.5, p['mix_pre_g'])

    mixed = _mixer(h, p, batch, seq)
    x, h = resnorm(x, mixed, p['mix_post_g'], 1.0, p['xattn_pre_g'])

    mn = rmsnorm(mem, p['mem_norm_g'])
    xa = _xattn(h, mn, p, batch, seq, n_mem)
    x, h = resnorm(x, xa, p['xattn_post_g'], 1.0, p['ffn2_pre_g'])

    y = ffn(h, bf(p['ffn2_w_gate']), bf(p['ffn2_w_up']), bf(p['ffn2_w_down']))
    return resnorm(x, y, p['ffn2_post_g'], 0.5)


_PARAM_NAMES = (
    'ffn1_pre_g', 'ffn1_w_gate', 'ffn1_w_up', 'ffn1_w_down', 'ffn1_post_g',
    'mix_pre_g', 'w_in', 'fox_f_bias', 'rwkv_mu', 'rwkv_w0', 'rwkv_w_up', 'rwkv_a0',
    'rwkv_a_up', 'rwkv_g_up', 'rwkv_k_k', 'rwkv_k_a', 'rwkv_r_k', 'rwkv_ln_w', 'rwkv_ln_b',
    'w_out', 'mix_post_g', 'xattn_pre_g', 'mem_norm_g', 'xattn_wq', 'xattn_wk', 'xattn_wv',
    'xattn_wo', 'xattn_post_g', 'ffn2_pre_g', 'ffn2_w_gate', 'ffn2_w_up', 'ffn2_w_down',
    'ffn2_post_g')


def kernel(x, mem, ffn1_pre_g, ffn1_w_gate, ffn1_w_up, ffn1_w_down, ffn1_post_g, mix_pre_g, w_in, fox_f_bias, rwkv_mu, rwkv_w0, rwkv_w_up, rwkv_a0, rwkv_a_up, rwkv_g_up, rwkv_k_k, rwkv_k_a, rwkv_r_k, rwkv_ln_w, rwkv_ln_b, w_out, mix_post_g, xattn_pre_g, mem_norm_g, xattn_wq, xattn_wk, xattn_wv, xattn_wo, xattn_post_g, ffn2_pre_g, ffn2_w_gate, ffn2_w_up, ffn2_w_down, ffn2_post_g):
    weights = (ffn1_pre_g, ffn1_w_gate, ffn1_w_up, ffn1_w_down, ffn1_post_g, mix_pre_g, w_in,
               fox_f_bias, rwkv_mu, rwkv_w0, rwkv_w_up, rwkv_a0, rwkv_a_up, rwkv_g_up, rwkv_k_k,
               rwkv_k_a, rwkv_r_k, rwkv_ln_w, rwkv_ln_b, w_out, mix_post_g, xattn_pre_g,
               mem_norm_g, xattn_wq, xattn_wk, xattn_wv, xattn_wo, xattn_post_g, ffn2_pre_g,
               ffn2_w_gate, ffn2_w_up, ffn2_w_down, ffn2_post_g)
    batch, seq, d = x.shape
    n_mem = mem.shape[1]
    depth = ffn1_pre_g.shape[0]
    xf = x.reshape(batch * seq, d)
    mf = mem.reshape(batch * n_mem, d)
    for l in range(depth):
        p = {name: w[l] for name, w in zip(_PARAM_NAMES, weights)}
        xf = _layer(xf, mf, p, batch, seq, n_mem)
    return xf.reshape(batch, seq, d)
```

```python
import functools

import jax
import jax.numpy as jnp
from jax import lax
from jax.experimental import pallas as pl
from jax.experimental.pallas import tpu as pltpu

F32 = jnp.float32
BF16 = jnp.bfloat16

SEQ = 2048
FOX_HEAD_DIM = 128
RWKV_HEAD_DIM = 64
RWKV_WIDTH = 2048
FOX_WIDTH = 2048
DECAY_LORA = 96
ICLR_LORA = 96
GATE_LORA = 256
XATTN_HEADS = 4
RMS_EPS = 1e-6
RWKV_GN_EPS = 64e-5
NEG_INF = -1e30

LANES = 128
CHUNK = 64
VMEM_LIMIT = 56 * 1024 * 1024
FFN_VMEM_LIMIT = 62 * 1024 * 1024
F32_W_TN = 512


def _cparams(sem, vmem=VMEM_LIMIT):
    return pltpu.CompilerParams(dimension_semantics=sem, vmem_limit_bytes=vmem)


NN = (((1,), (0,)), ((), ()))
NT = (((1,), (1,)), ((), ()))
TN = (((0,), (0,)), ((), ()))


def _split2(x):
    hi = x.astype(BF16)
    lo = (x - hi.astype(F32)).astype(BF16)
    return hi, lo


def _split3(x):
    hi = x.astype(BF16)
    r = x - hi.astype(F32)
    mid = r.astype(BF16)
    lo = (r - mid.astype(F32)).astype(BF16)
    return hi, mid, lo


def _dg(a, b, dims):
    return lax.dot_general(a, b, dims, preferred_element_type=F32)


def _dot3(a, b, dims=NN):
    ah, al = _split2(a)
    bh, bl = _split2(b)
    return _dg(ah, bh, dims) + (_dg(ah, bl, dims) + _dg(al, bh, dims))


def _dot_exact_lhs(a_bf16, b, dims=NN):
    b0, b1, b2 = _split3(b)
    return _dg(a_bf16, b0, dims) + (_dg(a_bf16, b1, dims) + _dg(a_bf16, b2, dims))


def _dot_exact_rhs(a, b_bf16, dims=NN):
    a0, a1, a2 = _split3(a)
    return _dg(a0, b_bf16, dims) + (_dg(a1, b_bf16, dims) + _dg(a2, b_bf16, dims))


def _softplus(x):
    return jnp.maximum(x, 0.0) + jnp.log1p(jnp.exp(-jnp.abs(x)))


def _rmsnorm_kernel(x_ref, g_ref, o_ref):
    x = x_ref[...]
    ms = jnp.mean(x * x, axis=-1, keepdims=True)
    o_ref[...] = (x * lax.rsqrt(ms + RMS_EPS) * g_ref[...]).astype(o_ref.dtype)


def rmsnorm(x, g, tm=256):
    m, d = x.shape
    tm = min(tm, m)
    return pl.pallas_call(
        _rmsnorm_kernel,
        grid=(m // tm,),
        in_specs=[pl.BlockSpec((tm, d), lambda i: (i, 0)),
                  pl.BlockSpec((1, d), lambda i: (0, 0))],
        out_specs=pl.BlockSpec((tm, d), lambda i: (i, 0)),
        out_shape=jax.ShapeDtypeStruct((m, d), BF16),
        compiler_params=_cparams(("parallel",)),
        name="rmsnorm",
    )(x, g.reshape(1, d))


def _resnorm_kernel(x_ref, y_ref, gp_ref, gn_ref, xo_ref, h_ref, *, scale):
    y = y_ref[...]
    ms = jnp.mean(y * y, axis=-1, keepdims=True)
    xn = x_ref[...] + scale * (y * lax.rsqrt(ms + RMS_EPS) * gp_ref[...])
    xo_ref[...] = xn
    ms2 = jnp.mean(xn * xn, axis=-1, keepdims=True)
    h_ref[...] = (xn * lax.rsqrt(ms2 + RMS_EPS) * gn_ref[...]).astype(h_ref.dtype)


def _resnorm_last_kernel(x_ref, y_ref, gp_ref, xo_ref, *, scale):
    y = y_ref[...]
    ms = jnp.mean(y * y, axis=-1, keepdims=True)
    xo_ref[...] = x_ref[...] + scale * (y * lax.rsqrt(ms + RMS_EPS) * gp_ref[...])


def resnorm(x, y, g_post, scale, g_next=None, tm=256):
    m, d = x.shape
    row = pl.BlockSpec((tm, d), lambda i: (i, 0))
    vec = pl.BlockSpec((1, d), lambda i: (0, 0))
    if g_next is None:
        return pl.pallas_call(
            functools.partial(_resnorm_last_kernel, scale=scale),
            grid=(m // tm,),
            in_specs=[row, row, vec],
            out_specs=row,
            out_shape=jax.ShapeDtypeStruct((m, d), F32),
            compiler_params=_cparams(("parallel",)),
            name="resnorm_last",
        )(x, y, g_post.reshape(1, d))
    return pl.pallas_call(
        functools.partial(_resnorm_kernel, scale=scale),
        grid=(m // tm,),
        in_specs=[row, row, vec, vec],
        out_specs=[row, row],
        out_shape=[jax.ShapeDtypeStruct((m, d), F32),
                   jax.ShapeDtypeStruct((m, d), BF16)],
        compiler_params=_cparams(("parallel",)),
        name="resnorm",
    )(x, y, g_post.reshape(1, d), g_next.reshape(1, d))


def _mm_kernel(a_ref, w_ref, o_ref):
    o_ref[...] = jnp.dot(a_ref[...], w_ref[...].astype(BF16),
                         preferred_element_type=F32).astype(o_ref.dtype)


def _mm_scaled_kernel(a_ref, w_ref, s_ref, o_ref):
    acc = jnp.dot(a_ref[...], w_ref[...].astype(BF16), preferred_element_type=F32)
    o_ref[...] = (acc * s_ref[...]).astype(o_ref.dtype)


def _mm2_kernel(a1_ref, a2_ref, w1_ref, w2_ref, o_ref):
    acc = jnp.dot(a1_ref[...], w1_ref[...].astype(BF16), preferred_element_type=F32)
    acc += jnp.dot(a2_ref[...], w2_ref[...].astype(BF16), preferred_element_type=F32)
    o_ref[...] = acc.astype(o_ref.dtype)


def matmul(a, w, out_dtype, tm=1024, tn=1024, col_scale=None, n=None):
    m, k = a.shape
    n = w.shape[1] if n is None else n
    tm = min(tm, m)
    tn = min(tn, n)
    assert m % tm == 0 and n % tn == 0
    in_specs = [pl.BlockSpec((tm, k), lambda i, j: (i, 0)),
                pl.BlockSpec((k, tn), lambda i, j: (0, j))]
    args = (a, w)
    kern = _mm_kernel
    if col_scale is not None:
        in_specs.append(pl.BlockSpec((1, tn), lambda i, j: (0, j)))
        args = (a, w, col_scale)
        kern = _mm_scaled_kernel
    return pl.pallas_call(
        kern,
        grid=(m // tm, n // tn),
        in_specs=in_specs,
        out_specs=pl.BlockSpec((tm, tn), lambda i, j: (i, j)),
        out_shape=jax.ShapeDtypeStruct((m, n), out_dtype),
        compiler_params=_cparams(("parallel", "arbitrary")),
        name="matmul",
    )(*args)


def matmul2(a1, a2, w, out_dtype, tm=1024, tn=1024):
    m, k1 = a1.shape
    k2 = a2.shape[1]
    n = w.shape[1]
    assert k1 == k2 and w.shape[0] == k1 + k2
    tm = min(tm, m)
    tn = min(tn, n)
    assert m % tm == 0 and n % tn == 0
    return pl.pallas_call(
        _mm2_kernel,
        grid=(m // tm, n // tn),
        in_specs=[pl.BlockSpec((tm, k1), lambda i, j: (i, 0)),
                  pl.BlockSpec((tm, k2), lambda i, j: (i, 0)),
                  pl.BlockSpec((k1, tn), lambda i, j: (0, j)),
                  pl.BlockSpec((k2, tn), lambda i, j: (1, j))],
        out_specs=pl.BlockSpec((tm, tn), lambda i, j: (i, j)),
        out_shape=jax.ShapeDtypeStruct((m, n), out_dtype),
        compiler_params=_cparams(("parallel", "arbitrary")),
        name="matmul2",
    )(a1, a2, w, w)


def _ffn_kernel(h_ref, wg_ref, wu_ref, wd_ref, o_ref, *, n_chunk):
    f = pl.program_id(1)

    @pl.when(f == 0)
    def _():
        o_ref[...] = jnp.zeros_like(o_ref)

    h = h_ref[...]
    g = jnp.dot(h, wg_ref[...].astype(BF16), preferred_element_type=F32)
    u = jnp.dot(h, wu_ref[...].astype(BF16), preferred_element_type=F32)
    act = (g * jax.nn.sigmoid(g) * u).astype(BF16)
    d = o_ref.shape[1]
    for c in range(0, d, n_chunk):
        o_ref[:, c:c + n_chunk] += jnp.dot(act, wd_ref[:, c:c + n_chunk].astype(BF16),
                                           preferred_element_type=F32)


def ffn(h, wg, wu, wd, tm=1024, tf=256, n_chunk=512):
    m, d = h.shape
    dff = wg.shape[1]
    tm = min(tm, m)
    tf = min(tf, dff)
    n_chunk = min(n_chunk, d)
    assert m % tm == 0 and dff % tf == 0 and d % n_chunk == 0
    return pl.pallas_call(
        functools.partial(_ffn_kernel, n_chunk=n_chunk),
        grid=(m // tm, dff // tf),
        in_specs=[pl.BlockSpec((tm, d), lambda i, f: (i, 0), pipeline_mode=pl.Buffered(1)),
                  pl.BlockSpec((d, tf), lambda i, f: (0, f)),
                  pl.BlockSpec((d, tf), lambda i, f: (0, f)),
                  pl.BlockSpec((tf, d), lambda i, f: (f, 0))],
        out_specs=pl.BlockSpec((tm, d), lambda i, f: (i, 0), pipeline_mode=pl.Buffered(1)),
        out_shape=jax.ShapeDtypeStruct((m, d), F32),
        compiler_params=_cparams(("parallel", "arbitrary"), vmem=FFN_VMEM_LIMIT),
        name="ffn",
    )(h, wg, wu, wd)


def _fgate_kernel(f_ref, b_ref, c_ref, *, blk):
    s = f_ref.shape[0]
    row = lax.broadcasted_iota(jnp.int32, (blk, blk), 0)
    col = lax.broadcasted_iota(jnp.int32, (blk, blk), 1)
    tri = jnp.where(row >= col, 1.0, 0.0).astype(BF16)
    carry = jnp.zeros((1, f_ref.shape[1]), F32)
    for r0 in range(0, s, blk):
        x = f_ref[r0:r0 + blk, :] + b_ref[...]
        logf = -_softplus(-x)
        c = _dot_exact_lhs(tri, logf) + carry
        c_ref[r0:r0 + blk, :] = c
        carry = c[blk - 1:blk, :]


def fgate_cumsum(small, f_bias_row, f_block, seq, blk=256):
    m = small.shape[0]
    blk = min(blk, seq)
    return pl.pallas_call(
        functools.partial(_fgate_kernel, blk=blk),
        grid=(m // seq,),
        in_specs=[pl.BlockSpec((seq, LANES), lambda b: (b, f_block)),
                  pl.BlockSpec((1, LANES), lambda b: (0, 0))],
        out_specs=pl.BlockSpec((seq, LANES), lambda b: (b, 0)),
        out_shape=jax.ShapeDtypeStruct((m, LANES), F32),
        compiler_params=_cparams(("parallel",)),
        name="fgate_cumsum",
    )(small, f_bias_row)


LOG2E = 1.4426950408889634
FOX_Q_SCALE = FOX_HEAD_DIM ** -0.5 * LOG2E


def _fox_kernel(q_ref, k_ref, v_ref, cq_ref, ck_ref, o_ref, s_scr, p_scr, m_scr, cq_scr,
                acc_scr, *, tq, tk, hg, strip):
    qi = pl.program_id(2)
    dh = FOX_HEAD_DIM
    hs = range(hg)
    nl = tk // dh
    m_scr[...] = jnp.full_like(m_scr, NEG_INF)
    acc_scr[...] = jnp.zeros_like(acc_scr)
    for g in hs:
        cq_scr[g] = jnp.broadcast_to(cq_ref[g] * LOG2E, (tq, dh))
    ones = jnp.ones((tk, dh), BF16)
    col = lax.broadcasted_iota(jnp.int32, (strip, tk), 1)
    row = lax.broadcasted_iota(jnp.int32, (strip, tk), 0)

    def head_cols(g):
        return slice(g * dh, (g + 1) * dh)

    def lane_tile(x, n):
        return x if n == 1 else jnp.concatenate([x] * n, axis=1)

    def block(ks, r0, mask_off):
        for g in hs:
            s_scr[g, r0:, :] = _dg(q_ref[r0:, head_cols(g)],
                                   k_ref[pl.ds(ks, tk), head_cols(g)], NT)
        for g in hs:
            ckg = ck_ref[g, :, pl.ds(ks, tk)] * LOG2E
            for r in range(r0, tq, strip):
                rows = slice(r, r + strip)
                s = s_scr[g, rows, :] - ckg
                if mask_off is not None:
                    s = jnp.where(row + (r - mask_off) >= col, s, NEG_INF)
                cqr = cq_scr[g, rows, :]
                m_prev = m_scr[g, rows, :]
                m_new = jnp.maximum(m_prev, jnp.max(s, axis=-1, keepdims=True) + cqr)
                p_scr[g, rows, :] = jnp.exp2(s + lane_tile(cqr - m_new, nl)).astype(BF16)
                m_scr[g, rows, :] = m_new
                acc_scr[g, rows, :] = lane_tile(jnp.exp2(m_prev - m_new), 2) * acc_scr[g, rows, :]
        for g in hs:
            v_aug = jnp.concatenate([v_ref[pl.ds(ks, tk), head_cols(g)], ones], axis=1)
            acc_scr[g, r0:, :] += _dg(p_scr[g, r0:, :], v_aug, NN)

    def body(ki, carry):
        block(pl.multiple_of(ki * tk, tk), 0, None)
        return carry

    n_sub = tq // tk
    lax.fori_loop(0, qi * n_sub, body, 0)
    for j in range(n_sub):
        block(pl.multiple_of(qi * tq + j * tk, tk), j * tk, j * tk)
    for g in hs:
        acc = acc_scr[g]
        o_ref[:, head_cols(g)] = (acc[:, :dh] / acc[:, dh:]).astype(o_ref.dtype)


def fox_attention(qkv, c_col, c_row, batch, seq, heads, tq=512, tk=256, hg=2, strip=64):
    tq = min(tq, seq)
    tk = min(tk, tq)
    strip = min(strip, tk)
    nq = seq // tq
    dh = FOX_HEAD_DIM
    ng = heads // hg
    assert heads % hg == 0 and tq % tk == 0 and tk % strip == 0 and seq % tq == 0
    return pl.pallas_call(
        functools.partial(_fox_kernel, tq=tq, tk=tk, hg=hg, strip=strip),
        grid=(batch, ng, nq),
        in_specs=[
            pl.BlockSpec((tq, hg * dh), lambda b, h, qi: (b * nq + qi, h)),
            pl.BlockSpec((seq, hg * dh), lambda b, h, qi: (b, ng + h)),
            pl.BlockSpec((seq, hg * dh), lambda b, h, qi: (b, 2 * ng + h)),
            pl.BlockSpec((None, hg, tq, 1), lambda b, h, qi: (b, h, qi, 0)),
            pl.BlockSpec((None, hg, 1, seq), lambda b, h, qi: (b, h, 0, 0)),
        ],
        out_specs=pl.BlockSpec((tq, hg * dh), lambda b, h, qi: (b * nq + qi, h)),
        out_shape=jax.ShapeDtypeStruct((batch * seq, heads * dh), BF16),
        scratch_shapes=[pltpu.VMEM((hg, tq, tk), F32), pltpu.VMEM((hg, tq, tk), BF16),
                        pltpu.VMEM((hg, tq, dh), F32), pltpu.VMEM((hg, tq, dh), F32),
                        pltpu.VMEM((hg, tq, 2 * dh), F32)],
        compiler_params=_cparams(("parallel", "parallel", "arbitrary")),
        name="fox_attention",
    )(qkv, qkv, qkv, c_col, c_row)


def _rwkv_prep_kernel(r_ref, k_ref, v_ref, rp_ref, kp_ref, vp_ref, sm_ref, smp_ref,
                      mu_r_ref, mu_k_ref, mu_v_ref, mu_s_ref,
                      w0_ref, wup_ref, a0_ref, aup_ref, gup_ref, kk_ref, ka_ref,
                      r_o, lw_o, k_o, v_o, kk_o, b_o, g_o, *, tiles_per_seq):
    i = pl.program_id(0)
    first = (i % tiles_per_seq) == 0
    tm = r_ref.shape[0]
    row0 = lax.broadcasted_iota(jnp.int32, (tm, 1), 0) == 0

    def mix(cur_ref, prev_ref, mu_ref):
        cur = cur_ref[...]
        last = jnp.where(first, 0.0, prev_ref[7:8, :])
        prev = jnp.where(row0, last, pltpu.roll(cur, 1, axis=0))
        return cur + mu_ref[...] * (prev - cur)

    zr = mix(r_ref, rp_ref, mu_r_ref)
    zk = mix(k_ref, kp_ref, mu_k_ref)
    zv = mix(v_ref, vp_ref, mu_v_ref)
    zs = mix(sm_ref, smp_ref, mu_s_ref)
    zw = zs[:, 0:LANES]
    za = zs[:, LANES:2 * LANES]
    zg = zs[:, 2 * LANES:2 * LANES + GATE_LORA]

    w_log = -_softplus(-(w0_ref[...] + _dot3(jnp.tanh(zw), wup_ref[...]))) - 0.5
    lw_o[...] = -jnp.exp(w_log)
    a = jax.nn.sigmoid(a0_ref[...] + _dot3(za, aup_ref[...]))
    g_o[...] = _dot3(jax.nn.sigmoid(zg), gup_ref[...])

    tc = zk.shape[1]
    hr = lax.broadcasted_iota(jnp.int32, (tc, tc), 0) // RWKV_HEAD_DIM
    hc = lax.broadcasted_iota(jnp.int32, (tc, tc), 1) // RWKV_HEAD_DIM
    head_ones = jnp.where(hr == hc, 1.0, 0.0).astype(BF16)
    kk = zk * kk_ref[...]
    ss = _dot_exact_rhs(kk * kk, head_ones)
    kk = kk / jnp.maximum(jnp.sqrt(ss), 1e-12)
    r_o[...] = zr
    v_o[...] = zv
    kk_o[...] = kk
    b_o[...] = kk * a
    k_o[...] = zk * (1.0 + (a - 1.0) * ka_ref[...])


def rwkv_prep(rkv, small, mu_r, mu_k, mu_v, mu_s, w0, w_up, a0, a_up, g_up, k_k, k_a,
              seq, tm=512, tc=512):
    m = rkv.shape[0]
    width = RWKV_WIDTH
    tm = min(tm, seq)
    nc = width // tc
    ns = small.shape[1]
    tps = seq // tm
    sub = tm // 8

    def cur(piece):
        return pl.BlockSpec((tm, tc), lambda i, j: (i, piece * nc + j))

    def prev(piece):
        return pl.BlockSpec((8, tc), lambda i, j: (jnp.maximum(i * sub - 1, 0), piece * nc + j))

    colv = pl.BlockSpec((1, tc), lambda i, j: (0, j))
    out = pl.BlockSpec((tm, tc), lambda i, j: (i, j))
    outs = [jax.ShapeDtypeStruct((m, width), F32)] * 7
    return pl.pallas_call(
        functools.partial(_rwkv_prep_kernel, tiles_per_seq=tps),
        grid=(m // tm, nc),
        in_specs=[cur(0), cur(1), cur(2), prev(0), prev(1), prev(2),
                  pl.BlockSpec((tm, ns), lambda i, j: (i, 0)),
                  pl.BlockSpec((8, ns), lambda i, j: (jnp.maximum(i * sub - 1, 0), 0)),
                  colv, colv, colv,
                  pl.BlockSpec((1, ns), lambda i, j: (0, 0)),
                  colv,
                  pl.BlockSpec((LANES, tc), lambda i, j: (0, j)),
                  colv,
                  pl.BlockSpec((LANES, tc), lambda i, j: (0, j)),
                  pl.BlockSpec((GATE_LORA, tc), lambda i, j: (0, j)),
                  colv, colv],
        out_specs=[out] * 7,
        out_shape=outs,
        compiler_params=_cparams(("parallel", "arbitrary")),
        name="rwkv_prep",
    )(rkv, rkv, rkv, rkv, rkv, rkv, small, small,
      mu_r, mu_k, mu_v, mu_s, w0, w_up, a0, a_up, g_up, k_k, k_a)


def _rwkv_kernel(r_ref, lw_ref, k_ref, v_ref, kk_ref, b_ref, g_ref,
                 lnw_ref, lnb_ref, rk_ref, o_ref, s_ref, *, pairs):
    ci = pl.program_id(2)
    c = CHUNK
    hd = RWKV_HEAD_DIM

    @pl.when(ci == 0)
    def _():
        s_ref[...] = jnp.zeros_like(s_ref)

    r2 = lax.broadcasted_iota(jnp.int32, (LANES, LANES), 0)
    c2 = lax.broadcasted_iota(jnp.int32, (LANES, LANES), 1)
    same = (r2 < c) == (c2 < c)
    strict = jnp.logical_and(same, r2 > c2)
    incl = jnp.logical_and(same, r2 >= c2)
    head_ones = jnp.where((r2 < hd) == (c2 < hd), 1.0, 0.0).astype(BF16)
    rc = lax.broadcasted_iota(jnp.int32, (c, c), 0)
    cc = lax.broadcasted_iota(jnp.int32, (c, c), 1)
    tri_incl = jnp.where(rc >= cc, 1.0, 0.0).astype(BF16)
    m_a = lax.broadcasted_iota(jnp.int32, (1, LANES), 1) < hd
    inv_n = 1.0 / hd
    prs = range(pairs)
    cat = lambda xs: jnp.concatenate(xs, axis=0)

    def lanes(x, p):
        return x[:, p * LANES:(p + 1) * LANES]

    def stack2(x):
        return cat([jnp.where(m_a, x, 0.0), jnp.where(m_a, 0.0, x)])

    lw = lw_ref[...]
    r, k, v = r_ref[...], k_ref[...], v_ref[...]
    kk, beta = kk_ref[...], b_ref[...]
    big_l = _dot_exact_lhs(tri_incl, lw)
    l_tot = big_l[c - 1:c, :]
    e_nl = jnp.exp(-big_l)
    e_rem = jnp.exp(l_tot - big_l)
    e_tot = jnp.exp(l_tot)
    at = -kk * jnp.exp(big_l - lw)
    rt = r * jnp.exp(big_l)
    kt = k * e_nl
    bt = beta * e_nl
    bend = beta * e_rem
    kend = k * e_rem

    s0 = [s_ref[p] for p in prs]
    lhs = [cat([stack2(lanes(at, p)), stack2(lanes(rt, p))]).astype(BF16) for p in prs]
    wbk = [cat([lanes(bt, p), lanes(kt, p)]).astype(BF16) for p in prs]
    v2 = [stack2(lanes(v, p)).astype(BF16) for p in prs]
    m = [_dg(lhs[p], wbk[p], NT) for p in prs]
    sx = [_dg(lhs[p], s0[p].astype(BF16), NT) for p in prs]

    a_ab, a_kr, a_rb = [], [], []
    for p in prs:
        mp = m[p]
        sw = pltpu.roll(mp, c, axis=1)
        a_ab.append(jnp.where(strict, cat([mp[0:c], sw[c:2 * c]]), 0.0))
        a_ak = jnp.where(strict, cat([sw[0:c], mp[c:2 * c]]), 0.0)
        a_rb.append(jnp.where(incl, cat([mp[2 * c:3 * c], sw[3 * c:4 * c]]), 0.0))
        a_rk = jnp.where(incl, cat([sw[2 * c:3 * c], mp[3 * c:4 * c]]), 0.0)
        a_kr.append(cat([a_ak, a_rk]).astype(BF16))
    xy = [_dg(a_kr[p], v2[p], NN) for p in prs]
    u = [sx[p][:2 * c] + xy[p][:2 * c] for p in prs]
    ypart = [sx[p][2 * c:] + xy[p][2 * c:] for p in prs]

    pw = a_ab
    n_sq = (c - 1).bit_length()
    for it in range(n_sq):
        u = [u[p] + _dot3(pw[p], u[p]) for p in prs]
        if it + 1 < n_sq:
            pw = [_dot3(pw[p], pw[p]) for p in prs]

    ub = [u[p].astype(BF16) for p in prs]
    y2 = [ypart[p] + _dg(a_rb[p].astype(BF16), ub[p], NN) for p in prs]
    for p in prs:
        ends = cat([stack2(lanes(bend, p)), stack2(lanes(kend, p))]).astype(BF16)
        s_ref[p] = s0[p] * lanes(e_tot, p) + _dg(cat([ub[p], v2[p]]), ends, TN)

    y_all = cat([y2[p][:c] + y2[p][c:] for p in prs])
    mean = _dot_exact_rhs(y_all, head_ones) * inv_n
    yc = y_all - mean
    var = _dot_exact_rhs(yc * yc, head_ones) * inv_n
    yn = yc * lax.rsqrt(var + RWKV_GN_EPS)
    rkk = r * k * rk_ref[...]
    bonus = _dot_exact_rhs(cat([lanes(rkk, p) for p in prs]), head_ones)
    lnw, lnb, g = lnw_ref[...], lnb_ref[...], g_ref[...]
    for p in prs:
        rows = slice(p * c, (p + 1) * c)
        out = (yn[rows] * lanes(lnw, p) + lanes(lnb, p) + bonus[rows] * lanes(v, p)) * lanes(g, p)
        o_ref[:, p * LANES:(p + 1) * LANES] = out.astype(o_ref.dtype)


def rwkv_recurrence(r, lw, k, v, kk, beta, g, ln_w, ln_b, r_k, batch, seq, pairs=8):
    m, width = r.shape
    c = CHUNK
    assert seq % c == 0 and 2 * c == LANES and 2 * RWKV_HEAD_DIM == LANES
    tc = pairs * LANES
    nchunk = seq // c
    blk = pl.BlockSpec((c, tc), lambda b, j, ci: (b * nchunk + ci, j))
    vec = pl.BlockSpec((1, tc), lambda b, j, ci: (0, j))
    return pl.pallas_call(
        functools.partial(_rwkv_kernel, pairs=pairs),
        grid=(batch, width // tc, nchunk),
        in_specs=[blk] * 7 + [vec] * 3,
        out_specs=blk,
        out_shape=jax.ShapeDtypeStruct((m, width), BF16),
        scratch_shapes=[pltpu.VMEM((pairs, LANES, LANES), F32)],
        compiler_params=_cparams(("parallel", "parallel", "arbitrary")),
        name="rwkv_recurrence",
    )(r, lw, k, v, kk, beta, g, ln_w, ln_b, r_k)


def _xattn_kernel(q_ref, k_ref, v_ref, o_ref, *, scale):
    s = _dg(q_ref[...], k_ref[...], NT) * scale
    m = jnp.max(s, axis=-1, keepdims=True)
    p = jnp.exp(s - m)
    l = jnp.sum(p, axis=-1, keepdims=True)
    o = jnp.dot(p.astype(BF16), v_ref[...], preferred_element_type=F32)
    o_ref[...] = (o / l).astype(o_ref.dtype)


def cross_attention(q, k, v, batch, seq, n_mem, heads, tq=512):
    m, d = q.shape
    dh = d // heads
    tq = min(tq, seq)
    nq = seq // tq
    return pl.pallas_call(
        functools.partial(_xattn_kernel, scale=dh ** -0.5),
        grid=(batch, heads, nq),
        in_specs=[pl.BlockSpec((tq, dh), lambda b, h, qi: (b * nq + qi, h)),
                  pl.BlockSpec((n_mem, dh), lambda b, h, qi: (b, h)),
                  pl.BlockSpec((n_mem, dh), lambda b, h, qi: (b, h))],
        out_specs=pl.BlockSpec((tq, dh), lambda b, h, qi: (b * nq + qi, h)),
        out_shape=jax.ShapeDtypeStruct((m, d), BF16),
        compiler_params=_cparams(("parallel", "parallel", "arbitrary")),
        name="cross_attention",
    )(q, k, v)


def _pad_cols(w, n):
    return jnp.pad(w, ((0, 0), (0, n - w.shape[1])))


def _pad_rows(w, n):
    return jnp.pad(w, ((0, n - w.shape[0]), (0, 0)))


def _bf(w):
    return w.astype(BF16)


def _row(t):
    return t.reshape(1, -1).astype(F32)


def _mixer(h, p, batch, seq):
    fox_w, rw = FOX_WIDTH, RWKV_WIDTH
    fox_heads = fox_w // FOX_HEAD_DIM
    bf, row = _bf, _row

    w_in = p['w_in']
    o_f = 3 * fox_w
    o_r = o_f + fox_heads
    o_w = o_r + 3 * rw
    o_a = o_w + DECAY_LORA
    o_g = o_a + ICLR_LORA
    w_rkv = bf(w_in[:, o_r:o_w])
    w_small = bf(jnp.concatenate([
        _pad_cols(w_in[:, o_w:o_a], LANES), _pad_cols(w_in[:, o_a:o_g], LANES),
        w_in[:, o_g:o_g + GATE_LORA], _pad_cols(w_in[:, o_f:o_r], LANES)], axis=1))
    mu = p['rwkv_mu']
    mu_small = jnp.concatenate([
        jnp.pad(mu[3 * rw:3 * rw + DECAY_LORA], (0, LANES - DECAY_LORA)),
        jnp.pad(mu[3 * rw + DECAY_LORA:3 * rw + DECAY_LORA + ICLR_LORA], (0, LANES - ICLR_LORA)),
        mu[3 * rw + DECAY_LORA + ICLR_LORA:], jnp.zeros((LANES,), F32)])

    q_scale = jnp.concatenate([jnp.full((1, fox_w), FOX_Q_SCALE, F32),
                               jnp.ones((1, 2 * fox_w), F32)], axis=1)
    qkv = matmul(h, w_in, BF16, tn=F32_W_TN, col_scale=q_scale, n=o_f)
    rkv = matmul(h, w_rkv, F32)
    small = matmul(h, w_small, F32, tn=w_small.shape[1])

    f_bias = jnp.pad(p['fox_f_bias'], (0, LANES - fox_heads)).reshape(1, LANES)
    c = fgate_cumsum(small, f_bias, f_block=(2 * LANES + GATE_LORA) // LANES, seq=seq)
    c = c[:, :fox_heads].reshape(batch, seq, fox_heads).transpose(0, 2, 1)
    y_fox = fox_attention(qkv, c[..., None], c[:, :, None, :], batch, seq, fox_heads)

    r, lw, k, v, kk, beta, g = rwkv_prep(
        rkv, small, row(mu[:rw]), row(mu[rw:2 * rw]), row(mu[2 * rw:3 * rw]), row(mu_small),
        row(p['rwkv_w0']), _pad_rows(p['rwkv_w_up'], LANES), row(p['rwkv_a0']),
        _pad_rows(p['rwkv_a_up'], LANES), p['rwkv_g_up'], row(p['rwkv_k_k']),
        row(p['rwkv_k_a']), seq=seq)
    y_rwkv = rwkv_recurrence(r, lw, k, v, kk, beta, g, row(p['rwkv_ln_w']),
                             row(p['rwkv_ln_b']), row(p['rwkv_r_k']), batch, seq)

    return matmul2(y_fox, y_rwkv, p['w_out'], F32, tn=F32_W_TN)


def _xattn(h, mn, p, batch, seq, n_mem):
    q = matmul(h, p['xattn_wq'], BF16, tn=F32_W_TN)
    km = matmul(mn, p['xattn_wk'], BF16, tn=F32_W_TN)
    vm = matmul(mn, p['xattn_wv'], BF16, tn=F32_W_TN)
    o = cross_attention(q, km, vm, batch, seq, n_mem, XATTN_HEADS)
    return matmul(o, p['xattn_wo'], F32, tn=F32_W_TN)


def _layer(x, mem, p, batch, seq, n_mem):
    h = rmsnorm(x, p['ffn1_pre_g'])
    y = ffn(h, p['ffn1_w_gate'], p['ffn1_w_up'], p['ffn1_w_down'])
    x, h = resnorm(x, y, p['ffn1_post_g'], 0.5, p['mix_pre_g'])

    mixed = _mixer(h, p, batch, seq)
    x, h = resnorm(x, mixed, p['mix_post_g'], 1.0, p['xattn_pre_g'])

    mn = rmsnorm(mem, p['mem_norm_g'])
    xa = _xattn(h, mn, p, batch, seq, n_mem)
    x, h = resnorm(x, xa, p['xattn_post_g'], 1.0, p['ffn2_pre_g'])

    y = ffn(h, p['ffn2_w_gate'], p['ffn2_w_up'], p['ffn2_w_down'])
    return resnorm(x, y, p['ffn2_post_g'], 0.5)


_PARAM_NAMES = (
    'ffn1_pre_g', 'ffn1_w_gate', 'ffn1_w_up', 'ffn1_w_down', 'ffn1_post_g',
    'mix_pre_g', 'w_in', 'fox_f_bias', 'rwkv_mu', 'rwkv_w0', 'rwkv_w_up', 'rwkv_a0',
    'rwkv_a_up', 'rwkv_g_up', 'rwkv_k_k', 'rwkv_k_a', 'rwkv_r_k', 'rwkv_ln_w', 'rwkv_ln_b',
    'w_out', 'mix_post_g', 'xattn_pre_g', 'mem_norm_g', 'xattn_wq', 'xattn_wk', 'xattn_wv',
    'xattn_wo', 'xattn_post_g', 'ffn2_pre_g', 'ffn2_w_gate', 'ffn2_w_up', 'ffn2_w_down',
    'ffn2_post_g')


def kernel(x, mem, ffn1_pre_g, ffn1_w_gate, ffn1_w_up, ffn1_w_down, ffn1_post_g, mix_pre_g, w_in, fox_f_bias, rwkv_mu, rwkv_w0, rwkv_w_up, rwkv_a0, rwkv_a_up, rwkv_g_up, rwkv_k_k, rwkv_k_a, rwkv_r_k, rwkv_ln_w, rwkv_ln_b, w_out, mix_post_g, xattn_pre_g, mem_norm_g, xattn_wq, xattn_wk, xattn_wv, xattn_wo, xattn_post_g, ffn2_pre_g, ffn2_w_gate, ffn2_w_up, ffn2_w_down, ffn2_post_g):
    weights = (ffn1_pre_g, ffn1_w_gate, ffn1_w_up, ffn1_w_down, ffn1_post_g, mix_pre_g, w_in,
               fox_f_bias, rwkv_mu, rwkv_w0, rwkv_w_up, rwkv_a0, rwkv_a_up, rwkv_g_up, rwkv_k_k,
               rwkv_k_a, rwkv_r_k, rwkv_ln_w, rwkv_ln_b, w_out, mix_post_g, xattn_pre_g,
               mem_norm_g, xattn_wq, xattn_wk, xattn_wv, xattn_wo, xattn_post_g, ffn2_pre_g,
               ffn2_w_gate, ffn2_w_up, ffn2_w_down, ffn2_post_g)
    batch, seq, d = x.shape
    n_mem = mem.shape[1]
    depth = ffn1_pre_g.shape[0]
    xf = x.reshape(batch * seq, d)
    mf = mem.reshape(batch * n_mem, d)
    for l in range(depth):
        p = {name: w[l] for name, w in zip(_PARAM_NAMES, weights)}
        xf = _layer(xf, mf, p, batch, seq, n_mem)
    return xf.reshape(batch, seq, d)
```

```python
import functools

import jax
import jax.numpy as jnp
from jax import lax
from jax.experimental import pallas as pl
from jax.experimental.pallas import tpu as pltpu

F32 = jnp.float32
BF16 = jnp.bfloat16

SEQ = 2048
FOX_HEAD_DIM = 128
RWKV_HEAD_DIM = 64
RWKV_WIDTH = 2048
FOX_WIDTH = 2048
DECAY_LORA = 96
ICLR_LORA = 96
GATE_LORA = 256
XATTN_HEADS = 4
RMS_EPS = 1e-6
RWKV_GN_EPS = 64e-5
NEG_INF = -1e30

LANES = 128
CHUNK = 64
VMEM_LIMIT = 56 * 1024 * 1024
FFN_VMEM_LIMIT = 62 * 1024 * 1024
F32_W_TN = 512


def _cparams(sem, vmem=VMEM_LIMIT):
    return pltpu.CompilerParams(dimension_semantics=sem, vmem_limit_bytes=vmem)


NN = (((1,), (0,)), ((), ()))
NT = (((1,), (1,)), ((), ()))
TN = (((0,), (0,)), ((), ()))


def _split2(x):
    hi = x.astype(BF16)
    lo = (x - hi.astype(F32)).astype(BF16)
    return hi, lo


def _split3(x):
    hi = x.astype(BF16)
    r = x - hi.astype(F32)
    mid = r.astype(BF16)
    lo = (r - mid.astype(F32)).astype(BF16)
    return hi, mid, lo


def _dg(a, b, dims):
    return lax.dot_general(a, b, dims, preferred_element_type=F32)


def _dot3(a, b, dims=NN):
    ah, al = _split2(a)
    bh, bl = _split2(b)
    return _dg(ah, bh, dims) + (_dg(ah, bl, dims) + _dg(al, bh, dims))


def _dot_exact_lhs(a_bf16, b, dims=NN):
    b0, b1, b2 = _split3(b)
    return _dg(a_bf16, b0, dims) + (_dg(a_bf16, b1, dims) + _dg(a_bf16, b2, dims))


def _dot_exact_rhs(a, b_bf16, dims=NN):
    a0, a1, a2 = _split3(a)
    return _dg(a0, b_bf16, dims) + (_dg(a1, b_bf16, dims) + _dg(a2, b_bf16, dims))


def _head_sum(a, ones_bf16):
    a0, a1 = _split2(a)
    return _dg(a0, ones_bf16, NN) + _dg(a1, ones_bf16, NN)


def _softplus(x):
    return jnp.maximum(x, 0.0) + jnp.log1p(jnp.exp(-jnp.abs(x)))


def _rmsnorm_kernel(x_ref, g_ref, o_ref):
    x = x_ref[...]
    ms = jnp.mean(x * x, axis=-1, keepdims=True)
    o_ref[...] = (x * lax.rsqrt(ms + RMS_EPS) * g_ref[...]).astype(o_ref.dtype)


def rmsnorm(x, g, tm=256):
    m, d = x.shape
    tm = min(tm, m)
    return pl.pallas_call(
        _rmsnorm_kernel,
        grid=(m // tm,),
        in_specs=[pl.BlockSpec((tm, d), lambda i: (i, 0)),
                  pl.BlockSpec((1, d), lambda i: (0, 0))],
        out_specs=pl.BlockSpec((tm, d), lambda i: (i, 0)),
        out_shape=jax.ShapeDtypeStruct((m, d), BF16),
        compiler_params=_cparams(("parallel",)),
        name="rmsnorm",
    )(x, g.reshape(1, d))


def _resnorm_kernel(x_ref, y_ref, gp_ref, gn_ref, xo_ref, h_ref, *, scale):
    y = y_ref[...]
    ms = jnp.mean(y * y, axis=-1, keepdims=True)
    xn = x_ref[...] + scale * (y * lax.rsqrt(ms + RMS_EPS) * gp_ref[...])
    xo_ref[...] = xn
    ms2 = jnp.mean(xn * xn, axis=-1, keepdims=True)
    h_ref[...] = (xn * lax.rsqrt(ms2 + RMS_EPS) * gn_ref[...]).astype(h_ref.dtype)


def _resnorm_last_kernel(x_ref, y_ref, gp_ref, xo_ref, *, scale):
    y = y_ref[...]
    ms = jnp.mean(y * y, axis=-1, keepdims=True)
    xo_ref[...] = x_ref[...] + scale * (y * lax.rsqrt(ms + RMS_EPS) * gp_ref[...])


def resnorm(x, y, g_post, scale, g_next=None, tm=256):
    m, d = x.shape
    row = pl.BlockSpec((tm, d), lambda i: (i, 0))
    vec = pl.BlockSpec((1, d), lambda i: (0, 0))
    if g_next is None:
        return pl.pallas_call(
            functools.partial(_resnorm_last_kernel, scale=scale),
            grid=(m // tm,),
            in_specs=[row, row, vec],
            out_specs=row,
            out_shape=jax.ShapeDtypeStruct((m, d), F32),
            compiler_params=_cparams(("parallel",)),
            name="resnorm_last",
        )(x, y, g_post.reshape(1, d))
    return pl.pallas_call(
        functools.partial(_resnorm_kernel, scale=scale),
        grid=(m // tm,),
        in_specs=[row, row, vec, vec],
        out_specs=[row, row],
        out_shape=[jax.ShapeDtypeStruct((m, d), F32),
                   jax.ShapeDtypeStruct((m, d), BF16)],
        compiler_params=_cparams(("parallel",)),
        name="resnorm",
    )(x, y, g_post.reshape(1, d), g_next.reshape(1, d))


def _mm_kernel(a_ref, w_ref, o_ref):
    o_ref[...] = jnp.dot(a_ref[...], w_ref[...].astype(BF16),
                         preferred_element_type=F32).astype(o_ref.dtype)


def _mm_scaled_kernel(a_ref, w_ref, s_ref, o_ref):
    acc = jnp.dot(a_ref[...], w_ref[...].astype(BF16), preferred_element_type=F32)
    o_ref[...] = (acc * s_ref[...]).astype(o_ref.dtype)


def _mm2_kernel(a1_ref, a2_ref, w1_ref, w2_ref, o_ref):
    acc = jnp.dot(a1_ref[...], w1_ref[...].astype(BF16), preferred_element_type=F32)
    acc += jnp.dot(a2_ref[...], w2_ref[...].astype(BF16), preferred_element_type=F32)
    o_ref[...] = acc.astype(o_ref.dtype)


def matmul(a, w, out_dtype, tm=1024, tn=1024, col_scale=None, n=None):
    m, k = a.shape
    n = w.shape[1] if n is None else n
    tm = min(tm, m)
    tn = min(tn, n)
    assert m % tm == 0 and n % tn == 0
    in_specs = [pl.BlockSpec((tm, k), lambda i, j: (i, 0)),
                pl.BlockSpec((k, tn), lambda i, j: (0, j))]
    args = (a, w)
    kern = _mm_kernel
    if col_scale is not None:
        in_specs.append(pl.BlockSpec((1, tn), lambda i, j: (0, j)))
        args = (a, w, col_scale)
        kern = _mm_scaled_kernel
    return pl.pallas_call(
        kern,
        grid=(m // tm, n // tn),
        in_specs=in_specs,
        out_specs=pl.BlockSpec((tm, tn), lambda i, j: (i, j)),
        out_shape=jax.ShapeDtypeStruct((m, n), out_dtype),
        compiler_params=_cparams(("parallel", "arbitrary")),
        name="matmul",
    )(*args)


def matmul2(a1, a2, w, out_dtype, tm=1024, tn=1024):
    m, k1 = a1.shape
    k2 = a2.shape[1]
    n = w.shape[1]
    assert k1 == k2 and w.shape[0] == k1 + k2
    tm = min(tm, m)
    tn = min(tn, n)
    assert m % tm == 0 and n % tn == 0
    return pl.pallas_call(
        _mm2_kernel,
        grid=(m // tm, n // tn),
        in_specs=[pl.BlockSpec((tm, k1), lambda i, j: (i, 0)),
                  pl.BlockSpec((tm, k2), lambda i, j: (i, 0)),
                  pl.BlockSpec((k1, tn), lambda i, j: (0, j)),
                  pl.BlockSpec((k2, tn), lambda i, j: (1, j))],
        out_specs=pl.BlockSpec((tm, tn), lambda i, j: (i, j)),
        out_shape=jax.ShapeDtypeStruct((m, n), out_dtype),
        compiler_params=_cparams(("parallel", "arbitrary")),
        name="matmul2",
    )(a1, a2, w, w)


def _ffn_kernel(h_ref, wg_ref, wu_ref, wd_ref, o_ref, *, n_chunk):
    f = pl.program_id(1)

    @pl.when(f == 0)
    def _():
        o_ref[...] = jnp.zeros_like(o_ref)

    h = h_ref[...]
    g = jnp.dot(h, wg_ref[...].astype(BF16), preferred_element_type=F32)
    u = jnp.dot(h, wu_ref[...].astype(BF16), preferred_element_type=F32)
    act = (g * jax.nn.sigmoid(g) * u).astype(BF16)
    d = o_ref.shape[1]
    for c in range(0, d, n_chunk):
        o_ref[:, c:c + n_chunk] += jnp.dot(act, wd_ref[:, c:c + n_chunk].astype(BF16),
                                           preferred_element_type=F32)


def ffn(h, wg, wu, wd, tm=1024, tf=256, n_chunk=512):
    m, d = h.shape
    dff = wg.shape[1]
    tm = min(tm, m)
    tf = min(tf, dff)
    n_chunk = min(n_chunk, d)
    assert m % tm == 0 and dff % tf == 0 and d % n_chunk == 0
    return pl.pallas_call(
        functools.partial(_ffn_kernel, n_chunk=n_chunk),
        grid=(m // tm, dff // tf),
        in_specs=[pl.BlockSpec((tm, d), lambda i, f: (i, 0), pipeline_mode=pl.Buffered(1)),
                  pl.BlockSpec((d, tf), lambda i, f: (0, f)),
                  pl.BlockSpec((d, tf), lambda i, f: (0, f)),
                  pl.BlockSpec((tf, d), lambda i, f: (f, 0))],
        out_specs=pl.BlockSpec((tm, d), lambda i, f: (i, 0), pipeline_mode=pl.Buffered(1)),
        out_shape=jax.ShapeDtypeStruct((m, d), F32),
        compiler_params=_cparams(("parallel", "arbitrary"), vmem=FFN_VMEM_LIMIT),
        name="ffn",
    )(h, wg, wu, wd)


def _fgate_kernel(f_ref, b_ref, c_ref, *, blk):
    s = f_ref.shape[0]
    row = lax.broadcasted_iota(jnp.int32, (blk, blk), 0)
    col = lax.broadcasted_iota(jnp.int32, (blk, blk), 1)
    tri = jnp.where(row >= col, 1.0, 0.0).astype(BF16)
    carry = jnp.zeros((1, f_ref.shape[1]), F32)
    for r0 in range(0, s, blk):
        x = f_ref[r0:r0 + blk, :] + b_ref[...]
        logf = -_softplus(-x)
        c = _dot_exact_lhs(tri, logf) + carry
        c_ref[r0:r0 + blk, :] = c
        carry = c[blk - 1:blk, :]


def fgate_cumsum(small, f_bias_row, f_block, seq, blk=256):
    m = small.shape[0]
    blk = min(blk, seq)
    return pl.pallas_call(
        functools.partial(_fgate_kernel, blk=blk),
        grid=(m // seq,),
        in_specs=[pl.BlockSpec((seq, LANES), lambda b: (b, f_block)),
                  pl.BlockSpec((1, LANES), lambda b: (0, 0))],
        out_specs=pl.BlockSpec((seq, LANES), lambda b: (b, 0)),
        out_shape=jax.ShapeDtypeStruct((m, LANES), F32),
        compiler_params=_cparams(("parallel",)),
        name="fgate_cumsum",
    )(small, f_bias_row)


LOG2E = 1.4426950408889634
FOX_Q_SCALE = FOX_HEAD_DIM ** -0.5 * LOG2E


def _fox_kernel(q_ref, k_ref, v_ref, cq_ref, ck_ref, o_ref, s_scr, p_scr, m_scr, cq_scr,
                acc_scr, *, tq, tk, hg, strip):
    qi = pl.program_id(2)
    dh = FOX_HEAD_DIM
    hs = range(hg)
    nl = tk // dh
    m_scr[...] = jnp.full_like(m_scr, NEG_INF)
    acc_scr[...] = jnp.zeros_like(acc_scr)
    for g in hs:
        cq_scr[g] = jnp.broadcast_to(cq_ref[g] * LOG2E, (tq, dh))
    ones = jnp.ones((tk, dh), BF16)
    col = lax.broadcasted_iota(jnp.int32, (strip, tk), 1)
    row = lax.broadcasted_iota(jnp.int32, (strip, tk), 0)

    def head_cols(g):
        return slice(g * dh, (g + 1) * dh)

    def lane_tile(x, n):
        return x if n == 1 else jnp.concatenate([x] * n, axis=1)

    def block(ks, r0, mask_off):
        for g in hs:
            s_scr[g, r0:, :] = _dg(q_ref[r0:, head_cols(g)],
                                   k_ref[pl.ds(ks, tk), head_cols(g)], NT)
        for g in hs:
            ckg = ck_ref[g, :, pl.ds(ks, tk)] * LOG2E
            for r in range(r0, tq, strip):
                rows = slice(r, r + strip)
                s = s_scr[g, rows, :] - ckg
                if mask_off is not None:
                    s = jnp.where(row + (r - mask_off) >= col, s, NEG_INF)
                cqr = cq_scr[g, rows, :]
                m_prev = m_scr[g, rows, :]
                m_new = jnp.maximum(m_prev, jnp.max(s, axis=-1, keepdims=True) + cqr)
                p_scr[g, rows, :] = jnp.exp2(s + lane_tile(cqr - m_new, nl)).astype(BF16)
                m_scr[g, rows, :] = m_new
                acc_scr[g, rows, :] = lane_tile(jnp.exp2(m_prev - m_new), 2) * acc_scr[g, rows, :]
        for g in hs:
            v_aug = jnp.concatenate([v_ref[pl.ds(ks, tk), head_cols(g)], ones], axis=1)
            acc_scr[g, r0:, :] += _dg(p_scr[g, r0:, :], v_aug, NN)

    def body(ki, carry):
        block(pl.multiple_of(ki * tk, tk), 0, None)
        return carry

    n_sub = tq // tk
    lax.fori_loop(0, qi * n_sub, body, 0)
    for j in range(n_sub):
        block(pl.multiple_of(qi * tq + j * tk, tk), j * tk, j * tk)
    for g in hs:
        acc = acc_scr[g]
        o_ref[:, head_cols(g)] = (acc[:, :dh] / acc[:, dh:]).astype(o_ref.dtype)


def fox_attention(qkv, c_col, c_row, batch, seq, heads, tq=512, tk=256, hg=2, strip=64):
    tq = min(tq, seq)
    tk = min(tk, tq)
    strip = min(strip, tk)
    nq = seq // tq
    dh = FOX_HEAD_DIM
    ng = heads // hg
    assert heads % hg == 0 and tq % tk == 0 and tk % strip == 0 and seq % tq == 0
    return pl.pallas_call(
        functools.partial(_fox_kernel, tq=tq, tk=tk, hg=hg, strip=strip),
        grid=(batch, ng, nq),
        in_specs=[
            pl.BlockSpec((tq, hg * dh), lambda b, h, qi: (b * nq + qi, h)),
            pl.BlockSpec((seq, hg * dh), lambda b, h, qi: (b, ng + h)),
            pl.BlockSpec((seq, hg * dh), lambda b, h, qi: (b, 2 * ng + h)),
            pl.BlockSpec((None, hg, tq, 1), lambda b, h, qi: (b, h, qi, 0)),
            pl.BlockSpec((None, hg, 1, seq), lambda b, h, qi: (b, h, 0, 0)),
        ],
        out_specs=pl.BlockSpec((tq, hg * dh), lambda b, h, qi: (b * nq + qi, h)),
        out_shape=jax.ShapeDtypeStruct((batch * seq, heads * dh), BF16),
        scratch_shapes=[pltpu.VMEM((hg, tq, tk), F32), pltpu.VMEM((hg, tq, tk), BF16),
                        pltpu.VMEM((hg, tq, dh), F32), pltpu.VMEM((hg, tq, dh), F32),
                        pltpu.VMEM((hg, tq, 2 * dh), F32)],
        compiler_params=_cparams(("parallel", "parallel", "arbitrary")),
        name="fox_attention",
    )(qkv, qkv, qkv, c_col, c_row)


def _rwkv_prep_kernel(r_ref, k_ref, v_ref, rp_ref, kp_ref, vp_ref, sm_ref, smp_ref,
                      mu_r_ref, mu_k_ref, mu_v_ref, mu_s_ref,
                      w0_ref, wup_ref, a0_ref, aup_ref, gup_ref, kk_ref, ka_ref,
                      r_o, lw_o, k_o, v_o, kk_o, b_o, g_o, *, tiles_per_seq):
    i = pl.program_id(0)
    first = (i % tiles_per_seq) == 0
    tm = r_ref.shape[0]
    row0 = lax.broadcasted_iota(jnp.int32, (tm, 1), 0) == 0

    def mix(cur_ref, prev_ref, mu_ref):
        cur = cur_ref[...]
        last = jnp.where(first, 0.0, prev_ref[7:8, :])
        prev = jnp.where(row0, last, pltpu.roll(cur, 1, axis=0))
        return cur + mu_ref[...] * (prev - cur)

    zr = mix(r_ref, rp_ref, mu_r_ref)
    zk = mix(k_ref, kp_ref, mu_k_ref)
    zv = mix(v_ref, vp_ref, mu_v_ref)
    zs = mix(sm_ref, smp_ref, mu_s_ref)
    zw = zs[:, 0:LANES]
    za = zs[:, LANES:2 * LANES]
    zg = zs[:, 2 * LANES:2 * LANES + GATE_LORA]

    w_log = -_softplus(-(w0_ref[...] + _dot3(jnp.tanh(zw), wup_ref[...]))) - 0.5
    lw_o[...] = -jnp.exp(w_log)
    a = jax.nn.sigmoid(a0_ref[...] + _dot3(za, aup_ref[...]))
    g_o[...] = _dot3(jax.nn.sigmoid(zg), gup_ref[...])

    tc = zk.shape[1]
    hr = lax.broadcasted_iota(jnp.int32, (tc, tc), 0) // RWKV_HEAD_DIM
    hc = lax.broadcasted_iota(jnp.int32, (tc, tc), 1) // RWKV_HEAD_DIM
    head_ones = jnp.where(hr == hc, 1.0, 0.0).astype(BF16)
    kk = zk * kk_ref[...]
    ss = _dot_exact_rhs(kk * kk, head_ones)
    kk = kk / jnp.maximum(jnp.sqrt(ss), 1e-12)
    r_o[...] = zr
    v_o[...] = zv
    kk_o[...] = kk
    b_o[...] = kk * a
    k_o[...] = zk * (1.0 + (a - 1.0) * ka_ref[...])


def rwkv_prep(rkv, small, mu_r, mu_k, mu_v, mu_s, w0, w_up, a0, a_up, g_up, k_k, k_a,
              seq, tm=512, tc=512):
    m = rkv.shape[0]
    width = RWKV_WIDTH
    tm = min(tm, seq)
    nc = width // tc
    ns = small.shape[1]
    tps = seq // tm
    sub = tm // 8

    def cur(piece):
        return pl.BlockSpec((tm, tc), lambda i, j: (i, piece * nc + j))

    def prev(piece):
        return pl.BlockSpec((8, tc), lambda i, j: (jnp.maximum(i * sub - 1, 0), piece * nc + j))

    colv = pl.BlockSpec((1, tc), lambda i, j: (0, j))
    out = pl.BlockSpec((tm, tc), lambda i, j: (i, j))
    outs = [jax.ShapeDtypeStruct((m, width), F32)] * 7
    return pl.pallas_call(
        functools.partial(_rwkv_prep_kernel, tiles_per_seq=tps),
        grid=(m // tm, nc),
        in_specs=[cur(0), cur(1), cur(2), prev(0), prev(1), prev(2),
                  pl.BlockSpec((tm, ns), lambda i, j: (i, 0)),
                  pl.BlockSpec((8, ns), lambda i, j: (jnp.maximum(i * sub - 1, 0), 0)),
                  colv, colv, colv,
                  pl.BlockSpec((1, ns), lambda i, j: (0, 0)),
                  colv,
                  pl.BlockSpec((LANES, tc), lambda i, j: (0, j)),
                  colv,
                  pl.BlockSpec((LANES, tc), lambda i, j: (0, j)),
                  pl.BlockSpec((GATE_LORA, tc), lambda i, j: (0, j)),
                  colv, colv],
        out_specs=[out] * 7,
        out_shape=outs,
        compiler_params=_cparams(("parallel", "arbitrary")),
        name="rwkv_prep",
    )(rkv, rkv, rkv, rkv, rkv, rkv, small, small,
      mu_r, mu_k, mu_v, mu_s, w0, w_up, a0, a_up, g_up, k_k, k_a)


def _rwkv_kernel(r_ref, lw_ref, k_ref, v_ref, kk_ref, b_ref, g_ref,
                 lnw_ref, lnb_ref, rk_ref, o_ref, s_ref, *, pairs):
    ci = pl.program_id(2)
    c = CHUNK
    hd = RWKV_HEAD_DIM

    @pl.when(ci == 0)
    def _():
        s_ref[...] = jnp.zeros_like(s_ref)

    r2 = lax.broadcasted_iota(jnp.int32, (LANES, LANES), 0)
    c2 = lax.broadcasted_iota(jnp.int32, (LANES, LANES), 1)
    same = (r2 < c) == (c2 < c)
    strict = jnp.logical_and(same, r2 > c2)
    incl = jnp.logical_and(same, r2 >= c2)
    head_ones = jnp.where((r2 < hd) == (c2 < hd), 1.0, 0.0).astype(BF16)
    rc = lax.broadcasted_iota(jnp.int32, (c, c), 0)
    cc = lax.broadcasted_iota(jnp.int32, (c, c), 1)
    tri_incl = jnp.where(rc >= cc, 1.0, 0.0).astype(BF16)
    m_a = lax.broadcasted_iota(jnp.int32, (1, LANES), 1) < hd
    inv_n = 1.0 / hd
    prs = range(pairs)
    cat = lambda xs: jnp.concatenate(xs, axis=0)

    def lanes(x, p):
        return x[:, p * LANES:(p + 1) * LANES]

    def stack2(x):
        return cat([jnp.where(m_a, x, 0.0), jnp.where(m_a, 0.0, x)])

    lw = lw_ref[...]
    r, k, v = r_ref[...], k_ref[...], v_ref[...]
    kk, beta = kk_ref[...], b_ref[...]
    big_l = _dot_exact_lhs(tri_incl, lw)
    l_tot = big_l[c - 1:c, :]
    e_nl = jnp.exp(-big_l)
    e_rem = jnp.exp(l_tot - big_l)
    e_tot = jnp.exp(l_tot)
    at = -kk * jnp.exp(big_l - lw)
    rt = r * jnp.exp(big_l)
    kt = k * e_nl
    bt = beta * e_nl
    bend = beta * e_rem
    kend = k * e_rem

    s0 = [s_ref[p] for p in prs]
    lhs = [cat([stack2(lanes(at, p)), stack2(lanes(rt, p))]).astype(BF16) for p in prs]
    wbk = [cat([lanes(bt, p), lanes(kt, p)]).astype(BF16) for p in prs]
    v2 = [stack2(lanes(v, p)).astype(BF16) for p in prs]
    m = [_dg(lhs[p], wbk[p], NT) for p in prs]
    sx = [_dg(lhs[p], s0[p].astype(BF16), NT) for p in prs]

    a_ab, a_kr, a_rb = [], [], []
    for p in prs:
        mp = m[p]
        sw = pltpu.roll(mp, c, axis=1)
        a_ab.append(jnp.where(strict, cat([mp[0:c], sw[c:2 * c]]), 0.0))
        a_ak = jnp.where(strict, cat([sw[0:c], mp[c:2 * c]]), 0.0)
        a_rb.append(jnp.where(incl, cat([mp[2 * c:3 * c], sw[3 * c:4 * c]]), 0.0))
        a_rk = jnp.where(incl, cat([sw[2 * c:3 * c], mp[3 * c:4 * c]]), 0.0)
        a_kr.append(cat([a_ak, a_rk]).astype(BF16))
    xy = [_dg(a_kr[p], v2[p], NN) for p in prs]
    u = [sx[p][:2 * c] + xy[p][:2 * c] for p in prs]
    ypart = [sx[p][2 * c:] + xy[p][2 * c:] for p in prs]

    def d1(a, b):
        return _dg(a.astype(BF16), b.astype(BF16), NN)

    blk = lambda n: (r2 // n) == (c2 // n)
    eye = jnp.where(r2 == c2, 1.0, 0.0)
    base = 8
    x_in = u
    p1 = [jnp.where(blk(base), a_ab[p], 0.0) for p in prs]
    p2 = [d1(p1[p], p1[p]) for p in prs]
    ip = [eye + p1[p] for p in prs]
    p4 = [d1(p2[p], p2[p]) for p in prs]
    m1 = [ip[p] + d1(ip[p], p2[p]) for p in prs]
    t = [m1[p] + d1(m1[p], p4[p]) for p in prs]
    n = base
    while n < c:
        quad = jnp.logical_and(blk(2 * n), jnp.logical_not(blk(n)))
        et = [d1(jnp.where(quad, a_ab[p], 0.0), t[p]) for p in prs]
        t = [t[p] + d1(t[p], et[p]) for p in prs]
        n *= 2
    u1 = [d1(t[p], x_in[p]) for p in prs]
    res = [x_in[p] - u1[p] + _dot3(a_ab[p], u1[p]) for p in prs]
    u = [u1[p] + d1(t[p], res[p]) for p in prs]

    ub = [u[p].astype(BF16) for p in prs]
    y2 = [ypart[p] + _dg(a_rb[p].astype(BF16), ub[p], NN) for p in prs]
    for p in prs:
        ends = cat([stack2(lanes(bend, p)), stack2(lanes(kend, p))]).astype(BF16)
        s_ref[p] = s0[p] * lanes(e_tot, p) + _dg(cat([ub[p], v2[p]]), ends, TN)

    y_all = cat([y2[p][:c] + y2[p][c:] for p in prs])
    mean = _head_sum(y_all, head_ones) * inv_n
    yc = y_all - mean
    var = _head_sum(yc * yc, head_ones) * inv_n
    yn = yc * lax.rsqrt(var + RWKV_GN_EPS)
    rkk = r * k * rk_ref[...]
    bonus = _head_sum(cat([lanes(rkk, p) for p in prs]), head_ones)
    lnw, lnb, g = lnw_ref[...], lnb_ref[...], g_ref[...]
    for p in prs:
        rows = slice(p * c, (p + 1) * c)
        out = (yn[rows] * lanes(lnw, p) + lanes(lnb, p) + bonus[rows] * lanes(v, p)) * lanes(g, p)
        o_ref[:, p * LANES:(p + 1) * LANES] = out.astype(o_ref.dtype)


def rwkv_recurrence(r, lw, k, v, kk, beta, g, ln_w, ln_b, r_k, batch, seq, pairs=8):
    m, width = r.shape
    c = CHUNK
    assert seq % c == 0 and 2 * c == LANES and 2 * RWKV_HEAD_DIM == LANES
    tc = pairs * LANES
    nchunk = seq // c
    blk = pl.BlockSpec((c, tc), lambda b, j, ci: (b * nchunk + ci, j))
    vec = pl.BlockSpec((1, tc), lambda b, j, ci: (0, j))
    return pl.pallas_call(
        functools.partial(_rwkv_kernel, pairs=pairs),
        grid=(batch, width // tc, nchunk),
        in_specs=[blk] * 7 + [vec] * 3,
        out_specs=blk,
        out_shape=jax.ShapeDtypeStruct((m, width), BF16),
        scratch_shapes=[pltpu.VMEM((pairs, LANES, LANES), F32)],
        compiler_params=_cparams(("parallel", "parallel", "arbitrary")),
        name="rwkv_recurrence",
    )(r, lw, k, v, kk, beta, g, ln_w, ln_b, r_k)


def _xattn_kernel(q_ref, k_ref, v_ref, o_ref, *, scale):
    s = _dg(q_ref[...], k_ref[...], NT) * scale
    m = jnp.max(s, axis=-1, keepdims=True)
    p = jnp.exp(s - m)
    l = jnp.sum(p, axis=-1, keepdims=True)
    o = jnp.dot(p.astype(BF16), v_ref[...], preferred_element_type=F32)
    o_ref[...] = (o / l).astype(o_ref.dtype)


def cross_attention(q, k, v, batch, seq, n_mem, heads, tq=512):
    m, d = q.shape
    dh = d // heads
    tq = min(tq, seq)
    nq = seq // tq
    return pl.pallas_call(
        functools.partial(_xattn_kernel, scale=dh ** -0.5),
        grid=(batch, heads, nq),
        in_specs=[pl.BlockSpec((tq, dh), lambda b, h, qi: (b * nq + qi, h)),
                  pl.BlockSpec((n_mem, dh), lambda b, h, qi: (b, h)),
                  pl.BlockSpec((n_mem, dh), lambda b, h, qi: (b, h))],
        out_specs=pl.BlockSpec((tq, dh), lambda b, h, qi: (b * nq + qi, h)),
        out_shape=jax.ShapeDtypeStruct((m, d), BF16),
        compiler_params=_cparams(("parallel", "parallel", "arbitrary")),
        name="cross_attention",
    )(q, k, v)


def _pad_cols(w, n):
    return jnp.pad(w, ((0, 0), (0, n - w.shape[1])))


def _pad_rows(w, n):
    return jnp.pad(w, ((0, n - w.shape[0]), (0, 0)))


def _bf(w):
    return w.astype(BF16)


def _row(t):
    return t.reshape(1, -1).astype(F32)


def _mixer(h, p, batch, seq):
    fox_w, rw = FOX_WIDTH, RWKV_WIDTH
    fox_heads = fox_w // FOX_HEAD_DIM
    bf, row = _bf, _row

    w_in = p['w_in']
    o_f = 3 * fox_w
    o_r = o_f + fox_heads
    o_w = o_r + 3 * rw
    o_a = o_w + DECAY_LORA
    o_g = o_a + ICLR_LORA
    w_tail = bf(w_in[:, o_f:])
    tail = lambda a, b: w_tail[:, a - o_f:b - o_f]
    w_rkv = tail(o_r, o_w)
    w_small = jnp.concatenate([
        _pad_cols(tail(o_w, o_a), LANES), _pad_cols(tail(o_a, o_g), LANES),
        tail(o_g, o_g + GATE_LORA), _pad_cols(tail(o_f, o_r), LANES)], axis=1)
    mu = p['rwkv_mu']
    mu_small = jnp.concatenate([
        jnp.pad(mu[3 * rw:3 * rw + DECAY_LORA], (0, LANES - DECAY_LORA)),
        jnp.pad(mu[3 * rw + DECAY_LORA:3 * rw + DECAY_LORA + ICLR_LORA], (0, LANES - ICLR_LORA)),
        mu[3 * rw + DECAY_LORA + ICLR_LORA:], jnp.zeros((LANES,), F32)])

    q_scale = jnp.concatenate([jnp.full((1, fox_w), FOX_Q_SCALE, F32),
                               jnp.ones((1, 2 * fox_w), F32)], axis=1)
    qkv = matmul(h, w_in, BF16, tn=F32_W_TN, col_scale=q_scale, n=o_f)
    rkv = matmul(h, w_rkv, F32)
    small = matmul(h, w_small, F32, tn=w_small.shape[1])

    f_bias = jnp.pad(p['fox_f_bias'], (0, LANES - fox_heads)).reshape(1, LANES)
    c = fgate_cumsum(small, f_bias, f_block=(2 * LANES + GATE_LORA) // LANES, seq=seq)
    c = c[:, :fox_heads].reshape(batch, seq, fox_heads).transpose(0, 2, 1)
    y_fox = fox_attention(qkv, c[..., None], c[:, :, None, :], batch, seq, fox_heads)

    r, lw, k, v, kk, beta, g = rwkv_prep(
        rkv, small, row(mu[:rw]), row(mu[rw:2 * rw]), row(mu[2 * rw:3 * rw]), row(mu_small),
        row(p['rwkv_w0']), _pad_rows(p['rwkv_w_up'], LANES), row(p['rwkv_a0']),
        _pad_rows(p['rwkv_a_up'], LANES), p['rwkv_g_up'], row(p['rwkv_k_k']),
        row(p['rwkv_k_a']), seq=seq)
    y_rwkv = rwkv_recurrence(r, lw, k, v, kk, beta, g, row(p['rwkv_ln_w']),
                             row(p['rwkv_ln_b']), row(p['rwkv_r_k']), batch, seq)

    return matmul2(y_fox, y_rwkv, p['w_out'], F32, tn=F32_W_TN)


def _xattn(h, mn, p, batch, seq, n_mem):
    q = matmul(h, p['xattn_wq'], BF16, tn=F32_W_TN)
    km = matmul(mn, p['xattn_wk'], BF16, tn=F32_W_TN)
    vm = matmul(mn, p['xattn_wv'], BF16, tn=F32_W_TN)
    o = cross_attention(q, km, vm, batch, seq, n_mem, XATTN_HEADS)
    return matmul(o, p['xattn_wo'], F32, tn=F32_W_TN)


def _layer(x, mem, p, batch, seq, n_mem):
    h = rmsnorm(x, p['ffn1_pre_g'])
    y = ffn(h, p['ffn1_w_gate'], p['ffn1_w_up'], p['ffn1_w_down'])
    x, h = resnorm(x, y, p['ffn1_post_g'], 0.5, p['mix_pre_g'])

    mixed = _mixer(h, p, batch, seq)
    x, h = resnorm(x, mixed, p['mix_post_g'], 1.0, p['xattn_pre_g'])

    mn = rmsnorm(mem, p['mem_norm_g'])
    xa = _xattn(h, mn, p, batch, seq, n_mem)
    x, h = resnorm(x, xa, p['xattn_post_g'], 1.0, p['ffn2_pre_g'])

    y = ffn(h, p['ffn2_w_gate'], p['ffn2_w_up'], p['ffn2_w_down'])
    return resnorm(x, y, p['ffn2_post_g'], 0.5)


_PARAM_NAMES = (
    'ffn1_pre_g', 'ffn1_w_gate', 'ffn1_w_up', 'ffn1_w_down', 'ffn1_post_g',
    'mix_pre_g', 'w_in', 'fox_f_bias', 'rwkv_mu', 'rwkv_w0', 'rwkv_w_up', 'rwkv_a0',
    'rwkv_a_up', 'rwkv_g_up', 'rwkv_k_k', 'rwkv_k_a', 'rwkv_r_k', 'rwkv_ln_w', 'rwkv_ln_b',
    'w_out', 'mix_post_g', 'xattn_pre_g', 'mem_norm_g', 'xattn_wq', 'xattn_wk', 'xattn_wv',
    'xattn_wo', 'xattn_post_g', 'ffn2_pre_g', 'ffn2_w_gate', 'ffn2_w_up', 'ffn2_w_down',
    'ffn2_post_g')


def kernel(x, mem, ffn1_pre_g, ffn1_w_gate, ffn1_w_up, ffn1_w_down, ffn1_post_g, mix_pre_g, w_in, fox_f_bias, rwkv_mu, rwkv_w0, rwkv_w_up, rwkv_a0, rwkv_a_up, rwkv_g_up, rwkv_k_k, rwkv_k_a, rwkv_r_k, rwkv_ln_w, rwkv_ln_b, w_out, mix_post_g, xattn_pre_g, mem_norm_g, xattn_wq, xattn_wk, xattn_wv, xattn_wo, xattn_post_g, ffn2_pre_g, ffn2_w_gate, ffn2_w_up, ffn2_w_down, ffn2_post_g):
    weights = (ffn1_pre_g, ffn1_w_gate, ffn1_w_up, ffn1_w_down, ffn1_post_g, mix_pre_g, w_in,
               fox_f_bias, rwkv_mu, rwkv_w0, rwkv_w_up, rwkv_a0, rwkv_a_up, rwkv_g_up, rwkv_k_k,
               rwkv_k_a, rwkv_r_k, rwkv_ln_w, rwkv_ln_b, w_out, mix_post_g, xattn_pre_g,
               mem_norm_g, xattn_wq, xattn_wk, xattn_wv, xattn_wo, xattn_post_g, ffn2_pre_g,
               ffn2_w_gate, ffn2_w_up, ffn2_w_down, ffn2_post_g)
    batch, seq, d = x.shape
    n_mem = mem.shape[1]
    depth = ffn1_pre_g.shape[0]
    xf = x.reshape(batch * seq, d)
    mf = mem.reshape(batch * n_mem, d)
    for l in range(depth):
        p = {name: w[l] for name, w in zip(_PARAM_NAMES, weights)}
        xf = _layer(xf, mf, p, batch, seq, n_mem)
    return xf.reshape(batch, seq, d)
```

```python
import functools

import jax
import jax.numpy as jnp
from jax import lax
from jax.experimental import pallas as pl
from jax.experimental.pallas import tpu as pltpu

F32 = jnp.float32
BF16 = jnp.bfloat16

SEQ = 2048
FOX_HEAD_DIM = 128
RWKV_HEAD_DIM = 64
RWKV_WIDTH = 2048
FOX_WIDTH = 2048
DECAY_LORA = 96
ICLR_LORA = 96
GATE_LORA = 256
XATTN_HEADS = 4
RMS_EPS = 1e-6
RWKV_GN_EPS = 64e-5
NEG_INF = -1e30

LANES = 128
CHUNK = 64
VMEM_LIMIT = 56 * 1024 * 1024
FFN_VMEM_LIMIT = 62 * 1024 * 1024
F32_W_TN = 512


def _cparams(sem, vmem=VMEM_LIMIT):
    return pltpu.CompilerParams(dimension_semantics=sem, vmem_limit_bytes=vmem)


NN = (((1,), (0,)), ((), ()))
NT = (((1,), (1,)), ((), ()))
TN = (((0,), (0,)), ((), ()))


def _split2(x):
    hi = x.astype(BF16)
    lo = (x - hi.astype(F32)).astype(BF16)
    return hi, lo


def _split3(x):
    hi = x.astype(BF16)
    r = x - hi.astype(F32)
    mid = r.astype(BF16)
    lo = (r - mid.astype(F32)).astype(BF16)
    return hi, mid, lo


def _dg(a, b, dims):
    return lax.dot_general(a, b, dims, preferred_element_type=F32)


def _dot3(a, b, dims=NN):
    ah, al = _split2(a)
    bh, bl = _split2(b)
    return _dg(ah, bh, dims) + (_dg(ah, bl, dims) + _dg(al, bh, dims))


def _dot_exact_lhs(a_bf16, b, dims=NN):
    b0, b1, b2 = _split3(b)
    return _dg(a_bf16, b0, dims) + (_dg(a_bf16, b1, dims) + _dg(a_bf16, b2, dims))


def _dot_exact_rhs(a, b_bf16, dims=NN):
    a0, a1, a2 = _split3(a)
    return _dg(a0, b_bf16, dims) + (_dg(a1, b_bf16, dims) + _dg(a2, b_bf16, dims))


def _head_sum(a, ones_bf16):
    a0, a1 = _split2(a)
    return _dg(a0, ones_bf16, NN) + _dg(a1, ones_bf16, NN)


def _softplus(x):
    return jnp.maximum(x, 0.0) + jnp.log1p(jnp.exp(-jnp.abs(x)))


def _rmsnorm_kernel(x_ref, g_ref, o_ref):
    x = x_ref[...]
    ms = jnp.mean(x * x, axis=-1, keepdims=True)
    o_ref[...] = (x * lax.rsqrt(ms + RMS_EPS) * g_ref[...]).astype(o_ref.dtype)


def rmsnorm(x, g, tm=256):
    m, d = x.shape
    tm = min(tm, m)
    return pl.pallas_call(
        _rmsnorm_kernel,
        grid=(m // tm,),
        in_specs=[pl.BlockSpec((tm, d), lambda i: (i, 0)),
                  pl.BlockSpec((1, d), lambda i: (0, 0))],
        out_specs=pl.BlockSpec((tm, d), lambda i: (i, 0)),
        out_shape=jax.ShapeDtypeStruct((m, d), BF16),
        compiler_params=_cparams(("parallel",)),
        name="rmsnorm",
    )(x, g.reshape(1, d))


def _resnorm_kernel(x_ref, y_ref, gp_ref, gn_ref, xo_ref, h_ref, *, scale):
    y = y_ref[...]
    ms = jnp.mean(y * y, axis=-1, keepdims=True)
    xn = x_ref[...] + scale * (y * lax.rsqrt(ms + RMS_EPS) * gp_ref[...])
    xo_ref[...] = xn
    ms2 = jnp.mean(xn * xn, axis=-1, keepdims=True)
    h_ref[...] = (xn * lax.rsqrt(ms2 + RMS_EPS) * gn_ref[...]).astype(h_ref.dtype)


def _resnorm_last_kernel(x_ref, y_ref, gp_ref, xo_ref, *, scale):
    y = y_ref[...]
    ms = jnp.mean(y * y, axis=-1, keepdims=True)
    xo_ref[...] = x_ref[...] + scale * (y * lax.rsqrt(ms + RMS_EPS) * gp_ref[...])


def resnorm(x, y, g_post, scale, g_next=None, tm=256):
    m, d = x.shape
    row = pl.BlockSpec((tm, d), lambda i: (i, 0))
    vec = pl.BlockSpec((1, d), lambda i: (0, 0))
    if g_next is None:
        return pl.pallas_call(
            functools.partial(_resnorm_last_kernel, scale=scale),
            grid=(m // tm,),
            in_specs=[row, row, vec],
            out_specs=row,
            out_shape=jax.ShapeDtypeStruct((m, d), F32),
            compiler_params=_cparams(("parallel",)),
            name="resnorm_last",
        )(x, y, g_post.reshape(1, d))
    return pl.pallas_call(
        functools.partial(_resnorm_kernel, scale=scale),
        grid=(m // tm,),
        in_specs=[row, row, vec, vec],
        out_specs=[row, row],
        out_shape=[jax.ShapeDtypeStruct((m, d), F32),
                   jax.ShapeDtypeStruct((m, d), BF16)],
        compiler_params=_cparams(("parallel",)),
        name="resnorm",
    )(x, y, g_post.reshape(1, d), g_next.reshape(1, d))


def _mm_kernel(a_ref, w_ref, o_ref):
    o_ref[...] = jnp.dot(a_ref[...], w_ref[...].astype(BF16),
                         preferred_element_type=F32).astype(o_ref.dtype)


def _mm_scaled_kernel(a_ref, w_ref, s_ref, o_ref):
    acc = jnp.dot(a_ref[...], w_ref[...].astype(BF16), preferred_element_type=F32)
    o_ref[...] = (acc * s_ref[...]).astype(o_ref.dtype)


def _mm2_kernel(a1_ref, a2_ref, w1_ref, w2_ref, o_ref):
    acc = jnp.dot(a1_ref[...], w1_ref[...].astype(BF16), preferred_element_type=F32)
    acc += jnp.dot(a2_ref[...], w2_ref[...].astype(BF16), preferred_element_type=F32)
    o_ref[...] = acc.astype(o_ref.dtype)


def matmul(a, w, out_dtype, tm=1024, tn=1024, col_scale=None, n=None):
    m, k = a.shape
    n = w.shape[1] if n is None else n
    tm = min(tm, m)
    tn = min(tn, n)
    assert m % tm == 0 and n % tn == 0
    in_specs = [pl.BlockSpec((tm, k), lambda i, j: (i, 0)),
                pl.BlockSpec((k, tn), lambda i, j: (0, j))]
    args = (a, w)
    kern = _mm_kernel
    if col_scale is not None:
        in_specs.append(pl.BlockSpec((1, tn), lambda i, j: (0, j)))
        args = (a, w, col_scale)
        kern = _mm_scaled_kernel
    return pl.pallas_call(
        kern,
        grid=(m // tm, n // tn),
        in_specs=in_specs,
        out_specs=pl.BlockSpec((tm, tn), lambda i, j: (i, j)),
        out_shape=jax.ShapeDtypeStruct((m, n), out_dtype),
        compiler_params=_cparams(("parallel", "arbitrary")),
        name="matmul",
    )(*args)


def _shift_cast_kernel(a_ref, b_ref, o_ref, *, shift):
    x = jnp.concatenate([a_ref[...], b_ref[...]], axis=1)
    o_ref[...] = x[:, shift:shift + o_ref.shape[1]].astype(o_ref.dtype)


def shift_cast(w, col0, ncols, tr=256, tc=512):
    k = w.shape[0]
    shift = col0 % LANES
    base = col0 // LANES
    assert ncols % tc == 0 and k % tr == 0 and tc % LANES == 0 and base % (tc // LANES) == 0
    assert (base + ncols // LANES + 1) * LANES <= w.shape[1]
    step = tc // LANES
    return pl.pallas_call(
        functools.partial(_shift_cast_kernel, shift=shift),
        grid=(k // tr, ncols // tc),
        in_specs=[pl.BlockSpec((tr, tc), lambda i, j: (i, base // step + j)),
                  pl.BlockSpec((tr, LANES), lambda i, j: (i, base + (j + 1) * step))],
        out_specs=pl.BlockSpec((tr, tc), lambda i, j: (i, j)),
        out_shape=jax.ShapeDtypeStruct((k, ncols), BF16),
        compiler_params=_cparams(("parallel", "parallel")),
        name="shift_cast",
    )(w, w)


def matmul2(a1, a2, w, out_dtype, tm=1024, tn=1024):
    m, k1 = a1.shape
    k2 = a2.shape[1]
    n = w.shape[1]
    assert k1 == k2 and w.shape[0] == k1 + k2
    tm = min(tm, m)
    tn = min(tn, n)
    assert m % tm == 0 and n % tn == 0
    return pl.pallas_call(
        _mm2_kernel,
        grid=(m // tm, n // tn),
        in_specs=[pl.BlockSpec((tm, k1), lambda i, j: (i, 0)),
                  pl.BlockSpec((tm, k2), lambda i, j: (i, 0)),
                  pl.BlockSpec((k1, tn), lambda i, j: (0, j)),
                  pl.BlockSpec((k2, tn), lambda i, j: (1, j))],
        out_specs=pl.BlockSpec((tm, tn), lambda i, j: (i, j)),
        out_shape=jax.ShapeDtypeStruct((m, n), out_dtype),
        compiler_params=_cparams(("parallel", "arbitrary")),
        name="matmul2",
    )(a1, a2, w, w)


def _ffn_kernel(h_ref, wg_ref, wu_ref, wd_ref, o_ref, *, n_chunk):
    f = pl.program_id(1)

    @pl.when(f == 0)
    def _():
        o_ref[...] = jnp.zeros_like(o_ref)

    h = h_ref[...]
    g = jnp.dot(h, wg_ref[...].astype(BF16), preferred_element_type=F32)
    u = jnp.dot(h, wu_ref[...].astype(BF16), preferred_element_type=F32)
    act = (g * jax.nn.sigmoid(g) * u).astype(BF16)
    d = o_ref.shape[1]
    for c in range(0, d, n_chunk):
        o_ref[:, c:c + n_chunk] += jnp.dot(act, wd_ref[:, c:c + n_chunk].astype(BF16),
                                           preferred_element_type=F32)


def ffn(h, wg, wu, wd, tm=1024, tf=256, n_chunk=512):
    m, d = h.shape
    dff = wg.shape[1]
    tm = min(tm, m)
    tf = min(tf, dff)
    n_chunk = min(n_chunk, d)
    assert m % tm == 0 and dff % tf == 0 and d % n_chunk == 0
    return pl.pallas_call(
        functools.partial(_ffn_kernel, n_chunk=n_chunk),
        grid=(m // tm, dff // tf),
        in_specs=[pl.BlockSpec((tm, d), lambda i, f: (i, 0), pipeline_mode=pl.Buffered(1)),
                  pl.BlockSpec((d, tf), lambda i, f: (0, f)),
                  pl.BlockSpec((d, tf), lambda i, f: (0, f)),
                  pl.BlockSpec((tf, d), lambda i, f: (f, 0))],
        out_specs=pl.BlockSpec((tm, d), lambda i, f: (i, 0), pipeline_mode=pl.Buffered(1)),
        out_shape=jax.ShapeDtypeStruct((m, d), F32),
        compiler_params=_cparams(("parallel", "arbitrary"), vmem=FFN_VMEM_LIMIT),
        name="ffn",
    )(h, wg, wu, wd)


def _fgate_kernel(f_ref, b_ref, c_ref, *, blk):
    s = f_ref.shape[0]
    row = lax.broadcasted_iota(jnp.int32, (blk, blk), 0)
    col = lax.broadcasted_iota(jnp.int32, (blk, blk), 1)
    tri = jnp.where(row >= col, 1.0, 0.0).astype(BF16)
    carry = jnp.zeros((1, f_ref.shape[1]), F32)
    for r0 in range(0, s, blk):
        x = f_ref[r0:r0 + blk, :] + b_ref[...]
        logf = -_softplus(-x)
        c = _dot_exact_lhs(tri, logf) + carry
        c_ref[r0:r0 + blk, :] = c
        carry = c[blk - 1:blk, :]


def fgate_cumsum(small, f_bias_row, f_block, seq, blk=256):
    m = small.shape[0]
    blk = min(blk, seq)
    return pl.pallas_call(
        functools.partial(_fgate_kernel, blk=blk),
        grid=(m // seq,),
        in_specs=[pl.BlockSpec((seq, LANES), lambda b: (b, f_block)),
                  pl.BlockSpec((1, LANES), lambda b: (0, 0))],
        out_specs=pl.BlockSpec((seq, LANES), lambda b: (b, 0)),
        out_shape=jax.ShapeDtypeStruct((m, LANES), F32),
        compiler_params=_cparams(("parallel",)),
        name="fgate_cumsum",
    )(small, f_bias_row)


LOG2E = 1.4426950408889634
FOX_Q_SCALE = FOX_HEAD_DIM ** -0.5 * LOG2E


def _fox_kernel(q_ref, k_ref, v_ref, cq_ref, ck_ref, o_ref, s_scr, p_scr, m_scr, cq_scr,
                acc_scr, *, tq, tk, hg, strip):
    qi = pl.program_id(2)
    dh = FOX_HEAD_DIM
    hs = range(hg)
    nl = tk // dh
    m_scr[...] = jnp.full_like(m_scr, NEG_INF)
    acc_scr[...] = jnp.zeros_like(acc_scr)
    for g in hs:
        cq_scr[g] = jnp.broadcast_to(cq_ref[g] * LOG2E, (tq, dh))
    ones = jnp.ones((tk, dh), BF16)
    col = lax.broadcasted_iota(jnp.int32, (strip, tk), 1)
    row = lax.broadcasted_iota(jnp.int32, (strip, tk), 0)

    def head_cols(g):
        return slice(g * dh, (g + 1) * dh)

    def lane_tile(x, n):
        return x if n == 1 else jnp.concatenate([x] * n, axis=1)

    def block(ks, r0, mask_off):
        for g in hs:
            s_scr[g, r0:, :] = _dg(q_ref[r0:, head_cols(g)],
                                   k_ref[pl.ds(ks, tk), head_cols(g)], NT)
        for g in hs:
            ckg = ck_ref[g, :, pl.ds(ks, tk)] * LOG2E
            for r in range(r0, tq, strip):
                rows = slice(r, r + strip)
                s = s_scr[g, rows, :] - ckg
                if mask_off is not None:
                    s = jnp.where(row + (r - mask_off) >= col, s, NEG_INF)
                cqr = cq_scr[g, rows, :]
                m_prev = m_scr[g, rows, :]
                m_new = jnp.maximum(m_prev, jnp.max(s, axis=-1, keepdims=True) + cqr)
                p_scr[g, rows, :] = jnp.exp2(s + lane_tile(cqr - m_new, nl)).astype(BF16)
                m_scr[g, rows, :] = m_new
                acc_scr[g, rows, :] = lane_tile(jnp.exp2(m_prev - m_new), 2) * acc_scr[g, rows, :]
        for g in hs:
            v_aug = jnp.concatenate([v_ref[pl.ds(ks, tk), head_cols(g)], ones], axis=1)
            acc_scr[g, r0:, :] += _dg(p_scr[g, r0:, :], v_aug, NN)

    def body(ki, carry):
        block(pl.multiple_of(ki * tk, tk), 0, None)
        return carry

    n_sub = tq // tk
    lax.fori_loop(0, qi * n_sub, body, 0)
    for j in range(n_sub):
        block(pl.multiple_of(qi * tq + j * tk, tk), j * tk, j * tk)
    for g in hs:
        acc = acc_scr[g]
        o_ref[:, head_cols(g)] = (acc[:, :dh] / acc[:, dh:]).astype(o_ref.dtype)


def fox_attention(qkv, c_col, c_row, batch, seq, heads, tq=512, tk=256, hg=2, strip=64):
    tq = min(tq, seq)
    tk = min(tk, tq)
    strip = min(strip, tk)
    nq = seq // tq
    dh = FOX_HEAD_DIM
    ng = heads // hg
    assert heads % hg == 0 and tq % tk == 0 and tk % strip == 0 and seq % tq == 0
    return pl.pallas_call(
        functools.partial(_fox_kernel, tq=tq, tk=tk, hg=hg, strip=strip),
        grid=(batch, ng, nq),
        in_specs=[
            pl.BlockSpec((tq, hg * dh), lambda b, h, qi: (b * nq + qi, h)),
            pl.BlockSpec((seq, hg * dh), lambda b, h, qi: (b, ng + h)),
            pl.BlockSpec((seq, hg * dh), lambda b, h, qi: (b, 2 * ng + h)),
            pl.BlockSpec((None, hg, tq, 1), lambda b, h, qi: (b, h, qi, 0)),
            pl.BlockSpec((None, hg, 1, seq), lambda b, h, qi: (b, h, 0, 0)),
        ],
        out_specs=pl.BlockSpec((tq, hg * dh), lambda b, h, qi: (b * nq + qi, h)),
        out_shape=jax.ShapeDtypeStruct((batch * seq, heads * dh), BF16),
        scratch_shapes=[pltpu.VMEM((hg, tq, tk), F32), pltpu.VMEM((hg, tq, tk), BF16),
                        pltpu.VMEM((hg, tq, dh), F32), pltpu.VMEM((hg, tq, dh), F32),
                        pltpu.VMEM((hg, tq, 2 * dh), F32)],
        compiler_params=_cparams(("parallel", "parallel", "arbitrary")),
        name="fox_attention",
    )(qkv, qkv, qkv, c_col, c_row)


def _rwkv_prep_kernel(r_ref, k_ref, v_ref, rp_ref, kp_ref, vp_ref, sm_ref, smp_ref,
                      mu_r_ref, mu_k_ref, mu_v_ref, mu_s_ref,
                      w0_ref, wup_ref, a0_ref, aup_ref, gup_ref, kk_ref, ka_ref,
                      r_o, lw_o, k_o, v_o, kk_o, b_o, g_o, *, tiles_per_seq):
    i = pl.program_id(0)
    first = (i % tiles_per_seq) == 0
    tm = r_ref.shape[0]
    row0 = lax.broadcasted_iota(jnp.int32, (tm, 1), 0) == 0

    def mix(cur_ref, prev_ref, mu_ref):
        cur = cur_ref[...]
        last = jnp.where(first, 0.0, prev_ref[7:8, :])
        prev = jnp.where(row0, last, pltpu.roll(cur, 1, axis=0))
        return cur + mu_ref[...] * (prev - cur)

    zr = mix(r_ref, rp_ref, mu_r_ref)
    zk = mix(k_ref, kp_ref, mu_k_ref)
    zv = mix(v_ref, vp_ref, mu_v_ref)
    zs = mix(sm_ref, smp_ref, mu_s_ref)
    zw = zs[:, 0:LANES]
    za = zs[:, LANES:2 * LANES]
    zg = zs[:, 2 * LANES:2 * LANES + GATE_LORA]

    w_log = -_softplus(-(w0_ref[...] + _dot3(jnp.tanh(zw), wup_ref[...]))) - 0.5
    lw_o[...] = -jnp.exp(w_log)
    a = jax.nn.sigmoid(a0_ref[...] + _dot3(za, aup_ref[...]))
    g_o[...] = _dot3(jax.nn.sigmoid(zg), gup_ref[...])

    tc = zk.shape[1]
    hr = lax.broadcasted_iota(jnp.int32, (tc, tc), 0) // RWKV_HEAD_DIM
    hc = lax.broadcasted_iota(jnp.int32, (tc, tc), 1) // RWKV_HEAD_DIM
    head_ones = jnp.where(hr == hc, 1.0, 0.0).astype(BF16)
    kk = zk * kk_ref[...]
    ss = _dot_exact_rhs(kk * kk, head_ones)
    kk = kk / jnp.maximum(jnp.sqrt(ss), 1e-12)
    r_o[...] = zr
    v_o[...] = zv
    kk_o[...] = kk
    b_o[...] = kk * a
    k_o[...] = zk * (1.0 + (a - 1.0) * ka_ref[...])


def rwkv_prep(rkv, small, mu_r, mu_k, mu_v, mu_s, w0, w_up, a0, a_up, g_up, k_k, k_a,
              seq, tm=512, tc=512):
    m = rkv.shape[0]
    width = RWKV_WIDTH
    tm = min(tm, seq)
    nc = width // tc
    ns = small.shape[1]
    tps = seq // tm
    sub = tm // 8

    def cur(piece):
        return pl.BlockSpec((tm, tc), lambda i, j: (i, piece * nc + j))

    def prev(piece):
        return pl.BlockSpec((8, tc), lambda i, j: (jnp.maximum(i * sub - 1, 0), piece * nc + j))

    colv = pl.BlockSpec((1, tc), lambda i, j: (0, j))
    out = pl.BlockSpec((tm, tc), lambda i, j: (i, j))
    outs = [jax.ShapeDtypeStruct((m, width), F32)] * 7
    return pl.pallas_call(
        functools.partial(_rwkv_prep_kernel, tiles_per_seq=tps),
        grid=(m // tm, nc),
        in_specs=[cur(0), cur(1), cur(2), prev(0), prev(1), prev(2),
                  pl.BlockSpec((tm, ns), lambda i, j: (i, 0)),
                  pl.BlockSpec((8, ns), lambda i, j: (jnp.maximum(i * sub - 1, 0), 0)),
                  colv, colv, colv,
                  pl.BlockSpec((1, ns), lambda i, j: (0, 0)),
                  colv,
                  pl.BlockSpec((LANES, tc), lambda i, j: (0, j)),
                  colv,
                  pl.BlockSpec((LANES, tc), lambda i, j: (0, j)),
                  pl.BlockSpec((GATE_LORA, tc), lambda i, j: (0, j)),
                  colv, colv],
        out_specs=[out] * 7,
        out_shape=outs,
        compiler_params=_cparams(("parallel", "arbitrary")),
        name="rwkv_prep",
    )(rkv, rkv, rkv, rkv, rkv, rkv, small, small,
      mu_r, mu_k, mu_v, mu_s, w0, w_up, a0, a_up, g_up, k_k, k_a)


def _rwkv_kernel(r_ref, lw_ref, k_ref, v_ref, kk_ref, b_ref, g_ref,
                 lnw_ref, lnb_ref, rk_ref, o_ref, s_ref, *, pairs):
    ci = pl.program_id(2)
    c = CHUNK
    hd = RWKV_HEAD_DIM

    @pl.when(ci == 0)
    def _():
        s_ref[...] = jnp.zeros_like(s_ref)

    r2 = lax.broadcasted_iota(jnp.int32, (LANES, LANES), 0)
    c2 = lax.broadcasted_iota(jnp.int32, (LANES, LANES), 1)
    same = (r2 < c) == (c2 < c)
    strict = jnp.logical_and(same, r2 > c2)
    incl = jnp.logical_and(same, r2 >= c2)
    head_ones = jnp.where((r2 < hd) == (c2 < hd), 1.0, 0.0).astype(BF16)
    rc = lax.broadcasted_iota(jnp.int32, (c, c), 0)
    cc = lax.broadcasted_iota(jnp.int32, (c, c), 1)
    tri_incl = jnp.where(rc >= cc, 1.0, 0.0).astype(BF16)
    m_a = lax.broadcasted_iota(jnp.int32, (1, LANES), 1) < hd
    inv_n = 1.0 / hd
    prs = range(pairs)
    cat = lambda xs: jnp.concatenate(xs, axis=0)

    def lanes(x, p):
        return x[:, p * LANES:(p + 1) * LANES]

    def stack2(x):
        return cat([jnp.where(m_a, x, 0.0), jnp.where(m_a, 0.0, x)])

    lw = lw_ref[...]
    r, k, v = r_ref[...], k_ref[...], v_ref[...]
    kk, beta = kk_ref[...], b_ref[...]
    big_l = _dot_exact_lhs(tri_incl, lw)
    l_tot = big_l[c - 1:c, :]
    e_nl = jnp.exp(-big_l)
    e_rem = jnp.exp(l_tot - big_l)
    e_tot = jnp.exp(l_tot)
    at = -kk * jnp.exp(big_l - lw)
    rt = r * jnp.exp(big_l)
    kt = k * e_nl
    bt = beta * e_nl
    bend = beta * e_rem
    kend = k * e_rem

    s0 = [s_ref[p] for p in prs]
    lhs = [cat([stack2(lanes(at, p)), stack2(lanes(rt, p))]).astype(BF16) for p in prs]
    wbk = [cat([lanes(bt, p), lanes(kt, p)]).astype(BF16) for p in prs]
    v2 = [stack2(lanes(v, p)).astype(BF16) for p in prs]
    m = [_dg(lhs[p], wbk[p], NT) for p in prs]
    sx = [_dg(lhs[p], s0[p].astype(BF16), NT) for p in prs]

    a_ab, a_kr, a_rb = [], [], []
    for p in prs:
        mp = m[p]
        sw = pltpu.roll(mp, c, axis=1)
        a_ab.append(jnp.where(strict, cat([mp[0:c], sw[c:2 * c]]), 0.0))
        a_ak = jnp.where(strict, cat([sw[0:c], mp[c:2 * c]]), 0.0)
        a_rb.append(jnp.where(incl, cat([mp[2 * c:3 * c], sw[3 * c:4 * c]]), 0.0))
        a_rk = jnp.where(incl, cat([sw[2 * c:3 * c], mp[3 * c:4 * c]]), 0.0)
        a_kr.append(cat([a_ak, a_rk]).astype(BF16))
    xy = [_dg(a_kr[p], v2[p], NN) for p in prs]
    u = [sx[p][:2 * c] + xy[p][:2 * c] for p in prs]
    ypart = [sx[p][2 * c:] + xy[p][2 * c:] for p in prs]

    def d1(a, b):
        return _dg(a.astype(BF16), b.astype(BF16), NN)

    blk = lambda n: (r2 // n) == (c2 // n)
    eye = jnp.where(r2 == c2, 1.0, 0.0)
    base = 8
    x_in = u
    p1 = [jnp.where(blk(base), a_ab[p], 0.0) for p in prs]
    p2 = [d1(p1[p], p1[p]) for p in prs]
    ip = [eye + p1[p] for p in prs]
    p4 = [d1(p2[p], p2[p]) for p in prs]
    m1 = [ip[p] + d1(ip[p], p2[p]) for p in prs]
    t = [m1[p] + d1(m1[p], p4[p]) for p in prs]
    n = base
    while n < c:
        quad = jnp.logical_and(blk(2 * n), jnp.logical_not(blk(n)))
        et = [d1(jnp.where(quad, a_ab[p], 0.0), t[p]) for p in prs]
        t = [t[p] + d1(t[p], et[p]) for p in prs]
        n *= 2
    u1 = [d1(t[p], x_in[p]) for p in prs]
    res = [x_in[p] - u1[p] + _dot3(a_ab[p], u1[p]) for p in prs]
    u = [u1[p] + d1(t[p], res[p]) for p in prs]

    ub = [u[p].astype(BF16) for p in prs]
    y2 = [ypart[p] + _dg(a_rb[p].astype(BF16), ub[p], NN) for p in prs]
    for p in prs:
        ends = cat([stack2(lanes(bend, p)), stack2(lanes(kend, p))]).astype(BF16)
        s_ref[p] = s0[p] * lanes(e_tot, p) + _dg(cat([ub[p], v2[p]]), ends, TN)

    y_all = cat([y2[p][:c] + y2[p][c:] for p in prs])
    mean = _head_sum(y_all, head_ones) * inv_n
    yc = y_all - mean
    var = _head_sum(yc * yc, head_ones) * inv_n
    yn = yc * lax.rsqrt(var + RWKV_GN_EPS)
    rkk = r * k * rk_ref[...]
    bonus = _head_sum(cat([lanes(rkk, p) for p in prs]), head_ones)
    lnw, lnb, g = lnw_ref[...], lnb_ref[...], g_ref[...]
    for p in prs:
        rows = slice(p * c, (p + 1) * c)
        out = (yn[rows] * lanes(lnw, p) + lanes(lnb, p) + bonus[rows] * lanes(v, p)) * lanes(g, p)
        o_ref[:, p * LANES:(p + 1) * LANES] = out.astype(o_ref.dtype)


def rwkv_recurrence(r, lw, k, v, kk, beta, g, ln_w, ln_b, r_k, batch, seq, pairs=8):
    m, width = r.shape
    c = CHUNK
    assert seq % c == 0 and 2 * c == LANES and 2 * RWKV_HEAD_DIM == LANES
    tc = pairs * LANES
    nchunk = seq // c
    blk = pl.BlockSpec((c, tc), lambda b, j, ci: (b * nchunk + ci, j))
    vec = pl.BlockSpec((1, tc), lambda b, j, ci: (0, j))
    return pl.pallas_call(
        functools.partial(_rwkv_kernel, pairs=pairs),
        grid=(batch, width // tc, nchunk),
        in_specs=[blk] * 7 + [vec] * 3,
        out_specs=blk,
        out_shape=jax.ShapeDtypeStruct((m, width), BF16),
        scratch_shapes=[pltpu.VMEM((pairs, LANES, LANES), F32)],
        compiler_params=_cparams(("parallel", "parallel", "arbitrary")),
        name="rwkv_recurrence",
    )(r, lw, k, v, kk, beta, g, ln_w, ln_b, r_k)


def _xattn_kernel(q_ref, k_ref, v_ref, o_ref, *, scale):
    s = _dg(q_ref[...], k_ref[...], NT) * scale
    m = jnp.max(s, axis=-1, keepdims=True)
    p = jnp.exp(s - m)
    l = jnp.sum(p, axis=-1, keepdims=True)
    o = jnp.dot(p.astype(BF16), v_ref[...], preferred_element_type=F32)
    o_ref[...] = (o / l).astype(o_ref.dtype)


def cross_attention(q, k, v, batch, seq, n_mem, heads, tq=512):
    m, d = q.shape
    dh = d // heads
    tq = min(tq, seq)
    nq = seq // tq
    return pl.pallas_call(
        functools.partial(_xattn_kernel, scale=dh ** -0.5),
        grid=(batch, heads, nq),
        in_specs=[pl.BlockSpec((tq, dh), lambda b, h, qi: (b * nq + qi, h)),
                  pl.BlockSpec((n_mem, dh), lambda b, h, qi: (b, h)),
                  pl.BlockSpec((n_mem, dh), lambda b, h, qi: (b, h))],
        out_specs=pl.BlockSpec((tq, dh), lambda b, h, qi: (b * nq + qi, h)),
        out_shape=jax.ShapeDtypeStruct((m, d), BF16),
        compiler_params=_cparams(("parallel", "parallel", "arbitrary")),
        name="cross_attention",
    )(q, k, v)


def _pad_cols(w, n):
    return jnp.pad(w, ((0, 0), (0, n - w.shape[1])))


def _pad_rows(w, n):
    return jnp.pad(w, ((0, n - w.shape[0]), (0, 0)))


def _bf(w):
    return w.astype(BF16)


def _row(t):
    return t.reshape(1, -1).astype(F32)


def _mixer(h, p, batch, seq):
    fox_w, rw = FOX_WIDTH, RWKV_WIDTH
    fox_heads = fox_w // FOX_HEAD_DIM
    bf, row = _bf, _row

    w_in = p['w_in']
    o_f = 3 * fox_w
    o_r = o_f + fox_heads
    o_w = o_r + 3 * rw
    o_a = o_w + DECAY_LORA
    o_g = o_a + ICLR_LORA
    w_rkv = shift_cast(w_in, o_r, 3 * rw)
    w_small = bf(jnp.concatenate([
        _pad_cols(w_in[:, o_w:o_a], LANES), _pad_cols(w_in[:, o_a:o_g], LANES),
        w_in[:, o_g:o_g + GATE_LORA], _pad_cols(w_in[:, o_f:o_r], LANES)], axis=1))
    mu = p['rwkv_mu']
    mu_small = jnp.concatenate([
        jnp.pad(mu[3 * rw:3 * rw + DECAY_LORA], (0, LANES - DECAY_LORA)),
        jnp.pad(mu[3 * rw + DECAY_LORA:3 * rw + DECAY_LORA + ICLR_LORA], (0, LANES - ICLR_LORA)),
        mu[3 * rw + DECAY_LORA + ICLR_LORA:], jnp.zeros((LANES,), F32)])

    q_scale = jnp.concatenate([jnp.full((1, fox_w), FOX_Q_SCALE, F32),
                               jnp.ones((1, 2 * fox_w), F32)], axis=1)
    qkv = matmul(h, w_in, BF16, tn=F32_W_TN, col_scale=q_scale, n=o_f)
    rkv = matmul(h, w_rkv, F32)
    small = matmul(h, w_small, F32, tn=w_small.shape[1])

    f_bias = jnp.pad(p['fox_f_bias'], (0, LANES - fox_heads)).reshape(1, LANES)
    c = fgate_cumsum(small, f_bias, f_block=(2 * LANES + GATE_LORA) // LANES, seq=seq)
    c = c[:, :fox_heads].reshape(batch, seq, fox_heads).transpose(0, 2, 1)
    y_fox = fox_attention(qkv, c[..., None], c[:, :, None, :], batch, seq, fox_heads)

    r, lw, k, v, kk, beta, g = rwkv_prep(
        rkv, small, row(mu[:rw]), row(mu[rw:2 * rw]), row(mu[2 * rw:3 * rw]), row(mu_small),
        row(p['rwkv_w0']), _pad_rows(p['rwkv_w_up'], LANES), row(p['rwkv_a0']),
        _pad_rows(p['rwkv_a_up'], LANES), p['rwkv_g_up'], row(p['rwkv_k_k']),
        row(p['rwkv_k_a']), seq=seq)
    y_rwkv = rwkv_recurrence(r, lw, k, v, kk, beta, g, row(p['rwkv_ln_w']),
                             row(p['rwkv_ln_b']), row(p['rwkv_r_k']), batch, seq)

    return matmul2(y_fox, y_rwkv, p['w_out'], F32, tn=F32_W_TN)


def _xattn(h, mn, p, batch, seq, n_mem):
    q = matmul(h, p['xattn_wq'], BF16, tn=F32_W_TN)
    km = matmul(mn, p['xattn_wk'], BF16, tn=F32_W_TN)
    vm = matmul(mn, p['xattn_wv'], BF16, tn=F32_W_TN)
    o = cross_attention(q, km, vm, batch, seq, n_mem, XATTN_HEADS)
    return matmul(o, p['xattn_wo'], F32, tn=F32_W_TN)


def _layer(x, mem, p, batch, seq, n_mem):
    h = rmsnorm(x, p['ffn1_pre_g'])
    y = ffn(h, p['ffn1_w_gate'], p['ffn1_w_up'], p['ffn1_w_down'])
    x, h = resnorm(x, y, p['ffn1_post_g'], 0.5, p['mix_pre_g'])

    mixed = _mixer(h, p, batch, seq)
    x, h = resnorm(x, mixed, p['mix_post_g'], 1.0, p['xattn_pre_g'])

    mn = rmsnorm(mem, p['mem_norm_g'])
    xa = _xattn(h, mn, p, batch, seq, n_mem)
    x, h = resnorm(x, xa, p['xattn_post_g'], 1.0, p['ffn2_pre_g'])

    y = ffn(h, p['ffn2_w_gate'], p['ffn2_w_up'], p['ffn2_w_down'])
    return resnorm(x, y, p['ffn2_post_g'], 0.5)


_PARAM_NAMES = (
    'ffn1_pre_g', 'ffn1_w_gate', 'ffn1_w_up', 'ffn1_w_down', 'ffn1_post_g',
    'mix_pre_g', 'w_in', 'fox_f_bias', 'rwkv_mu', 'rwkv_w0', 'rwkv_w_up', 'rwkv_a0',
    'rwkv_a_up', 'rwkv_g_up', 'rwkv_k_k', 'rwkv_k_a', 'rwkv_r_k', 'rwkv_ln_w', 'rwkv_ln_b',
    'w_out', 'mix_post_g', 'xattn_pre_g', 'mem_norm_g', 'xattn_wq', 'xattn_wk', 'xattn_wv',
    'xattn_wo', 'xattn_post_g', 'ffn2_pre_g', 'ffn2_w_gate', 'ffn2_w_up', 'ffn2_w_down',
    'ffn2_post_g')


def kernel(x, mem, ffn1_pre_g, ffn1_w_gate, ffn1_w_up, ffn1_w_down, ffn1_post_g, mix_pre_g, w_in, fox_f_bias, rwkv_mu, rwkv_w0, rwkv_w_up, rwkv_a0, rwkv_a_up, rwkv_g_up, rwkv_k_k, rwkv_k_a, rwkv_r_k, rwkv_ln_w, rwkv_ln_b, w_out, mix_post_g, xattn_pre_g, mem_norm_g, xattn_wq, xattn_wk, xattn_wv, xattn_wo, xattn_post_g, ffn2_pre_g, ffn2_w_gate, ffn2_w_up, ffn2_w_down, ffn2_post_g):
    weights = (ffn1_pre_g, ffn1_w_gate, ffn1_w_up, ffn1_w_down, ffn1_post_g, mix_pre_g, w_in,
               fox_f_bias, rwkv_mu, rwkv_w0, rwkv_w_up, rwkv_a0, rwkv_a_up, rwkv_g_up, rwkv_k_k,
               rwkv_k_a, rwkv_r_k, rwkv_ln_w, rwkv_ln_b, w_out, mix_post_g, xattn_pre_g,
               mem_norm_g, xattn_wq, xattn_wk, xattn_wv, xattn_wo, xattn_post_g, ffn2_pre_g,
               ffn2_w_gate, ffn2_w_up, ffn2_w_down, ffn2_post_g)
    batch, seq, d = x.shape
    n_mem = mem.shape[1]
    depth = ffn1_pre_g.shape[0]
    xf = x.reshape(batch * seq, d)
    mf = mem.reshape(batch * n_mem, d)
    for l in range(depth):
        p = {name: w[l] for name, w in zip(_PARAM_NAMES, weights)}
        xf = _layer(xf, mf, p, batch, seq, n_mem)
    return xf.reshape(batch, seq, d)
```

```python
import functools

import jax
import jax.numpy as jnp
from jax import lax
from jax.experimental import pallas as pl
from jax.experimental.pallas import tpu as pltpu

F32 = jnp.float32
BF16 = jnp.bfloat16

SEQ = 2048
FOX_HEAD_DIM = 128
RWKV_HEAD_DIM = 64
RWKV_WIDTH = 2048
FOX_WIDTH = 2048
DECAY_LORA = 96
ICLR_LORA = 96
GATE_LORA = 256
XATTN_HEADS = 4
RMS_EPS = 1e-6
RWKV_GN_EPS = 64e-5
NEG_INF = -1e30

LANES = 128
CHUNK = 64
VMEM_LIMIT = 56 * 1024 * 1024
FFN_VMEM_LIMIT = 62 * 1024 * 1024
F32_W_TN = 512


def _cparams(sem, vmem=VMEM_LIMIT):
    return pltpu.CompilerParams(dimension_semantics=sem, vmem_limit_bytes=vmem)


NN = (((1,), (0,)), ((), ()))
NT = (((1,), (1,)), ((), ()))
TN = (((0,), (0,)), ((), ()))


def _split2(x):
    hi = x.astype(BF16)
    lo = (x - hi.astype(F32)).astype(BF16)
    return hi, lo


def _split3(x):
    hi = x.astype(BF16)
    r = x - hi.astype(F32)
    mid = r.astype(BF16)
    lo = (r - mid.astype(F32)).astype(BF16)
    return hi, mid, lo


def _dg(a, b, dims):
    return lax.dot_general(a, b, dims, preferred_element_type=F32)


def _dot3(a, b, dims=NN):
    ah, al = _split2(a)
    bh, bl = _split2(b)
    return _dg(ah, bh, dims) + (_dg(ah, bl, dims) + _dg(al, bh, dims))


def _dot_exact_lhs(a_bf16, b, dims=NN):
    b0, b1, b2 = _split3(b)
    return _dg(a_bf16, b0, dims) + (_dg(a_bf16, b1, dims) + _dg(a_bf16, b2, dims))


def _dot_exact_rhs(a, b_bf16, dims=NN):
    a0, a1, a2 = _split3(a)
    return _dg(a0, b_bf16, dims) + (_dg(a1, b_bf16, dims) + _dg(a2, b_bf16, dims))


def _head_sum(a, ones_bf16):
    a0, a1 = _split2(a)
    return _dg(a0, ones_bf16, NN) + _dg(a1, ones_bf16, NN)


def _softplus(x):
    return jnp.maximum(x, 0.0) + jnp.log1p(jnp.exp(-jnp.abs(x)))


def _rmsnorm_kernel(x_ref, g_ref, o_ref):
    x = x_ref[...]
    ms = jnp.mean(x * x, axis=-1, keepdims=True)
    o_ref[...] = (x * lax.rsqrt(ms + RMS_EPS) * g_ref[...]).astype(o_ref.dtype)


def rmsnorm(x, g, tm=256):
    m, d = x.shape
    tm = min(tm, m)
    return pl.pallas_call(
        _rmsnorm_kernel,
        grid=(m // tm,),
        in_specs=[pl.BlockSpec((tm, d), lambda i: (i, 0)),
                  pl.BlockSpec((1, d), lambda i: (0, 0))],
        out_specs=pl.BlockSpec((tm, d), lambda i: (i, 0)),
        out_shape=jax.ShapeDtypeStruct((m, d), BF16),
        compiler_params=_cparams(("parallel",)),
        name="rmsnorm",
    )(x, g.reshape(1, d))


def _resnorm_kernel(x_ref, y_ref, gp_ref, gn_ref, xo_ref, h_ref, *, scale):
    y = y_ref[...]
    ms = jnp.mean(y * y, axis=-1, keepdims=True)
    xn = x_ref[...] + scale * (y * lax.rsqrt(ms + RMS_EPS) * gp_ref[...])
    xo_ref[...] = xn
    ms2 = jnp.mean(xn * xn, axis=-1, keepdims=True)
    h_ref[...] = (xn * lax.rsqrt(ms2 + RMS_EPS) * gn_ref[...]).astype(h_ref.dtype)


def _resnorm_last_kernel(x_ref, y_ref, gp_ref, xo_ref, *, scale):
    y = y_ref[...]
    ms = jnp.mean(y * y, axis=-1, keepdims=True)
    xo_ref[...] = x_ref[...] + scale * (y * lax.rsqrt(ms + RMS_EPS) * gp_ref[...])


def resnorm(x, y, g_post, scale, g_next=None, tm=256):
    m, d = x.shape
    row = pl.BlockSpec((tm, d), lambda i: (i, 0))
    vec = pl.BlockSpec((1, d), lambda i: (0, 0))
    if g_next is None:
        return pl.pallas_call(
            functools.partial(_resnorm_last_kernel, scale=scale),
            grid=(m // tm,),
            in_specs=[row, row, vec],
            out_specs=row,
            out_shape=jax.ShapeDtypeStruct((m, d), F32),
            compiler_params=_cparams(("parallel",)),
            name="resnorm_last",
        )(x, y, g_post.reshape(1, d))
    return pl.pallas_call(
        functools.partial(_resnorm_kernel, scale=scale),
        grid=(m // tm,),
        in_specs=[row, row, vec, vec],
        out_specs=[row, row],
        out_shape=[jax.ShapeDtypeStruct((m, d), F32),
                   jax.ShapeDtypeStruct((m, d), BF16)],
        compiler_params=_cparams(("parallel",)),
        name="resnorm",
    )(x, y, g_post.reshape(1, d), g_next.reshape(1, d))


def _mm_kernel(a_ref, w_ref, o_ref):
    o_ref[...] = jnp.dot(a_ref[...], w_ref[...].astype(BF16),
                         preferred_element_type=F32).astype(o_ref.dtype)


def _mm_scaled_kernel(a_ref, w_ref, s_ref, o_ref):
    acc = jnp.dot(a_ref[...], w_ref[...].astype(BF16), preferred_element_type=F32)
    o_ref[...] = (acc * s_ref[...]).astype(o_ref.dtype)


def _mm2_kernel(a1_ref, a2_ref, w1_ref, w2_ref, o_ref):
    acc = jnp.dot(a1_ref[...], w1_ref[...].astype(BF16), preferred_element_type=F32)
    acc += jnp.dot(a2_ref[...], w2_ref[...].astype(BF16), preferred_element_type=F32)
    o_ref[...] = acc.astype(o_ref.dtype)


def matmul(a, w, out_dtype, tm=1024, tn=1024, col_scale=None, n=None):
    m, k = a.shape
    n = w.shape[1] if n is None else n
    tm = min(tm, m)
    tn = min(tn, n)
    assert m % tm == 0 and n % tn == 0
    in_specs = [pl.BlockSpec((tm, k), lambda i, j: (i, 0)),
                pl.BlockSpec((k, tn), lambda i, j: (0, j))]
    args = (a, w)
    kern = _mm_kernel
    if col_scale is not None:
        in_specs.append(pl.BlockSpec((1, tn), lambda i, j: (0, j)))
        args = (a, w, col_scale)
        kern = _mm_scaled_kernel
    return pl.pallas_call(
        kern,
        grid=(m // tm, n // tn),
        in_specs=in_specs,
        out_specs=pl.BlockSpec((tm, tn), lambda i, j: (i, j)),
        out_shape=jax.ShapeDtypeStruct((m, n), out_dtype),
        compiler_params=_cparams(("parallel", "arbitrary")),
        name="matmul",
    )(*args)


def _mm_nt_kernel(a_ref, wt_ref, o_ref):
    o_ref[...] = _dg(a_ref[...], wt_ref[...].astype(BF16), NT).astype(o_ref.dtype)


def _mm_nt_scaled_kernel(a_ref, wt_ref, s_ref, o_ref):
    acc = _dg(a_ref[...], wt_ref[...].astype(BF16), NT)
    o_ref[...] = (acc * s_ref[...]).astype(o_ref.dtype)


def matmul_nt(a, wt, out_dtype, row0, n, tm=1024, tn=512, col_scale=None):
    m, k = a.shape
    tm = min(tm, m)
    tn = min(tn, n)
    assert m % tm == 0 and n % tn == 0 and row0 % 8 == 0 and wt.shape[1] == k
    in_specs = [pl.BlockSpec((tm, k), lambda i, j: (i, 0)),
                pl.BlockSpec((pl.Element(tn), pl.Element(k)),
                             lambda i, j: (pl.multiple_of(row0 + j * tn, 8), 0))]
    args = (a, wt)
    kern = _mm_nt_kernel
    if col_scale is not None:
        in_specs.append(pl.BlockSpec((1, tn), lambda i, j: (0, j)))
        args = (a, wt, col_scale)
        kern = _mm_nt_scaled_kernel
    return pl.pallas_call(
        kern,
        grid=(m // tm, n // tn),
        in_specs=in_specs,
        out_specs=pl.BlockSpec((tm, tn), lambda i, j: (i, j)),
        out_shape=jax.ShapeDtypeStruct((m, n), out_dtype),
        compiler_params=_cparams(("parallel", "arbitrary")),
        name="matmul_nt",
    )(*args)


def matmul2(a1, a2, w, out_dtype, tm=1024, tn=1024):
    m, k1 = a1.shape
    k2 = a2.shape[1]
    n = w.shape[1]
    assert k1 == k2 and w.shape[0] == k1 + k2
    tm = min(tm, m)
    tn = min(tn, n)
    assert m % tm == 0 and n % tn == 0
    return pl.pallas_call(
        _mm2_kernel,
        grid=(m // tm, n // tn),
        in_specs=[pl.BlockSpec((tm, k1), lambda i, j: (i, 0)),
                  pl.BlockSpec((tm, k2), lambda i, j: (i, 0)),
                  pl.BlockSpec((k1, tn), lambda i, j: (0, j)),
                  pl.BlockSpec((k2, tn), lambda i, j: (1, j))],
        out_specs=pl.BlockSpec((tm, tn), lambda i, j: (i, j)),
        out_shape=jax.ShapeDtypeStruct((m, n), out_dtype),
        compiler_params=_cparams(("parallel", "arbitrary")),
        name="matmul2",
    )(a1, a2, w, w)


def _ffn_kernel(h_ref, wg_ref, wu_ref, wd_ref, o_ref, *, n_chunk):
    f = pl.program_id(1)

    @pl.when(f == 0)
    def _():
        o_ref[...] = jnp.zeros_like(o_ref)

    h = h_ref[...]
    g = jnp.dot(h, wg_ref[...].astype(BF16), preferred_element_type=F32)
    u = jnp.dot(h, wu_ref[...].astype(BF16), preferred_element_type=F32)
    act = (g * jax.nn.sigmoid(g) * u).astype(BF16)
    d = o_ref.shape[1]
    for c in range(0, d, n_chunk):
        o_ref[:, c:c + n_chunk] += jnp.dot(act, wd_ref[:, c:c + n_chunk].astype(BF16),
                                           preferred_element_type=F32)


def ffn(h, wg, wu, wd, tm=1024, tf=256, n_chunk=512):
    m, d = h.shape
    dff = wg.shape[1]
    tm = min(tm, m)
    tf = min(tf, dff)
    n_chunk = min(n_chunk, d)
    assert m % tm == 0 and dff % tf == 0 and d % n_chunk == 0
    return pl.pallas_call(
        functools.partial(_ffn_kernel, n_chunk=n_chunk),
        grid=(m // tm, dff // tf),
        in_specs=[pl.BlockSpec((tm, d), lambda i, f: (i, 0), pipeline_mode=pl.Buffered(1)),
                  pl.BlockSpec((d, tf), lambda i, f: (0, f)),
                  pl.BlockSpec((d, tf), lambda i, f: (0, f)),
                  pl.BlockSpec((tf, d), lambda i, f: (f, 0))],
        out_specs=pl.BlockSpec((tm, d), lambda i, f: (i, 0), pipeline_mode=pl.Buffered(1)),
        out_shape=jax.ShapeDtypeStruct((m, d), F32),
        compiler_params=_cparams(("parallel", "arbitrary"), vmem=FFN_VMEM_LIMIT),
        name="ffn",
    )(h, wg, wu, wd)


def _fgate_kernel(f_ref, b_ref, c_ref, *, blk):
    s = f_ref.shape[0]
    row = lax.broadcasted_iota(jnp.int32, (blk, blk), 0)
    col = lax.broadcasted_iota(jnp.int32, (blk, blk), 1)
    tri = jnp.where(row >= col, 1.0, 0.0).astype(BF16)
    carry = jnp.zeros((1, f_ref.shape[1]), F32)
    for r0 in range(0, s, blk):
        x = f_ref[r0:r0 + blk, :] + b_ref[...]
        logf = -_softplus(-x)
        c = _dot_exact_lhs(tri, logf) + carry
        c_ref[r0:r0 + blk, :] = c
        carry = c[blk - 1:blk, :]


def fgate_cumsum(small, f_bias_row, f_block, seq, blk=256):
    m = small.shape[0]
    blk = min(blk, seq)
    return pl.pallas_call(
        functools.partial(_fgate_kernel, blk=blk),
        grid=(m // seq,),
        in_specs=[pl.BlockSpec((seq, LANES), lambda b: (b, f_block)),
                  pl.BlockSpec((1, LANES), lambda b: (0, 0))],
        out_specs=pl.BlockSpec((seq, LANES), lambda b: (b, 0)),
        out_shape=jax.ShapeDtypeStruct((m, LANES), F32),
        compiler_params=_cparams(("parallel",)),
        name="fgate_cumsum",
    )(small, f_bias_row)


LOG2E = 1.4426950408889634
FOX_Q_SCALE = FOX_HEAD_DIM ** -0.5 * LOG2E


def _fox_kernel(q_ref, k_ref, v_ref, cq_ref, ck_ref, o_ref, s_scr, p_scr, m_scr, cq_scr,
                acc_scr, *, tq, tk, hg, strip):
    qi = pl.program_id(2)
    dh = FOX_HEAD_DIM
    hs = range(hg)
    nl = tk // dh
    m_scr[...] = jnp.full_like(m_scr, NEG_INF)
    acc_scr[...] = jnp.zeros_like(acc_scr)
    for g in hs:
        cq_scr[g] = jnp.broadcast_to(cq_ref[g] * LOG2E, (tq, dh))
    ones = jnp.ones((tk, dh), BF16)
    col = lax.broadcasted_iota(jnp.int32, (strip, tk), 1)
    row = lax.broadcasted_iota(jnp.int32, (strip, tk), 0)

    def head_cols(g):
        return slice(g * dh, (g + 1) * dh)

    def lane_tile(x, n):
        return x if n == 1 else jnp.concatenate([x] * n, axis=1)

    def block(ks, r0, mask_off):
        for g in hs:
            s_scr[g, r0:, :] = _dg(q_ref[r0:, head_cols(g)],
                                   k_ref[pl.ds(ks, tk), head_cols(g)], NT)
        for g in hs:
            ckg = ck_ref[g, :, pl.ds(ks, tk)] * LOG2E
            for r in range(r0, tq, strip):
                rows = slice(r, r + strip)
                s = s_scr[g, rows, :] - ckg
                if mask_off is not None:
                    s = jnp.where(row + (r - mask_off) >= col, s, NEG_INF)
                cqr = cq_scr[g, rows, :]
                m_prev = m_scr[g, rows, :]
                m_new = jnp.maximum(m_prev, jnp.max(s, axis=-1, keepdims=True) + cqr)
                p_scr[g, rows, :] = jnp.exp2(s + lane_tile(cqr - m_new, nl)).astype(BF16)
                m_scr[g, rows, :] = m_new
                acc_scr[g, rows, :] = lane_tile(jnp.exp2(m_prev - m_new), 2) * acc_scr[g, rows, :]
        for g in hs:
            v_aug = jnp.concatenate([v_ref[pl.ds(ks, tk), head_cols(g)], ones], axis=1)
            acc_scr[g, r0:, :] += _dg(p_scr[g, r0:, :], v_aug, NN)

    def body(ki, carry):
        block(pl.multiple_of(ki * tk, tk), 0, None)
        return carry

    n_sub = tq // tk
    lax.fori_loop(0, qi * n_sub, body, 0)
    for j in range(n_sub):
        block(pl.multiple_of(qi * tq + j * tk, tk), j * tk, j * tk)
    for g in hs:
        acc = acc_scr[g]
        o_ref[:, head_cols(g)] = (acc[:, :dh] / acc[:, dh:]).astype(o_ref.dtype)


def fox_attention(qkv, c_col, c_row, batch, seq, heads, tq=512, tk=256, hg=2, strip=64):
    tq = min(tq, seq)
    tk = min(tk, tq)
    strip = min(strip, tk)
    nq = seq // tq
    dh = FOX_HEAD_DIM
    ng = heads // hg
    assert heads % hg == 0 and tq % tk == 0 and tk % strip == 0 and seq % tq == 0
    return pl.pallas_call(
        functools.partial(_fox_kernel, tq=tq, tk=tk, hg=hg, strip=strip),
        grid=(batch, ng, nq),
        in_specs=[
            pl.BlockSpec((tq, hg * dh), lambda b, h, qi: (b * nq + qi, h)),
            pl.BlockSpec((seq, hg * dh), lambda b, h, qi: (b, ng + h)),
            pl.BlockSpec((seq, hg * dh), lambda b, h, qi: (b, 2 * ng + h)),
            pl.BlockSpec((None, hg, tq, 1), lambda b, h, qi: (b, h, qi, 0)),
            pl.BlockSpec((None, hg, 1, seq), lambda b, h, qi: (b, h, 0, 0)),
        ],
        out_specs=pl.BlockSpec((tq, hg * dh), lambda b, h, qi: (b * nq + qi, h)),
        out_shape=jax.ShapeDtypeStruct((batch * seq, heads * dh), BF16),
        scratch_shapes=[pltpu.VMEM((hg, tq, tk), F32), pltpu.VMEM((hg, tq, tk), BF16),
                        pltpu.VMEM((hg, tq, dh), F32), pltpu.VMEM((hg, tq, dh), F32),
                        pltpu.VMEM((hg, tq, 2 * dh), F32)],
        compiler_params=_cparams(("parallel", "parallel", "arbitrary")),
        name="fox_attention",
    )(qkv, qkv, qkv, c_col, c_row)


def _rwkv_prep_kernel(r_ref, k_ref, v_ref, rp_ref, kp_ref, vp_ref, sm_ref, smp_ref,
                      mu_r_ref, mu_k_ref, mu_v_ref, mu_s_ref,
                      w0_ref, wup_ref, a0_ref, aup_ref, gup_ref, kk_ref, ka_ref,
                      r_o, lw_o, k_o, v_o, kk_o, b_o, g_o, *, tiles_per_seq):
    i = pl.program_id(0)
    first = (i % tiles_per_seq) == 0
    tm = r_ref.shape[0]
    row0 = lax.broadcasted_iota(jnp.int32, (tm, 1), 0) == 0

    def mix(cur_ref, prev_ref, mu_ref):
        cur = cur_ref[...]
        last = jnp.where(first, 0.0, prev_ref[7:8, :])
        prev = jnp.where(row0, last, pltpu.roll(cur, 1, axis=0))
        return cur + mu_ref[...] * (prev - cur)

    zr = mix(r_ref, rp_ref, mu_r_ref)
    zk = mix(k_ref, kp_ref, mu_k_ref)
    zv = mix(v_ref, vp_ref, mu_v_ref)
    zs = mix(sm_ref, smp_ref, mu_s_ref)
    zw = zs[:, 0:LANES]
    za = zs[:, LANES:2 * LANES]
    zg = zs[:, 2 * LANES:2 * LANES + GATE_LORA]

    w_log = -_softplus(-(w0_ref[...] + _dot3(jnp.tanh(zw), wup_ref[...]))) - 0.5
    lw_o[...] = -jnp.exp(w_log)
    a = jax.nn.sigmoid(a0_ref[...] + _dot3(za, aup_ref[...]))
    g_o[...] = _dot3(jax.nn.sigmoid(zg), gup_ref[...])

    tc = zk.shape[1]
    hr = lax.broadcasted_iota(jnp.int32, (tc, tc), 0) // RWKV_HEAD_DIM
    hc = lax.broadcasted_iota(jnp.int32, (tc, tc), 1) // RWKV_HEAD_DIM
    head_ones = jnp.where(hr == hc, 1.0, 0.0).astype(BF16)
    kk = zk * kk_ref[...]
    ss = _dot_exact_rhs(kk * kk, head_ones)
    kk = kk / jnp.maximum(jnp.sqrt(ss), 1e-12)
    r_o[...] = zr
    v_o[...] = zv
    kk_o[...] = kk
    b_o[...] = kk * a
    k_o[...] = zk * (1.0 + (a - 1.0) * ka_ref[...])


def rwkv_prep(rkv, small, mu_r, mu_k, mu_v, mu_s, w0, w_up, a0, a_up, g_up, k_k, k_a,
              seq, tm=512, tc=512):
    m = rkv.shape[0]
    width = RWKV_WIDTH
    tm = min(tm, seq)
    nc = width // tc
    ns = small.shape[1]
    tps = seq // tm
    sub = tm // 8

    def cur(piece):
        return pl.BlockSpec((tm, tc), lambda i, j: (i, piece * nc + j))

    def prev(piece):
        return pl.BlockSpec((8, tc), lambda i, j: (jnp.maximum(i * sub - 1, 0), piece * nc + j))

    colv = pl.BlockSpec((1, tc), lambda i, j: (0, j))
    out = pl.BlockSpec((tm, tc), lambda i, j: (i, j))
    outs = [jax.ShapeDtypeStruct((m, width), F32)] * 7
    return pl.pallas_call(
        functools.partial(_rwkv_prep_kernel, tiles_per_seq=tps),
        grid=(m // tm, nc),
        in_specs=[cur(0), cur(1), cur(2), prev(0), prev(1), prev(2),
                  pl.BlockSpec((tm, ns), lambda i, j: (i, 0)),
                  pl.BlockSpec((8, ns), lambda i, j: (jnp.maximum(i * sub - 1, 0), 0)),
                  colv, colv, colv,
                  pl.BlockSpec((1, ns), lambda i, j: (0, 0)),
                  colv,
                  pl.BlockSpec((LANES, tc), lambda i, j: (0, j)),
                  colv,
                  pl.BlockSpec((LANES, tc), lambda i, j: (0, j)),
                  pl.BlockSpec((GATE_LORA, tc), lambda i, j: (0, j)),
                  colv, colv],
        out_specs=[out] * 7,
        out_shape=outs,
        compiler_params=_cparams(("parallel", "arbitrary")),
        name="rwkv_prep",
    )(rkv, rkv, rkv, rkv, rkv, rkv, small, small,
      mu_r, mu_k, mu_v, mu_s, w0, w_up, a0, a_up, g_up, k_k, k_a)


def _rwkv_kernel(r_ref, lw_ref, k_ref, v_ref, kk_ref, b_ref, g_ref,
                 lnw_ref, lnb_ref, rk_ref, o_ref, s_ref, *, pairs):
    ci = pl.program_id(2)
    c = CHUNK
    hd = RWKV_HEAD_DIM

    @pl.when(ci == 0)
    def _():
        s_ref[...] = jnp.zeros_like(s_ref)

    r2 = lax.broadcasted_iota(jnp.int32, (LANES, LANES), 0)
    c2 = lax.broadcasted_iota(jnp.int32, (LANES, LANES), 1)
    same = (r2 < c) == (c2 < c)
    strict = jnp.logical_and(same, r2 > c2)
    incl = jnp.logical_and(same, r2 >= c2)
    head_ones = jnp.where((r2 < hd) == (c2 < hd), 1.0, 0.0).astype(BF16)
    rc = lax.broadcasted_iota(jnp.int32, (c, c), 0)
    cc = lax.broadcasted_iota(jnp.int32, (c, c), 1)
    tri_incl = jnp.where(rc >= cc, 1.0, 0.0).astype(BF16)
    m_a = lax.broadcasted_iota(jnp.int32, (1, LANES), 1) < hd
    inv_n = 1.0 / hd
    prs = range(pairs)
    cat = lambda xs: jnp.concatenate(xs, axis=0)

    def lanes(x, p):
        return x[:, p * LANES:(p + 1) * LANES]

    def stack2(x):
        return cat([jnp.where(m_a, x, 0.0), jnp.where(m_a, 0.0, x)])

    lw = lw_ref[...]
    r, k, v = r_ref[...], k_ref[...], v_ref[...]
    kk, beta = kk_ref[...], b_ref[...]
    big_l = _dot_exact_lhs(tri_incl, lw)
    l_tot = big_l[c - 1:c, :]
    e_nl = jnp.exp(-big_l)
    e_rem = jnp.exp(l_tot - big_l)
    e_tot = jnp.exp(l_tot)
    at = -kk * jnp.exp(big_l - lw)
    rt = r * jnp.exp(big_l)
    kt = k * e_nl
    bt = beta * e_nl
    bend = beta * e_rem
    kend = k * e_rem

    s0 = [s_ref[p] for p in prs]
    lhs = [cat([stack2(lanes(at, p)), stack2(lanes(rt, p))]).astype(BF16) for p in prs]
    wbk = [cat([lanes(bt, p), lanes(kt, p)]).astype(BF16) for p in prs]
    v2 = [stack2(lanes(v, p)).astype(BF16) for p in prs]
    m = [_dg(lhs[p], wbk[p], NT) for p in prs]
    sx = [_dg(lhs[p], s0[p].astype(BF16), NT) for p in prs]

    a_ab, a_kr, a_rb = [], [], []
    for p in prs:
        mp = m[p]
        sw = pltpu.roll(mp, c, axis=1)
        a_ab.append(jnp.where(strict, cat([mp[0:c], sw[c:2 * c]]), 0.0))
        a_ak = jnp.where(strict, cat([sw[0:c], mp[c:2 * c]]), 0.0)
        a_rb.append(jnp.where(incl, cat([mp[2 * c:3 * c], sw[3 * c:4 * c]]), 0.0))
        a_rk = jnp.where(incl, cat([sw[2 * c:3 * c], mp[3 * c:4 * c]]), 0.0)
        a_kr.append(cat([a_ak, a_rk]).astype(BF16))
    xy = [_dg(a_kr[p], v2[p], NN) for p in prs]
    u = [sx[p][:2 * c] + xy[p][:2 * c] for p in prs]
    ypart = [sx[p][2 * c:] + xy[p][2 * c:] for p in prs]

    def d1(a, b):
        return _dg(a.astype(BF16), b.astype(BF16), NN)

    blk = lambda n: (r2 // n) == (c2 // n)
    eye = jnp.where(r2 == c2, 1.0, 0.0)
    base = 8
    x_in = u
    p1 = [jnp.where(blk(base), a_ab[p], 0.0) for p in prs]
    p2 = [d1(p1[p], p1[p]) for p in prs]
    ip = [eye + p1[p] for p in prs]
    p4 = [d1(p2[p], p2[p]) for p in prs]
    m1 = [ip[p] + d1(ip[p], p2[p]) for p in prs]
    t = [m1[p] + d1(m1[p], p4[p]) for p in prs]
    n = base
    while n < c:
        quad = jnp.logical_and(blk(2 * n), jnp.logical_not(blk(n)))
        et = [d1(jnp.where(quad, a_ab[p], 0.0), t[p]) for p in prs]
        t = [t[p] + d1(t[p], et[p]) for p in prs]
        n *= 2
    u1 = [d1(t[p], x_in[p]) for p in prs]
    res = [x_in[p] - u1[p] + _dot3(a_ab[p], u1[p]) for p in prs]
    u = [u1[p] + d1(t[p], res[p]) for p in prs]

    ub = [u[p].astype(BF16) for p in prs]
    y2 = [ypart[p] + _dg(a_rb[p].astype(BF16), ub[p], NN) for p in prs]
    for p in prs:
        ends = cat([stack2(lanes(bend, p)), stack2(lanes(kend, p))]).astype(BF16)
        s_ref[p] = s0[p] * lanes(e_tot, p) + _dg(cat([ub[p], v2[p]]), ends, TN)

    y_all = cat([y2[p][:c] + y2[p][c:] for p in prs])
    mean = _head_sum(y_all, head_ones) * inv_n
    yc = y_all - mean
    var = _head_sum(yc * yc, head_ones) * inv_n
    yn = yc * lax.rsqrt(var + RWKV_GN_EPS)
    rkk = r * k * rk_ref[...]
    bonus = _head_sum(cat([lanes(rkk, p) for p in prs]), head_ones)
    lnw, lnb, g = lnw_ref[...], lnb_ref[...], g_ref[...]
    for p in prs:
        rows = slice(p * c, (p + 1) * c)
        out = (yn[rows] * lanes(lnw, p) + lanes(lnb, p) + bonus[rows] * lanes(v, p)) * lanes(g, p)
        o_ref[:, p * LANES:(p + 1) * LANES] = out.astype(o_ref.dtype)


def rwkv_recurrence(r, lw, k, v, kk, beta, g, ln_w, ln_b, r_k, batch, seq, pairs=8):
    m, width = r.shape
    c = CHUNK
    assert seq % c == 0 and 2 * c == LANES and 2 * RWKV_HEAD_DIM == LANES
    tc = pairs * LANES
    nchunk = seq // c
    blk = pl.BlockSpec((c, tc), lambda b, j, ci: (b * nchunk + ci, j))
    vec = pl.BlockSpec((1, tc), lambda b, j, ci: (0, j))
    return pl.pallas_call(
        functools.partial(_rwkv_kernel, pairs=pairs),
        grid=(batch, width // tc, nchunk),
        in_specs=[blk] * 7 + [vec] * 3,
        out_specs=blk,
        out_shape=jax.ShapeDtypeStruct((m, width), BF16),
        scratch_shapes=[pltpu.VMEM((pairs, LANES, LANES), F32)],
        compiler_params=_cparams(("parallel", "parallel", "arbitrary")),
        name="rwkv_recurrence",
    )(r, lw, k, v, kk, beta, g, ln_w, ln_b, r_k)


def _xattn_kernel(q_ref, k_ref, v_ref, o_ref, *, scale):
    s = _dg(q_ref[...], k_ref[...], NT) * scale
    m = jnp.max(s, axis=-1, keepdims=True)
    p = jnp.exp(s - m)
    l = jnp.sum(p, axis=-1, keepdims=True)
    o = jnp.dot(p.astype(BF16), v_ref[...], preferred_element_type=F32)
    o_ref[...] = (o / l).astype(o_ref.dtype)


def cross_attention(q, k, v, batch, seq, n_mem, heads, tq=512):
    m, d = q.shape
    dh = d // heads
    tq = min(tq, seq)
    nq = seq // tq
    return pl.pallas_call(
        functools.partial(_xattn_kernel, scale=dh ** -0.5),
        grid=(batch, heads, nq),
        in_specs=[pl.BlockSpec((tq, dh), lambda b, h, qi: (b * nq + qi, h)),
                  pl.BlockSpec((n_mem, dh), lambda b, h, qi: (b, h)),
                  pl.BlockSpec((n_mem, dh), lambda b, h, qi: (b, h))],
        out_specs=pl.BlockSpec((tq, dh), lambda b, h, qi: (b * nq + qi, h)),
        out_shape=jax.ShapeDtypeStruct((m, d), BF16),
        compiler_params=_cparams(("parallel", "parallel", "arbitrary")),
        name="cross_attention",
    )(q, k, v)


def _pad_cols(w, n):
    return jnp.pad(w, ((0, 0), (0, n - w.shape[1])))


def _pad_rows(w, n):
    return jnp.pad(w, ((0, n - w.shape[0]), (0, 0)))


def _bf(w):
    return w.astype(BF16)


def _row(t):
    return t.reshape(1, -1).astype(F32)


def _mixer(h, p, batch, seq):
    fox_w, rw = FOX_WIDTH, RWKV_WIDTH
    fox_heads = fox_w // FOX_HEAD_DIM
    bf, row = _bf, _row

    w_in = p['w_in']
    o_f = 3 * fox_w
    o_r = o_f + fox_heads
    o_w = o_r + 3 * rw
    o_a = o_w + DECAY_LORA
    o_g = o_a + ICLR_LORA
    wt = w_in.T
    wt_small = jnp.concatenate([
        _pad_rows(wt[o_w:o_a], LANES), _pad_rows(wt[o_a:o_g], LANES),
        wt[o_g:o_g + GATE_LORA], _pad_rows(wt[o_f:o_r], LANES)], axis=0)
    mu = p['rwkv_mu']
    mu_small = jnp.concatenate([
        jnp.pad(mu[3 * rw:3 * rw + DECAY_LORA], (0, LANES - DECAY_LORA)),
        jnp.pad(mu[3 * rw + DECAY_LORA:3 * rw + DECAY_LORA + ICLR_LORA], (0, LANES - ICLR_LORA)),
        mu[3 * rw + DECAY_LORA + ICLR_LORA:], jnp.zeros((LANES,), F32)])

    q_scale = jnp.concatenate([jnp.full((1, fox_w), FOX_Q_SCALE, F32),
                               jnp.ones((1, 2 * fox_w), F32)], axis=1)
    qkv = matmul_nt(h, wt, BF16, 0, o_f, col_scale=q_scale)
    rkv = matmul_nt(h, wt, F32, o_r, 3 * rw)
    small = matmul_nt(h, wt_small, F32, 0, wt_small.shape[0], tn=wt_small.shape[0])

    f_bias = jnp.pad(p['fox_f_bias'], (0, LANES - fox_heads)).reshape(1, LANES)
    c = fgate_cumsum(small, f_bias, f_block=(2 * LANES + GATE_LORA) // LANES, seq=seq)
    c = c[:, :fox_heads].reshape(batch, seq, fox_heads).transpose(0, 2, 1)
    y_fox = fox_attention(qkv, c[..., None], c[:, :, None, :], batch, seq, fox_heads)

    r, lw, k, v, kk, beta, g = rwkv_prep(
        rkv, small, row(mu[:rw]), row(mu[rw:2 * rw]), row(mu[2 * rw:3 * rw]), row(mu_small),
        row(p['rwkv_w0']), _pad_rows(p['rwkv_w_up'], LANES), row(p['rwkv_a0']),
        _pad_rows(p['rwkv_a_up'], LANES), p['rwkv_g_up'], row(p['rwkv_k_k']),
        row(p['rwkv_k_a']), seq=seq)
    y_rwkv = rwkv_recurrence(r, lw, k, v, kk, beta, g, row(p['rwkv_ln_w']),
                             row(p['rwkv_ln_b']), row(p['rwkv_r_k']), batch, seq)

    return matmul2(y_fox, y_rwkv, p['w_out'], F32, tn=F32_W_TN)


def _xattn(h, mn, p, batch, seq, n_mem):
    q = matmul(h, p['xattn_wq'], BF16, tn=F32_W_TN)
    km = matmul(mn, p['xattn_wk'], BF16, tn=F32_W_TN)
    vm = matmul(mn, p['xattn_wv'], BF16, tn=F32_W_TN)
    o = cross_attention(q, km, vm, batch, seq, n_mem, XATTN_HEADS)
    return matmul(o, p['xattn_wo'], F32, tn=F32_W_TN)


def _layer(x, mem, p, batch, seq, n_mem):
    h = rmsnorm(x, p['ffn1_pre_g'])
    y = ffn(h, p['ffn1_w_gate'], p['ffn1_w_up'], p['ffn1_w_down'])
    x, h = resnorm(x, y, p['ffn1_post_g'], 0.5, p['mix_pre_g'])

    mixed = _mixer(h, p, batch, seq)
    x, h = resnorm(x, mixed, p['mix_post_g'], 1.0, p['xattn_pre_g'])

    mn = rmsnorm(mem, p['mem_norm_g'])
    xa = _xattn(h, mn, p, batch, seq, n_mem)
    x, h = resnorm(x, xa, p['xattn_post_g'], 1.0, p['ffn2_pre_g'])

    y = ffn(h, p['ffn2_w_gate'], p['ffn2_w_up'], p['ffn2_w_down'])
    return resnorm(x, y, p['ffn2_post_g'], 0.5)


_PARAM_NAMES = (
    'ffn1_pre_g', 'ffn1_w_gate', 'ffn1_w_up', 'ffn1_w_down', 'ffn1_post_g',
    'mix_pre_g', 'w_in', 'fox_f_bias', 'rwkv_mu', 'rwkv_w0', 'rwkv_w_up', 'rwkv_a0',
    'rwkv_a_up', 'rwkv_g_up', 'rwkv_k_k', 'rwkv_k_a', 'rwkv_r_k', 'rwkv_ln_w', 'rwkv_ln_b',
    'w_out', 'mix_post_g', 'xattn_pre_g', 'mem_norm_g', 'xattn_wq', 'xattn_wk', 'xattn_wv',
    'xattn_wo', 'xattn_post_g', 'ffn2_pre_g', 'ffn2_w_gate', 'ffn2_w_up', 'ffn2_w_down',
    'ffn2_post_g')


def kernel(x, mem, ffn1_pre_g, ffn1_w_gate, ffn1_w_up, ffn1_w_down, ffn1_post_g, mix_pre_g, w_in, fox_f_bias, rwkv_mu, rwkv_w0, rwkv_w_up, rwkv_a0, rwkv_a_up, rwkv_g_up, rwkv_k_k, rwkv_k_a, rwkv_r_k, rwkv_ln_w, rwkv_ln_b, w_out, mix_post_g, xattn_pre_g, mem_norm_g, xattn_wq, xattn_wk, xattn_wv, xattn_wo, xattn_post_g, ffn2_pre_g, ffn2_w_gate, ffn2_w_up, ffn2_w_down, ffn2_post_g):
    weights = (ffn1_pre_g, ffn1_w_gate, ffn1_w_up, ffn1_w_down, ffn1_post_g, mix_pre_g, w_in,
               fox_f_bias, rwkv_mu, rwkv_w0, rwkv_w_up, rwkv_a0, rwkv_a_up, rwkv_g_up, rwkv_k_k,
               rwkv_k_a, rwkv_r_k, rwkv_ln_w, rwkv_ln_b, w_out, mix_post_g, xattn_pre_g,
               mem_norm_g, xattn_wq, xattn_wk, xattn_wv, xattn_wo, xattn_post_g, ffn2_pre_g,
               ffn2_w_gate, ffn2_w_up, ffn2_w_down, ffn2_post_g)
    batch, seq, d = x.shape
    n_mem = mem.shape[1]
    depth = ffn1_pre_g.shape[0]
    xf = x.reshape(batch * seq, d)
    mf = mem.reshape(batch * n_mem, d)
    for l in range(depth):
        p = {name: w[l] for name, w in zip(_PARAM_NAMES, weights)}
        xf = _layer(xf, mf, p, batch, seq, n_mem)
    return xf.reshape(batch, seq, d)
```

```python
import functools

import jax
import jax.numpy as jnp
from jax import lax
from jax.experimental import pallas as pl
from jax.experimental.pallas import tpu as pltpu

F32 = jnp.float32
BF16 = jnp.bfloat16

SEQ = 2048
FOX_HEAD_DIM = 128
RWKV_HEAD_DIM = 64
RWKV_WIDTH = 2048
FOX_WIDTH = 2048
DECAY_LORA = 96
ICLR_LORA = 96
GATE_LORA = 256
XATTN_HEADS = 4
RMS_EPS = 1e-6
RWKV_GN_EPS = 64e-5
NEG_INF = -1e30

EXP_NEG_HALF = 0.6065306597126334
LANES = 128
CHUNK = 64
VMEM_LIMIT = 56 * 1024 * 1024
FFN_VMEM_LIMIT = 62 * 1024 * 1024
F32_W_TN = 512


def _cparams(sem, vmem=VMEM_LIMIT):
    return pltpu.CompilerParams(dimension_semantics=sem, vmem_limit_bytes=vmem)


NN = (((1,), (0,)), ((), ()))
NT = (((1,), (1,)), ((), ()))
TN = (((0,), (0,)), ((), ()))


def _split2(x):
    hi = x.astype(BF16)
    lo = (x - hi.astype(F32)).astype(BF16)
    return hi, lo


def _split3(x):
    hi = x.astype(BF16)
    r = x - hi.astype(F32)
    mid = r.astype(BF16)
    lo = (r - mid.astype(F32)).astype(BF16)
    return hi, mid, lo


def _dg(a, b, dims):
    return lax.dot_general(a, b, dims, preferred_element_type=F32)


def _dot3(a, b, dims=NN):
    ah, al = _split2(a)
    bh, bl = _split2(b)
    return _dg(ah, bh, dims) + (_dg(ah, bl, dims) + _dg(al, bh, dims))


def _dot_exact_lhs(a_bf16, b, dims=NN):
    b0, b1, b2 = _split3(b)
    return _dg(a_bf16, b0, dims) + (_dg(a_bf16, b1, dims) + _dg(a_bf16, b2, dims))


def _dot_exact_rhs(a, b_bf16, dims=NN):
    a0, a1, a2 = _split3(a)
    return _dg(a0, b_bf16, dims) + (_dg(a1, b_bf16, dims) + _dg(a2, b_bf16, dims))


def _head_sum(a, ones_bf16):
    a0, a1 = _split2(a)
    return _dg(a0, ones_bf16, NN) + _dg(a1, ones_bf16, NN)


def _softplus(x):
    return jnp.maximum(x, 0.0) + jnp.log1p(jnp.exp(-jnp.abs(x)))


def _rmsnorm_kernel(x_ref, g_ref, o_ref):
    x = x_ref[...]
    ms = jnp.mean(x * x, axis=-1, keepdims=True)
    o_ref[...] = (x * lax.rsqrt(ms + RMS_EPS) * g_ref[...]).astype(o_ref.dtype)


def rmsnorm(x, g, tm=256):
    m, d = x.shape
    tm = min(tm, m)
    return pl.pallas_call(
        _rmsnorm_kernel,
        grid=(m // tm,),
        in_specs=[pl.BlockSpec((tm, d), lambda i: (i, 0)),
                  pl.BlockSpec((1, d), lambda i: (0, 0))],
        out_specs=pl.BlockSpec((tm, d), lambda i: (i, 0)),
        out_shape=jax.ShapeDtypeStruct((m, d), BF16),
        compiler_params=_cparams(("parallel",)),
        name="rmsnorm",
    )(x, g.reshape(1, d))


def _resnorm_kernel(x_ref, y_ref, gp_ref, gn_ref, xo_ref, h_ref, *, scale):
    y = y_ref[...]
    ms = jnp.mean(y * y, axis=-1, keepdims=True)
    xn = x_ref[...] + scale * (y * lax.rsqrt(ms + RMS_EPS) * gp_ref[...])
    xo_ref[...] = xn
    ms2 = jnp.mean(xn * xn, axis=-1, keepdims=True)
    h_ref[...] = (xn * lax.rsqrt(ms2 + RMS_EPS) * gn_ref[...]).astype(h_ref.dtype)


def _resnorm_last_kernel(x_ref, y_ref, gp_ref, xo_ref, *, scale):
    y = y_ref[...]
    ms = jnp.mean(y * y, axis=-1, keepdims=True)
    xo_ref[...] = x_ref[...] + scale * (y * lax.rsqrt(ms + RMS_EPS) * gp_ref[...])


def resnorm(x, y, g_post, scale, g_next=None, tm=256):
    m, d = x.shape
    row = pl.BlockSpec((tm, d), lambda i: (i, 0))
    vec = pl.BlockSpec((1, d), lambda i: (0, 0))
    if g_next is None:
        return pl.pallas_call(
            functools.partial(_resnorm_last_kernel, scale=scale),
            grid=(m // tm,),
            in_specs=[row, row, vec],
            out_specs=row,
            out_shape=jax.ShapeDtypeStruct((m, d), F32),
            compiler_params=_cparams(("parallel",)),
            name="resnorm_last",
        )(x, y, g_post.reshape(1, d))
    return pl.pallas_call(
        functools.partial(_resnorm_kernel, scale=scale),
        grid=(m // tm,),
        in_specs=[row, row, vec, vec],
        out_specs=[row, row],
        out_shape=[jax.ShapeDtypeStruct((m, d), F32),
                   jax.ShapeDtypeStruct((m, d), BF16)],
        compiler_params=_cparams(("parallel",)),
        name="resnorm",
    )(x, y, g_post.reshape(1, d), g_next.reshape(1, d))


def _mm_kernel(a_ref, w_ref, o_ref):
    o_ref[...] = jnp.dot(a_ref[...], w_ref[...].astype(BF16),
                         preferred_element_type=F32).astype(o_ref.dtype)


def _mm_scaled_kernel(a_ref, w_ref, s_ref, o_ref):
    acc = jnp.dot(a_ref[...], w_ref[...].astype(BF16), preferred_element_type=F32)
    o_ref[...] = (acc * s_ref[...]).astype(o_ref.dtype)


def _mm2_kernel(a1_ref, a2_ref, w1_ref, w2_ref, o_ref):
    acc = jnp.dot(a1_ref[...], w1_ref[...].astype(BF16), preferred_element_type=F32)
    acc += jnp.dot(a2_ref[...], w2_ref[...].astype(BF16), preferred_element_type=F32)
    o_ref[...] = acc.astype(o_ref.dtype)


def matmul(a, w, out_dtype, tm=1024, tn=1024, col_scale=None, n=None):
    m, k = a.shape
    n = w.shape[1] if n is None else n
    tm = min(tm, m)
    tn = min(tn, n)
    assert m % tm == 0 and n % tn == 0
    in_specs = [pl.BlockSpec((tm, k), lambda i, j: (i, 0)),
                pl.BlockSpec((k, tn), lambda i, j: (0, j))]
    args = (a, w)
    kern = _mm_kernel
    if col_scale is not None:
        in_specs.append(pl.BlockSpec((1, tn), lambda i, j: (0, j)))
        args = (a, w, col_scale)
        kern = _mm_scaled_kernel
    return pl.pallas_call(
        kern,
        grid=(m // tm, n // tn),
        in_specs=in_specs,
        out_specs=pl.BlockSpec((tm, tn), lambda i, j: (i, j)),
        out_shape=jax.ShapeDtypeStruct((m, n), out_dtype),
        compiler_params=_cparams(("parallel", "arbitrary")),
        name="matmul",
    )(*args)


def _mm_nt_kernel(a_ref, wt_ref, o_ref):
    o_ref[...] = _dg(a_ref[...], wt_ref[...].astype(BF16), NT).astype(o_ref.dtype)


def _mm_nt_scaled_kernel(a_ref, wt_ref, s_ref, o_ref):
    acc = _dg(a_ref[...], wt_ref[...].astype(BF16), NT)
    o_ref[...] = (acc * s_ref[...]).astype(o_ref.dtype)


def matmul_nt(a, wt, out_dtype, row0, n, tm=1024, tn=512, col_scale=None):
    m, k = a.shape
    tm = min(tm, m)
    tn = min(tn, n)
    assert m % tm == 0 and n % tn == 0 and row0 % 8 == 0 and wt.shape[1] == k
    in_specs = [pl.BlockSpec((tm, k), lambda i, j: (i, 0)),
                pl.BlockSpec((pl.Element(tn), pl.Element(k)),
                             lambda i, j: (pl.multiple_of(row0 + j * tn, 8), 0))]
    args = (a, wt)
    kern = _mm_nt_kernel
    if col_scale is not None:
        in_specs.append(pl.BlockSpec((1, tn), lambda i, j: (0, j)))
        args = (a, wt, col_scale)
        kern = _mm_nt_scaled_kernel
    return pl.pallas_call(
        kern,
        grid=(m // tm, n // tn),
        in_specs=in_specs,
        out_specs=pl.BlockSpec((tm, tn), lambda i, j: (i, j)),
        out_shape=jax.ShapeDtypeStruct((m, n), out_dtype),
        compiler_params=_cparams(("parallel", "arbitrary")),
        name="matmul_nt",
    )(*args)


def matmul2(a1, a2, w, out_dtype, tm=1024, tn=1024):
    m, k1 = a1.shape
    k2 = a2.shape[1]
    n = w.shape[1]
    assert k1 == k2 and w.shape[0] == k1 + k2
    tm = min(tm, m)
    tn = min(tn, n)
    assert m % tm == 0 and n % tn == 0
    return pl.pallas_call(
        _mm2_kernel,
        grid=(m // tm, n // tn),
        in_specs=[pl.BlockSpec((tm, k1), lambda i, j: (i, 0)),
                  pl.BlockSpec((tm, k2), lambda i, j: (i, 0)),
                  pl.BlockSpec((k1, tn), lambda i, j: (0, j)),
                  pl.BlockSpec((k2, tn), lambda i, j: (1, j))],
        out_specs=pl.BlockSpec((tm, tn), lambda i, j: (i, j)),
        out_shape=jax.ShapeDtypeStruct((m, n), out_dtype),
        compiler_params=_cparams(("parallel", "arbitrary")),
        name="matmul2",
    )(a1, a2, w, w)


def _ffn_kernel(h_ref, wg_ref, wu_ref, wd_ref, o_ref, *, n_chunk):
    f = pl.program_id(1)

    @pl.when(f == 0)
    def _():
        o_ref[...] = jnp.zeros_like(o_ref)

    h = h_ref[...]
    g = jnp.dot(h, wg_ref[...].astype(BF16), preferred_element_type=F32)
    u = jnp.dot(h, wu_ref[...].astype(BF16), preferred_element_type=F32)
    act = (g * jax.nn.sigmoid(g) * u).astype(BF16)
    d = o_ref.shape[1]
    for c in range(0, d, n_chunk):
        o_ref[:, c:c + n_chunk] += jnp.dot(act, wd_ref[:, c:c + n_chunk].astype(BF16),
                                           preferred_element_type=F32)


def ffn(h, wg, wu, wd, tm=1024, tf=256, n_chunk=512):
    m, d = h.shape
    dff = wg.shape[1]
    tm = min(tm, m)
    tf = min(tf, dff)
    n_chunk = min(n_chunk, d)
    assert m % tm == 0 and dff % tf == 0 and d % n_chunk == 0
    return pl.pallas_call(
        functools.partial(_ffn_kernel, n_chunk=n_chunk),
        grid=(m // tm, dff // tf),
        in_specs=[pl.BlockSpec((tm, d), lambda i, f: (i, 0), pipeline_mode=pl.Buffered(1)),
                  pl.BlockSpec((d, tf), lambda i, f: (0, f)),
                  pl.BlockSpec((d, tf), lambda i, f: (0, f)),
                  pl.BlockSpec((tf, d), lambda i, f: (f, 0))],
        out_specs=pl.BlockSpec((tm, d), lambda i, f: (i, 0), pipeline_mode=pl.Buffered(1)),
        out_shape=jax.ShapeDtypeStruct((m, d), F32),
        compiler_params=_cparams(("parallel", "arbitrary"), vmem=FFN_VMEM_LIMIT),
        name="ffn",
    )(h, wg, wu, wd)


def _fgate_kernel(f_ref, b_ref, c_ref, *, blk):
    s = f_ref.shape[0]
    row = lax.broadcasted_iota(jnp.int32, (blk, blk), 0)
    col = lax.broadcasted_iota(jnp.int32, (blk, blk), 1)
    tri = jnp.where(row >= col, 1.0, 0.0).astype(BF16)
    carry = jnp.zeros((1, f_ref.shape[1]), F32)
    for r0 in range(0, s, blk):
        x = f_ref[r0:r0 + blk, :] + b_ref[...]
        logf = -_softplus(-x)
        c = _dot_exact_lhs(tri, logf) + carry
        c_ref[r0:r0 + blk, :] = c
        carry = c[blk - 1:blk, :]


def fgate_cumsum(small, f_bias_row, f_block, seq, blk=256):
    m = small.shape[0]
    blk = min(blk, seq)
    return pl.pallas_call(
        functools.partial(_fgate_kernel, blk=blk),
        grid=(m // seq,),
        in_specs=[pl.BlockSpec((seq, LANES), lambda b: (b, f_block)),
                  pl.BlockSpec((1, LANES), lambda b: (0, 0))],
        out_specs=pl.BlockSpec((seq, LANES), lambda b: (b, 0)),
        out_shape=jax.ShapeDtypeStruct((m, LANES), F32),
        compiler_params=_cparams(("parallel",)),
        name="fgate_cumsum",
    )(small, f_bias_row)


LOG2E = 1.4426950408889634
FOX_Q_SCALE = FOX_HEAD_DIM ** -0.5 * LOG2E


def _fox_kernel(q_ref, k_ref, v_ref, cq_ref, ck_ref, o_ref, s_scr, p_scr, m_scr, cq_scr,
                acc_scr, *, tq, tk, hg, strip):
    qi = pl.program_id(2)
    dh = FOX_HEAD_DIM
    hs = range(hg)
    nl = tk // dh
    m_scr[...] = jnp.full_like(m_scr, NEG_INF)
    acc_scr[...] = jnp.zeros_like(acc_scr)
    for g in hs:
        cq_scr[g] = jnp.broadcast_to(cq_ref[g] * LOG2E, (tq, dh))
    ones = jnp.ones((tk, dh), BF16)
    col = lax.broadcasted_iota(jnp.int32, (strip, tk), 1)
    row = lax.broadcasted_iota(jnp.int32, (strip, tk), 0)

    def head_cols(g):
        return slice(g * dh, (g + 1) * dh)

    def lane_tile(x, n):
        return x if n == 1 else jnp.concatenate([x] * n, axis=1)

    def block(ks, r0, mask_off):
        for g in hs:
            s_scr[g, r0:, :] = _dg(q_ref[r0:, head_cols(g)],
                                   k_ref[pl.ds(ks, tk), head_cols(g)], NT)
        for g in hs:
            ckg = ck_ref[g, :, pl.ds(ks, tk)] * LOG2E
            for r in range(r0, tq, strip):
                rows = slice(r, r + strip)
                s = s_scr[g, rows, :] - ckg
                if mask_off is not None:
                    s = jnp.where(row + (r - mask_off) >= col, s, NEG_INF)
                cqr = cq_scr[g, rows, :]
                m_prev = m_scr[g, rows, :]
                m_new = jnp.maximum(m_prev, jnp.max(s, axis=-1, keepdims=True) + cqr)
                p_scr[g, rows, :] = jnp.exp2(s + lane_tile(cqr - m_new, nl)).astype(BF16)
                m_scr[g, rows, :] = m_new
                acc_scr[g, rows, :] = lane_tile(jnp.exp2(m_prev - m_new), 2) * acc_scr[g, rows, :]
        for g in hs:
            v_aug = jnp.concatenate([v_ref[pl.ds(ks, tk), head_cols(g)], ones], axis=1)
            acc_scr[g, r0:, :] += _dg(p_scr[g, r0:, :], v_aug, NN)

    def body(ki, carry):
        block(pl.multiple_of(ki * tk, tk), 0, None)
        return carry

    n_sub = tq // tk
    lax.fori_loop(0, qi * n_sub, body, 0)
    for j in range(n_sub):
        block(pl.multiple_of(qi * tq + j * tk, tk), j * tk, j * tk)
    for g in hs:
        acc = acc_scr[g]
        o_ref[:, head_cols(g)] = (acc[:, :dh] / acc[:, dh:]).astype(o_ref.dtype)


def fox_attention(qkv, c_col, c_row, batch, seq, heads, tq=512, tk=256, hg=2, strip=64):
    tq = min(tq, seq)
    tk = min(tk, tq)
    strip = min(strip, tk)
    nq = seq // tq
    dh = FOX_HEAD_DIM
    ng = heads // hg
    assert heads % hg == 0 and tq % tk == 0 and tk % strip == 0 and seq % tq == 0
    return pl.pallas_call(
        functools.partial(_fox_kernel, tq=tq, tk=tk, hg=hg, strip=strip),
        grid=(batch, ng, nq),
        in_specs=[
            pl.BlockSpec((tq, hg * dh), lambda b, h, qi: (b * nq + qi, h)),
            pl.BlockSpec((seq, hg * dh), lambda b, h, qi: (b, ng + h)),
            pl.BlockSpec((seq, hg * dh), lambda b, h, qi: (b, 2 * ng + h)),
            pl.BlockSpec((None, hg, tq, 1), lambda b, h, qi: (b, h, qi, 0)),
            pl.BlockSpec((None, hg, 1, seq), lambda b, h, qi: (b, h, 0, 0)),
        ],
        out_specs=pl.BlockSpec((tq, hg * dh), lambda b, h, qi: (b * nq + qi, h)),
        out_shape=jax.ShapeDtypeStruct((batch * seq, heads * dh), BF16),
        scratch_shapes=[pltpu.VMEM((hg, tq, tk), F32), pltpu.VMEM((hg, tq, tk), BF16),
                        pltpu.VMEM((hg, tq, dh), F32), pltpu.VMEM((hg, tq, dh), F32),
                        pltpu.VMEM((hg, tq, 2 * dh), F32)],
        compiler_params=_cparams(("parallel", "parallel", "arbitrary")),
        name="fox_attention",
    )(qkv, qkv, qkv, c_col, c_row)


def _rwkv_prep_kernel(r_ref, k_ref, v_ref, rp_ref, kp_ref, vp_ref, sm_ref, smp_ref,
                      mu_r_ref, mu_k_ref, mu_v_ref, mu_s_ref,
                      w0_ref, wup_ref, a0_ref, aup_ref, gup_ref, kk_ref, ka_ref,
                      r_o, lw_o, k_o, v_o, kk_o, b_o, g_o, *, tiles_per_seq):
    i = pl.program_id(0)
    first = (i % tiles_per_seq) == 0
    tm = r_ref.shape[0]
    row0 = lax.broadcasted_iota(jnp.int32, (tm, 1), 0) == 0

    def mix(cur_ref, prev_ref, mu_ref):
        cur = cur_ref[...]
        last = jnp.where(first, 0.0, prev_ref[7:8, :])
        prev = jnp.where(row0, last, pltpu.roll(cur, 1, axis=0))
        return cur + mu_ref[...] * (prev - cur)

    zr = mix(r_ref, rp_ref, mu_r_ref)
    zk = mix(k_ref, kp_ref, mu_k_ref)
    zv = mix(v_ref, vp_ref, mu_v_ref)
    zs = mix(sm_ref, smp_ref, mu_s_ref)
    zw = zs[:, 0:LANES]
    za = zs[:, LANES:2 * LANES]
    zg = zs[:, 2 * LANES:2 * LANES + GATE_LORA]

    def lora(z, w_ref):
        return _dg(z.astype(BF16), w_ref[...].astype(BF16), NN)

    lw_o[...] = -EXP_NEG_HALF * jax.nn.sigmoid(w0_ref[...] + lora(jnp.tanh(zw), wup_ref))
    a = jax.nn.sigmoid(a0_ref[...] + lora(za, aup_ref))
    g_o[...] = lora(jax.nn.sigmoid(zg), gup_ref)

    tc = zk.shape[1]
    hr = lax.broadcasted_iota(jnp.int32, (LANES, LANES), 0) // RWKV_HEAD_DIM
    hc = lax.broadcasted_iota(jnp.int32, (LANES, LANES), 1) // RWKV_HEAD_DIM
    head_ones = jnp.where(hr == hc, 1.0, 0.0).astype(BF16)
    kk = zk * kk_ref[...]
    sq = kk * kk
    ss = jnp.concatenate([_head_sum(sq[:, j:j + LANES], head_ones)
                          for j in range(0, tc, LANES)], axis=1)
    kk = kk * lax.rsqrt(jnp.maximum(ss, 1e-24))
    r_o[...] = zr
    v_o[...] = zv
    kk_o[...] = kk
    b_o[...] = kk * a
    k_o[...] = zk * (1.0 + (a - 1.0) * ka_ref[...])


def rwkv_prep(rkv, small, mu_r, mu_k, mu_v, mu_s, w0, w_up, a0, a_up, g_up, k_k, k_a,
              seq, tm=512, tc=512):
    m = rkv.shape[0]
    width = RWKV_WIDTH
    tm = min(tm, seq)
    nc = width // tc
    ns = small.shape[1]
    tps = seq // tm
    sub = tm // 8

    def cur(piece):
        return pl.BlockSpec((tm, tc), lambda i, j: (i, piece * nc + j))

    def prev(piece):
        return pl.BlockSpec((8, tc), lambda i, j: (jnp.maximum(i * sub - 1, 0), piece * nc + j))

    colv = pl.BlockSpec((1, tc), lambda i, j: (0, j))
    out = pl.BlockSpec((tm, tc), lambda i, j: (i, j))
    outs = [jax.ShapeDtypeStruct((m, width), F32)] * 7
    return pl.pallas_call(
        functools.partial(_rwkv_prep_kernel, tiles_per_seq=tps),
        grid=(m // tm, nc),
        in_specs=[cur(0), cur(1), cur(2), prev(0), prev(1), prev(2),
                  pl.BlockSpec((tm, ns), lambda i, j: (i, 0)),
                  pl.BlockSpec((8, ns), lambda i, j: (jnp.maximum(i * sub - 1, 0), 0)),
                  colv, colv, colv,
                  pl.BlockSpec((1, ns), lambda i, j: (0, 0)),
                  colv,
                  pl.BlockSpec((LANES, tc), lambda i, j: (0, j)),
                  colv,
                  pl.BlockSpec((LANES, tc), lambda i, j: (0, j)),
                  pl.BlockSpec((GATE_LORA, tc), lambda i, j: (0, j)),
                  colv, colv],
        out_specs=[out] * 7,
        out_shape=outs,
        compiler_params=_cparams(("parallel", "arbitrary")),
        name="rwkv_prep",
    )(rkv, rkv, rkv, rkv, rkv, rkv, small, small,
      mu_r, mu_k, mu_v, mu_s, w0, w_up, a0, a_up, g_up, k_k, k_a)


def _rwkv_kernel(r_ref, lw_ref, k_ref, v_ref, kk_ref, b_ref, g_ref,
                 lnw_ref, lnb_ref, rk_ref, o_ref, s_ref, *, pairs):
    ci = pl.program_id(2)
    c = CHUNK
    hd = RWKV_HEAD_DIM

    @pl.when(ci == 0)
    def _():
        s_ref[...] = jnp.zeros_like(s_ref)

    r2 = lax.broadcasted_iota(jnp.int32, (LANES, LANES), 0)
    c2 = lax.broadcasted_iota(jnp.int32, (LANES, LANES), 1)
    same = (r2 < c) == (c2 < c)
    strict = jnp.logical_and(same, r2 > c2)
    incl = jnp.logical_and(same, r2 >= c2)
    head_ones = jnp.where((r2 < hd) == (c2 < hd), 1.0, 0.0).astype(BF16)
    rc = lax.broadcasted_iota(jnp.int32, (c, c), 0)
    cc = lax.broadcasted_iota(jnp.int32, (c, c), 1)
    tri_incl = jnp.where(rc >= cc, 1.0, 0.0).astype(BF16)
    m_a = lax.broadcasted_iota(jnp.int32, (1, LANES), 1) < hd
    inv_n = 1.0 / hd
    prs = range(pairs)
    cat = lambda xs: jnp.concatenate(xs, axis=0)

    def lanes(x, p):
        return x[:, p * LANES:(p + 1) * LANES]

    def stack2(x):
        return cat([jnp.where(m_a, x, 0.0), jnp.where(m_a, 0.0, x)])

    lw = lw_ref[...]
    r, k, v = r_ref[...], k_ref[...], v_ref[...]
    kk, beta = kk_ref[...], b_ref[...]
    big_l = _dot_exact_lhs(tri_incl, lw)
    l_tot = big_l[c - 1:c, :]
    e_nl = jnp.exp(-big_l)
    e_rem = jnp.exp(l_tot - big_l)
    e_tot = jnp.exp(l_tot)
    at = -kk * jnp.exp(big_l - lw)
    rt = r * jnp.exp(big_l)
    kt = k * e_nl
    bt = beta * e_nl
    bend = beta * e_rem
    kend = k * e_rem

    s0 = [s_ref[p] for p in prs]
    lhs = [cat([stack2(lanes(at, p)), stack2(lanes(rt, p))]).astype(BF16) for p in prs]
    wbk = [cat([lanes(bt, p), lanes(kt, p)]).astype(BF16) for p in prs]
    v2 = [stack2(lanes(v, p)).astype(BF16) for p in prs]
    m = [_dg(lhs[p], wbk[p], NT) for p in prs]
    sx = [_dg(lhs[p], s0[p].astype(BF16), NT) for p in prs]

    a_ab, a_kr, a_rb = [], [], []
    for p in prs:
        mp = m[p]
        sw = pltpu.roll(mp, c, axis=1)
        a_ab.append(jnp.where(strict, cat([mp[0:c], sw[c:2 * c]]), 0.0))
        a_ak = jnp.where(strict, cat([sw[0:c], mp[c:2 * c]]), 0.0)
        a_rb.append(jnp.where(incl, cat([mp[2 * c:3 * c], sw[3 * c:4 * c]]), 0.0))
        a_rk = jnp.where(incl, cat([sw[2 * c:3 * c], mp[3 * c:4 * c]]), 0.0)
        a_kr.append(cat([a_ak, a_rk]).astype(BF16))
    xy = [_dg(a_kr[p], v2[p], NN) for p in prs]
    u = [sx[p][:2 * c] + xy[p][:2 * c] for p in prs]
    ypart = [sx[p][2 * c:] + xy[p][2 * c:] for p in prs]

    def d1(a, b):
        return _dg(a.astype(BF16), b.astype(BF16), NN)

    blk = lambda n: (r2 // n) == (c2 // n)
    eye = jnp.where(r2 == c2, 1.0, 0.0)
    base = 8
    x_in = u
    p1 = [jnp.where(blk(base), a_ab[p], 0.0) for p in prs]
    p2 = [d1(p1[p], p1[p]) for p in prs]
    ip = [eye + p1[p] for p in prs]
    p4 = [d1(p2[p], p2[p]) for p in prs]
    m1 = [ip[p] + d1(ip[p], p2[p]) for p in prs]
    t = [m1[p] + d1(m1[p], p4[p]) for p in prs]
    n = base
    while n < c:
        quad = jnp.logical_and(blk(2 * n), jnp.logical_not(blk(n)))
        et = [d1(jnp.where(quad, a_ab[p], 0.0), t[p]) for p in prs]
        t = [t[p] + d1(t[p], et[p]) for p in prs]
        n *= 2
    u1 = [d1(t[p], x_in[p]) for p in prs]
    res = [x_in[p] - u1[p] + _dot3(a_ab[p], u1[p]) for p in prs]
    u = [u1[p] + d1(t[p], res[p]) for p in prs]

    ub = [u[p].astype(BF16) for p in prs]
    y2 = [ypart[p] + _dg(a_rb[p].astype(BF16), ub[p], NN) for p in prs]
    for p in prs:
        ends = cat([stack2(lanes(bend, p)), stack2(lanes(kend, p))]).astype(BF16)
        s_ref[p] = s0[p] * lanes(e_tot, p) + _dg(cat([ub[p], v2[p]]), ends, TN)

    y_all = cat([y2[p][:c] + y2[p][c:] for p in prs])
    mean = _head_sum(y_all, head_ones) * inv_n
    yc = y_all - mean
    var = _head_sum(yc * yc, head_ones) * inv_n
    yn = yc * lax.rsqrt(var + RWKV_GN_EPS)
    rkk = r * k * rk_ref[...]
    bonus = _head_sum(cat([lanes(rkk, p) for p in prs]), head_ones)
    lnw, lnb, g = lnw_ref[...], lnb_ref[...], g_ref[...]
    for p in prs:
        rows = slice(p * c, (p + 1) * c)
        out = (yn[rows] * lanes(lnw, p) + lanes(lnb, p) + bonus[rows] * lanes(v, p)) * lanes(g, p)
        o_ref[:, p * LANES:(p + 1) * LANES] = out.astype(o_ref.dtype)


def rwkv_recurrence(r, lw, k, v, kk, beta, g, ln_w, ln_b, r_k, batch, seq, pairs=16):
    m, width = r.shape
    c = CHUNK
    assert seq % c == 0 and 2 * c == LANES and 2 * RWKV_HEAD_DIM == LANES
    tc = pairs * LANES
    nchunk = seq // c
    blk = pl.BlockSpec((c, tc), lambda b, j, ci: (b * nchunk + ci, j))
    vec = pl.BlockSpec((1, tc), lambda b, j, ci: (0, j))
    return pl.pallas_call(
        functools.partial(_rwkv_kernel, pairs=pairs),
        grid=(batch, width // tc, nchunk),
        in_specs=[blk] * 7 + [vec] * 3,
        out_specs=blk,
        out_shape=jax.ShapeDtypeStruct((m, width), BF16),
        scratch_shapes=[pltpu.VMEM((pairs, LANES, LANES), F32)],
        compiler_params=_cparams(("parallel", "parallel", "arbitrary")),
        name="rwkv_recurrence",
    )(r, lw, k, v, kk, beta, g, ln_w, ln_b, r_k)


def _xattn_kernel(q_ref, k_ref, v_ref, o_ref, *, scale):
    s = _dg(q_ref[...], k_ref[...], NT) * scale
    m = jnp.max(s, axis=-1, keepdims=True)
    p = jnp.exp(s - m)
    l = jnp.sum(p, axis=-1, keepdims=True)
    o = jnp.dot(p.astype(BF16), v_ref[...], preferred_element_type=F32)
    o_ref[...] = (o / l).astype(o_ref.dtype)


def cross_attention(q, k, v, batch, seq, n_mem, heads, tq=512):
    m, d = q.shape
    dh = d // heads
    tq = min(tq, seq)
    nq = seq // tq
    return pl.pallas_call(
        functools.partial(_xattn_kernel, scale=dh ** -0.5),
        grid=(batch, heads, nq),
        in_specs=[pl.BlockSpec((tq, dh), lambda b, h, qi: (b * nq + qi, h)),
                  pl.BlockSpec((n_mem, dh), lambda b, h, qi: (b, h)),
                  pl.BlockSpec((n_mem, dh), lambda b, h, qi: (b, h))],
        out_specs=pl.BlockSpec((tq, dh), lambda b, h, qi: (b * nq + qi, h)),
        out_shape=jax.ShapeDtypeStruct((m, d), BF16),
        compiler_params=_cparams(("parallel", "parallel", "arbitrary")),
        name="cross_attention",
    )(q, k, v)


def _pad_cols(w, n):
    return jnp.pad(w, ((0, 0), (0, n - w.shape[1])))


def _pad_rows(w, n):
    return jnp.pad(w, ((0, n - w.shape[0]), (0, 0)))


def _bf(w):
    return w.astype(BF16)


def _row(t):
    return t.reshape(1, -1).astype(F32)


def _mixer(h, p, batch, seq):
    fox_w, rw = FOX_WIDTH, RWKV_WIDTH
    fox_heads = fox_w // FOX_HEAD_DIM
    bf, row = _bf, _row

    w_in = p['w_in']
    o_f = 3 * fox_w
    o_r = o_f + fox_heads
    o_w = o_r + 3 * rw
    o_a = o_w + DECAY_LORA
    o_g = o_a + ICLR_LORA
    wt = w_in.T
    wt_small = jnp.concatenate([
        _pad_rows(wt[o_w:o_a], LANES), _pad_rows(wt[o_a:o_g], LANES),
        wt[o_g:o_g + GATE_LORA], _pad_rows(wt[o_f:o_r], LANES)], axis=0)
    mu = p['rwkv_mu']
    mu_small = jnp.concatenate([
        jnp.pad(mu[3 * rw:3 * rw + DECAY_LORA], (0, LANES - DECAY_LORA)),
        jnp.pad(mu[3 * rw + DECAY_LORA:3 * rw + DECAY_LORA + ICLR_LORA], (0, LANES - ICLR_LORA)),
        mu[3 * rw + DECAY_LORA + ICLR_LORA:], jnp.zeros((LANES,), F32)])

    q_scale = jnp.concatenate([jnp.full((1, fox_w), FOX_Q_SCALE, F32),
                               jnp.ones((1, 2 * fox_w), F32)], axis=1)
    qkv = matmul_nt(h, wt, BF16, 0, o_f, col_scale=q_scale)
    rkv = matmul_nt(h, wt, F32, o_r, 3 * rw)
    small = matmul_nt(h, wt_small, F32, 0, wt_small.shape[0], tn=wt_small.shape[0])

    f_bias = jnp.pad(p['fox_f_bias'], (0, LANES - fox_heads)).reshape(1, LANES)
    c = fgate_cumsum(small, f_bias, f_block=(2 * LANES + GATE_LORA) // LANES, seq=seq)
    c = c[:, :fox_heads].reshape(batch, seq, fox_heads).transpose(0, 2, 1)
    y_fox = fox_attention(qkv, c[..., None], c[:, :, None, :], batch, seq, fox_heads)

    r, lw, k, v, kk, beta, g = rwkv_prep(
        rkv, small, row(mu[:rw]), row(mu[rw:2 * rw]), row(mu[2 * rw:3 * rw]), row(mu_small),
        row(p['rwkv_w0']), _pad_rows(p['rwkv_w_up'], LANES), row(p['rwkv_a0']),
        _pad_rows(p['rwkv_a_up'], LANES), p['rwkv_g_up'], row(p['rwkv_k_k']),
        row(p['rwkv_k_a']), seq=seq)
    y_rwkv = rwkv_recurrence(r, lw, k, v, kk, beta, g, row(p['rwkv_ln_w']),
                             row(p['rwkv_ln_b']), row(p['rwkv_r_k']), batch, seq)

    return matmul2(y_fox, y_rwkv, p['w_out'], F32, tn=F32_W_TN)


def _xattn(h, mn, p, batch, seq, n_mem):
    q = matmul(h, p['xattn_wq'], BF16, tn=F32_W_TN)
    km = matmul(mn, p['xattn_wk'], BF16, tn=F32_W_TN)
    vm = matmul(mn, p['xattn_wv'], BF16, tn=F32_W_TN)
    o = cross_attention(q, km, vm, batch, seq, n_mem, XATTN_HEADS)
    return matmul(o, p['xattn_wo'], F32, tn=F32_W_TN)


def _layer(x, mem, p, batch, seq, n_mem):
    h = rmsnorm(x, p['ffn1_pre_g'])
    y = ffn(h, p['ffn1_w_gate'], p['ffn1_w_up'], p['ffn1_w_down'])
    x, h = resnorm(x, y, p['ffn1_post_g'], 0.5, p['mix_pre_g'])

    mixed = _mixer(h, p, batch, seq)
    x, h = resnorm(x, mixed, p['mix_post_g'], 1.0, p['xattn_pre_g'])

    mn = rmsnorm(mem, p['mem_norm_g'])
    xa = _xattn(h, mn, p, batch, seq, n_mem)
    x, h = resnorm(x, xa, p['xattn_post_g'], 1.0, p['ffn2_pre_g'])

    y = ffn(h, p['ffn2_w_gate'], p['ffn2_w_up'], p['ffn2_w_down'])
    return resnorm(x, y, p['ffn2_post_g'], 0.5)


_PARAM_NAMES = (
    'ffn1_pre_g', 'ffn1_w_gate', 'ffn1_w_up', 'ffn1_w_down', 'ffn1_post_g',
    'mix_pre_g', 'w_in', 'fox_f_bias', 'rwkv_mu', 'rwkv_w0', 'rwkv_w_up', 'rwkv_a0',
    'rwkv_a_up', 'rwkv_g_up', 'rwkv_k_k', 'rwkv_k_a', 'rwkv_r_k', 'rwkv_ln_w', 'rwkv_ln_b',
    'w_out', 'mix_post_g', 'xattn_pre_g', 'mem_norm_g', 'xattn_wq', 'xattn_wk', 'xattn_wv',
    'xattn_wo', 'xattn_post_g', 'ffn2_pre_g', 'ffn2_w_gate', 'ffn2_w_up', 'ffn2_w_down',
    'ffn2_post_g')


def kernel(x, mem, ffn1_pre_g, ffn1_w_gate, ffn1_w_up, ffn1_w_down, ffn1_post_g, mix_pre_g, w_in, fox_f_bias, rwkv_mu, rwkv_w0, rwkv_w_up, rwkv_a0, rwkv_a_up, rwkv_g_up, rwkv_k_k, rwkv_k_a, rwkv_r_k, rwkv_ln_w, rwkv_ln_b, w_out, mix_post_g, xattn_pre_g, mem_norm_g, xattn_wq, xattn_wk, xattn_wv, xattn_wo, xattn_post_g, ffn2_pre_g, ffn2_w_gate, ffn2_w_up, ffn2_w_down, ffn2_post_g):
    weights = (ffn1_pre_g, ffn1_w_gate, ffn1_w_up, ffn1_w_down, ffn1_post_g, mix_pre_g, w_in,
               fox_f_bias, rwkv_mu, rwkv_w0, rwkv_w_up, rwkv_a0, rwkv_a_up, rwkv_g_up, rwkv_k_k,
               rwkv_k_a, rwkv_r_k, rwkv_ln_w, rwkv_ln_b, w_out, mix_post_g, xattn_pre_g,
               mem_norm_g, xattn_wq, xattn_wk, xattn_wv, xattn_wo, xattn_post_g, ffn2_pre_g,
               ffn2_w_gate, ffn2_w_up, ffn2_w_down, ffn2_post_g)
    batch, seq, d = x.shape
    n_mem = mem.shape[1]
    depth = ffn1_pre_g.shape[0]
    xf = x.reshape(batch * seq, d)
    mf = mem.reshape(batch * n_mem, d)
    for l in range(depth):
        p = {name: w[l] for name, w in zip(_PARAM_NAMES, weights)}
        xf = _layer(xf, mf, p, batch, seq, n_mem)
    return xf.reshape(batch, seq, d)
```

```python
import functools

import jax
import jax.numpy as jnp
from jax import lax
from jax.experimental import pallas as pl
from jax.experimental.pallas import tpu as pltpu

F32 = jnp.float32
BF16 = jnp.bfloat16

SEQ = 2048
FOX_HEAD_DIM = 128
RWKV_HEAD_DIM = 64
RWKV_WIDTH = 2048
FOX_WIDTH = 2048
DECAY_LORA = 96
ICLR_LORA = 96
GATE_LORA = 256
XATTN_HEADS = 4
RMS_EPS = 1e-6
RWKV_GN_EPS = 64e-5
NEG_INF = -1e30

EXP_NEG_HALF = 0.6065306597126334
LANES = 128
CHUNK = 64
VMEM_LIMIT = 56 * 1024 * 1024
FFN_VMEM_LIMIT = 62 * 1024 * 1024
F32_W_TN = 512


def _cparams(sem, vmem=VMEM_LIMIT):
    return pltpu.CompilerParams(dimension_semantics=sem, vmem_limit_bytes=vmem)


NN = (((1,), (0,)), ((), ()))
NT = (((1,), (1,)), ((), ()))
TN = (((0,), (0,)), ((), ()))


def _split2(x):
    hi = x.astype(BF16)
    lo = (x - hi.astype(F32)).astype(BF16)
    return hi, lo


def _split3(x):
    hi = x.astype(BF16)
    r = x - hi.astype(F32)
    mid = r.astype(BF16)
    lo = (r - mid.astype(F32)).astype(BF16)
    return hi, mid, lo


def _dg(a, b, dims):
    return lax.dot_general(a, b, dims, preferred_element_type=F32)


def _dot3(a, b, dims=NN):
    ah, al = _split2(a)
    bh, bl = _split2(b)
    return _dg(ah, bh, dims) + (_dg(ah, bl, dims) + _dg(al, bh, dims))


def _dot_exact_lhs(a_bf16, b, dims=NN):
    b0, b1, b2 = _split3(b)
    return _dg(a_bf16, b0, dims) + (_dg(a_bf16, b1, dims) + _dg(a_bf16, b2, dims))


def _dot_exact_rhs(a, b_bf16, dims=NN):
    a0, a1, a2 = _split3(a)
    return _dg(a0, b_bf16, dims) + (_dg(a1, b_bf16, dims) + _dg(a2, b_bf16, dims))


def _head_sum(a, ones_bf16):
    a0, a1 = _split2(a)
    return _dg(a0, ones_bf16, NN) + _dg(a1, ones_bf16, NN)


def _softplus(x):
    return jnp.maximum(x, 0.0) + jnp.log1p(jnp.exp(-jnp.abs(x)))


def _rmsnorm_kernel(x_ref, g_ref, o_ref):
    x = x_ref[...]
    ms = jnp.mean(x * x, axis=-1, keepdims=True)
    o_ref[...] = (x * lax.rsqrt(ms + RMS_EPS) * g_ref[...]).astype(o_ref.dtype)


def rmsnorm(x, g, tm=256):
    m, d = x.shape
    tm = min(tm, m)
    return pl.pallas_call(
        _rmsnorm_kernel,
        grid=(m // tm,),
        in_specs=[pl.BlockSpec((tm, d), lambda i: (i, 0)),
                  pl.BlockSpec((1, d), lambda i: (0, 0))],
        out_specs=pl.BlockSpec((tm, d), lambda i: (i, 0)),
        out_shape=jax.ShapeDtypeStruct((m, d), BF16),
        compiler_params=_cparams(("parallel",)),
        name="rmsnorm",
    )(x, g.reshape(1, d))


def _resnorm_kernel(x_ref, y_ref, gp_ref, gn_ref, xo_ref, h_ref, *, scale):
    y = y_ref[...].astype(F32)
    ms = jnp.mean(y * y, axis=-1, keepdims=True)
    xn = x_ref[...] + scale * (y * lax.rsqrt(ms + RMS_EPS) * gp_ref[...])
    xo_ref[...] = xn
    ms2 = jnp.mean(xn * xn, axis=-1, keepdims=True)
    h_ref[...] = (xn * lax.rsqrt(ms2 + RMS_EPS) * gn_ref[...]).astype(h_ref.dtype)


def _resnorm_last_kernel(x_ref, y_ref, gp_ref, xo_ref, *, scale):
    y = y_ref[...]
    ms = jnp.mean(y * y, axis=-1, keepdims=True)
    xo_ref[...] = x_ref[...] + scale * (y * lax.rsqrt(ms + RMS_EPS) * gp_ref[...])


def resnorm(x, y, g_post, scale, g_next=None, tm=256):
    m, d = x.shape
    row = pl.BlockSpec((tm, d), lambda i: (i, 0))
    vec = pl.BlockSpec((1, d), lambda i: (0, 0))
    if g_next is None:
        return pl.pallas_call(
            functools.partial(_resnorm_last_kernel, scale=scale),
            grid=(m // tm,),
            in_specs=[row, row, vec],
            out_specs=row,
            out_shape=jax.ShapeDtypeStruct((m, d), F32),
            compiler_params=_cparams(("parallel",)),
            name="resnorm_last",
        )(x, y, g_post.reshape(1, d))
    return pl.pallas_call(
        functools.partial(_resnorm_kernel, scale=scale),
        grid=(m // tm,),
        in_specs=[row, row, vec, vec],
        out_specs=[row, row],
        out_shape=[jax.ShapeDtypeStruct((m, d), F32),
                   jax.ShapeDtypeStruct((m, d), BF16)],
        compiler_params=_cparams(("parallel",)),
        name="resnorm",
    )(x, y, g_post.reshape(1, d), g_next.reshape(1, d))


def _mm_kernel(a_ref, w_ref, o_ref):
    o_ref[...] = jnp.dot(a_ref[...], w_ref[...].astype(BF16),
                         preferred_element_type=F32).astype(o_ref.dtype)


def _mm_scaled_kernel(a_ref, w_ref, s_ref, o_ref):
    acc = jnp.dot(a_ref[...], w_ref[...].astype(BF16), preferred_element_type=F32)
    o_ref[...] = (acc * s_ref[...]).astype(o_ref.dtype)


def _mm2_kernel(a1_ref, a2_ref, w1_ref, w2_ref, o_ref):
    acc = jnp.dot(a1_ref[...], w1_ref[...].astype(BF16), preferred_element_type=F32)
    acc += jnp.dot(a2_ref[...], w2_ref[...].astype(BF16), preferred_element_type=F32)
    o_ref[...] = acc.astype(o_ref.dtype)


def matmul(a, w, out_dtype, tm=1024, tn=1024, col_scale=None, n=None):
    m, k = a.shape
    n = w.shape[1] if n is None else n
    tm = min(tm, m)
    tn = min(tn, n)
    assert m % tm == 0 and n % tn == 0
    in_specs = [pl.BlockSpec((tm, k), lambda i, j: (i, 0)),
                pl.BlockSpec((k, tn), lambda i, j: (0, j))]
    args = (a, w)
    kern = _mm_kernel
    if col_scale is not None:
        in_specs.append(pl.BlockSpec((1, tn), lambda i, j: (0, j)))
        args = (a, w, col_scale)
        kern = _mm_scaled_kernel
    return pl.pallas_call(
        kern,
        grid=(m // tm, n // tn),
        in_specs=in_specs,
        out_specs=pl.BlockSpec((tm, tn), lambda i, j: (i, j)),
        out_shape=jax.ShapeDtypeStruct((m, n), out_dtype),
        compiler_params=_cparams(("parallel", "arbitrary")),
        name="matmul",
    )(*args)


def _mm_nt_kernel(a_ref, wt_ref, o_ref):
    o_ref[...] = _dg(a_ref[...], wt_ref[...].astype(BF16), NT).astype(o_ref.dtype)


def _mm_nt_scaled_kernel(a_ref, wt_ref, s_ref, o_ref):
    acc = _dg(a_ref[...], wt_ref[...].astype(BF16), NT)
    o_ref[...] = (acc * s_ref[...]).astype(o_ref.dtype)


def matmul_nt(a, wt, out_dtype, row0, n, tm=1024, tn=512, col_scale=None):
    m, k = a.shape
    tm = min(tm, m)
    tn = min(tn, n)
    assert m % tm == 0 and n % tn == 0 and row0 % 8 == 0 and wt.shape[1] == k
    in_specs = [pl.BlockSpec((tm, k), lambda i, j: (i, 0)),
                pl.BlockSpec((pl.Element(tn), pl.Element(k)),
                             lambda i, j: (pl.multiple_of(row0 + j * tn, 8), 0))]
    args = (a, wt)
    kern = _mm_nt_kernel
    if col_scale is not None:
        in_specs.append(pl.BlockSpec((1, tn), lambda i, j: (0, j)))
        args = (a, wt, col_scale)
        kern = _mm_nt_scaled_kernel
    return pl.pallas_call(
        kern,
        grid=(m // tm, n // tn),
        in_specs=in_specs,
        out_specs=pl.BlockSpec((tm, tn), lambda i, j: (i, j)),
        out_shape=jax.ShapeDtypeStruct((m, n), out_dtype),
        compiler_params=_cparams(("parallel", "arbitrary")),
        name="matmul_nt",
    )(*args)


def matmul2(a1, a2, w, out_dtype, tm=1024, tn=1024):
    m, k1 = a1.shape
    k2 = a2.shape[1]
    n = w.shape[1]
    assert k1 == k2 and w.shape[0] == k1 + k2
    tm = min(tm, m)
    tn = min(tn, n)
    assert m % tm == 0 and n % tn == 0
    return pl.pallas_call(
        _mm2_kernel,
        grid=(m // tm, n // tn),
        in_specs=[pl.BlockSpec((tm, k1), lambda i, j: (i, 0)),
                  pl.BlockSpec((tm, k2), lambda i, j: (i, 0)),
                  pl.BlockSpec((k1, tn), lambda i, j: (0, j)),
                  pl.BlockSpec((k2, tn), lambda i, j: (1, j))],
        out_specs=pl.BlockSpec((tm, tn), lambda i, j: (i, j)),
        out_shape=jax.ShapeDtypeStruct((m, n), out_dtype),
        compiler_params=_cparams(("parallel", "arbitrary")),
        name="matmul2",
    )(a1, a2, w, w)


def _ffn_kernel(h_ref, wg_ref, wu_ref, wd_ref, o_ref, *, n_chunk):
    f = pl.program_id(1)

    @pl.when(f == 0)
    def _():
        o_ref[...] = jnp.zeros_like(o_ref)

    h = h_ref[...]
    g = jnp.dot(h, wg_ref[...].astype(BF16), preferred_element_type=F32)
    u = jnp.dot(h, wu_ref[...].astype(BF16), preferred_element_type=F32)
    act = (g * jax.nn.sigmoid(g) * u).astype(BF16)
    d = o_ref.shape[1]
    for c in range(0, d, n_chunk):
        o_ref[:, c:c + n_chunk] += jnp.dot(act, wd_ref[:, c:c + n_chunk].astype(BF16),
                                           preferred_element_type=F32)


def ffn(h, wg, wu, wd, tm=1024, tf=256, n_chunk=512):
    m, d = h.shape
    dff = wg.shape[1]
    tm = min(tm, m)
    tf = min(tf, dff)
    n_chunk = min(n_chunk, d)
    assert m % tm == 0 and dff % tf == 0 and d % n_chunk == 0
    return pl.pallas_call(
        functools.partial(_ffn_kernel, n_chunk=n_chunk),
        grid=(m // tm, dff // tf),
        in_specs=[pl.BlockSpec((tm, d), lambda i, f: (i, 0), pipeline_mode=pl.Buffered(1)),
                  pl.BlockSpec((d, tf), lambda i, f: (0, f)),
                  pl.BlockSpec((d, tf), lambda i, f: (0, f)),
                  pl.BlockSpec((tf, d), lambda i, f: (f, 0))],
        out_specs=pl.BlockSpec((tm, d), lambda i, f: (i, 0), pipeline_mode=pl.Buffered(1)),
        out_shape=jax.ShapeDtypeStruct((m, d), F32),
        compiler_params=_cparams(("parallel", "arbitrary"), vmem=FFN_VMEM_LIMIT),
        name="ffn",
    )(h, wg, wu, wd)


def _fgate_kernel(f_ref, b_ref, c_ref, *, blk):
    s = f_ref.shape[0]
    row = lax.broadcasted_iota(jnp.int32, (blk, blk), 0)
    col = lax.broadcasted_iota(jnp.int32, (blk, blk), 1)
    tri = jnp.where(row >= col, 1.0, 0.0).astype(BF16)
    carry = jnp.zeros((1, f_ref.shape[1]), F32)
    for r0 in range(0, s, blk):
        x = f_ref[r0:r0 + blk, :] + b_ref[...]
        logf = -_softplus(-x)
        c = _dot_exact_lhs(tri, logf) + carry
        c_ref[r0:r0 + blk, :] = c
        carry = c[blk - 1:blk, :]


def fgate_cumsum(small, f_bias_row, f_block, seq, blk=256):
    m = small.shape[0]
    blk = min(blk, seq)
    return pl.pallas_call(
        functools.partial(_fgate_kernel, blk=blk),
        grid=(m // seq,),
        in_specs=[pl.BlockSpec((seq, LANES), lambda b: (b, f_block)),
                  pl.BlockSpec((1, LANES), lambda b: (0, 0))],
        out_specs=pl.BlockSpec((seq, LANES), lambda b: (b, 0)),
        out_shape=jax.ShapeDtypeStruct((m, LANES), F32),
        compiler_params=_cparams(("parallel",)),
        name="fgate_cumsum",
    )(small, f_bias_row)


LOG2E = 1.4426950408889634
FOX_Q_SCALE = FOX_HEAD_DIM ** -0.5 * LOG2E


def _fox_kernel(q_ref, k_ref, v_ref, cq_ref, ck_ref, o_ref, s_scr, p_scr, m_scr, cq_scr,
                acc_scr, *, tq, tk, hg, strip):
    qi = pl.program_id(2)
    dh = FOX_HEAD_DIM
    hs = range(hg)
    nl = tk // dh
    m_scr[...] = jnp.full_like(m_scr, NEG_INF)
    acc_scr[...] = jnp.zeros_like(acc_scr)
    c_blk = cq_ref[...] * LOG2E
    lane_id = lax.broadcasted_iota(jnp.int32, (dh, dh), 0)
    for g in hs:
        pick = jnp.where(lane_id == pl.program_id(1) * hg + g, 1.0, 0.0).astype(BF16)
        cq_scr[g] = _dot_exact_rhs(c_blk, pick)
    ones = jnp.ones((tk, dh), BF16)
    col = lax.broadcasted_iota(jnp.int32, (strip, tk), 1)
    row = lax.broadcasted_iota(jnp.int32, (strip, tk), 0)

    def head_cols(g):
        return slice(g * dh, (g + 1) * dh)

    def lane_tile(x, n):
        return x if n == 1 else jnp.concatenate([x] * n, axis=1)

    def block(ks, r0, mask_off):
        for g in hs:
            s_scr[g, r0:, :] = _dg(q_ref[r0:, head_cols(g)],
                                   k_ref[pl.ds(ks, tk), head_cols(g)], NT)
        for g in hs:
            ckg = ck_ref[g, :, pl.ds(ks, tk)] * LOG2E
            for r in range(r0, tq, strip):
                rows = slice(r, r + strip)
                s = s_scr[g, rows, :] - ckg
                if mask_off is not None and r - mask_off < tk - 1:
                    s = jnp.where(row + (r - mask_off) >= col, s, NEG_INF)
                cqr = cq_scr[g, rows, :]
                m_prev = m_scr[g, rows, :]
                m_new = jnp.maximum(m_prev, jnp.max(s, axis=-1, keepdims=True) + cqr)
                p_scr[g, rows, :] = jnp.exp2(s + lane_tile(cqr - m_new, nl)).astype(BF16)
                m_scr[g, rows, :] = m_new
                acc_scr[g, rows, :] = lane_tile(jnp.exp2(m_prev - m_new), 2) * acc_scr[g, rows, :]
        for g in hs:
            v_aug = jnp.concatenate([v_ref[pl.ds(ks, tk), head_cols(g)], ones], axis=1)
            acc_scr[g, r0:, :] += _dg(p_scr[g, r0:, :], v_aug, NN)

    def body(ki, carry):
        block(pl.multiple_of(ki * tk, tk), 0, None)
        return carry

    n_sub = tq // tk
    lax.fori_loop(0, qi * n_sub, body, 0)
    for j in range(n_sub):
        block(pl.multiple_of(qi * tq + j * tk, tk), j * tk, j * tk)
    for g in hs:
        acc = acc_scr[g]
        o_ref[:, head_cols(g)] = (acc[:, :dh] / acc[:, dh:]).astype(o_ref.dtype)


def fox_attention(qkv, c_tok, c_row, batch, seq, heads, tq=512, tk=256, hg=4, strip=64):
    tq = min(tq, seq)
    tk = min(tk, tq)
    strip = min(strip, tk)
    nq = seq // tq
    dh = FOX_HEAD_DIM
    ng = heads // hg
    assert heads % hg == 0 and tq % tk == 0 and tk % strip == 0 and seq % tq == 0
    return pl.pallas_call(
        functools.partial(_fox_kernel, tq=tq, tk=tk, hg=hg, strip=strip),
        grid=(batch, ng, nq),
        in_specs=[
            pl.BlockSpec((tq, hg * dh), lambda b, h, qi: (b * nq + qi, h)),
            pl.BlockSpec((seq, hg * dh), lambda b, h, qi: (b, ng + h)),
            pl.BlockSpec((seq, hg * dh), lambda b, h, qi: (b, 2 * ng + h)),
            pl.BlockSpec((tq, LANES), lambda b, h, qi: (b * nq + qi, 0)),
            pl.BlockSpec((None, hg, 1, seq), lambda b, h, qi: (b, h, 0, 0)),
        ],
        out_specs=pl.BlockSpec((tq, hg * dh), lambda b, h, qi: (b * nq + qi, h)),
        out_shape=jax.ShapeDtypeStruct((batch * seq, heads * dh), BF16),
        scratch_shapes=[pltpu.VMEM((hg, tq, tk), F32), pltpu.VMEM((hg, tq, tk), BF16),
                        pltpu.VMEM((hg, tq, dh), F32), pltpu.VMEM((hg, tq, dh), F32),
                        pltpu.VMEM((hg, tq, 2 * dh), F32)],
        compiler_params=_cparams(("parallel", "parallel", "arbitrary")),
        name="fox_attention",
    )(qkv, qkv, qkv, c_tok, c_row)


def _rwkv_prep_kernel(r_ref, k_ref, v_ref, rp_ref, kp_ref, vp_ref, sm_ref, smp_ref,
                      mu_r_ref, mu_k_ref, mu_v_ref, mu_s_ref,
                      w0_ref, wup_ref, a0_ref, aup_ref, gup_ref, kk_ref, ka_ref,
                      r_o, lw_o, k_o, v_o, kk_o, b_o, g_o, *, tiles_per_seq):
    i = pl.program_id(0)
    first = (i % tiles_per_seq) == 0
    tm = r_ref.shape[0]
    row0 = lax.broadcasted_iota(jnp.int32, (tm, 1), 0) == 0

    def mix(cur_ref, prev_ref, mu_ref):
        cur = cur_ref[...]
        last = jnp.where(first, 0.0, prev_ref[7:8, :])
        prev = jnp.where(row0, last, pltpu.roll(cur, 1, axis=0))
        return cur + mu_ref[...] * (prev - cur)

    zr = mix(r_ref, rp_ref, mu_r_ref)
    zk = mix(k_ref, kp_ref, mu_k_ref)
    zv = mix(v_ref, vp_ref, mu_v_ref)
    zs = mix(sm_ref, smp_ref, mu_s_ref)
    zw = zs[:, 0:LANES]
    za = zs[:, LANES:2 * LANES]
    zg = zs[:, 2 * LANES:2 * LANES + GATE_LORA]

    def lora(z, w_ref):
        return _dg(z.astype(BF16), w_ref[...].astype(BF16), NN)

    lw_o[...] = -EXP_NEG_HALF * jax.nn.sigmoid(w0_ref[...] + lora(jnp.tanh(zw), wup_ref))
    a = jax.nn.sigmoid(a0_ref[...] + lora(za, aup_ref))
    g_o[...] = lora(jax.nn.sigmoid(zg), gup_ref)

    tc = zk.shape[1]
    hr = lax.broadcasted_iota(jnp.int32, (LANES, LANES), 0) // RWKV_HEAD_DIM
    hc = lax.broadcasted_iota(jnp.int32, (LANES, LANES), 1) // RWKV_HEAD_DIM
    head_ones = jnp.where(hr == hc, 1.0, 0.0).astype(BF16)
    kk = zk * kk_ref[...]
    sq = kk * kk
    ss = jnp.concatenate([_head_sum(sq[:, j:j + LANES], head_ones)
                          for j in range(0, tc, LANES)], axis=1)
    kk = kk * lax.rsqrt(jnp.maximum(ss, 1e-24))
    r_o[...] = zr
    v_o[...] = zv
    kk_o[...] = kk
    b_o[...] = kk * a
    k_o[...] = zk * (1.0 + (a - 1.0) * ka_ref[...])


def rwkv_prep(rkv, small, mu_r, mu_k, mu_v, mu_s, w0, w_up, a0, a_up, g_up, k_k, k_a,
              seq, tm=512, tc=512):
    m = rkv.shape[0]
    width = RWKV_WIDTH
    tm = min(tm, seq)
    nc = width // tc
    ns = small.shape[1]
    tps = seq // tm
    sub = tm // 8

    def cur(piece):
        return pl.BlockSpec((tm, tc), lambda i, j: (i, piece * nc + j))

    def prev(piece):
        return pl.BlockSpec((8, tc), lambda i, j: (jnp.maximum(i * sub - 1, 0), piece * nc + j))

    colv = pl.BlockSpec((1, tc), lambda i, j: (0, j))
    out = pl.BlockSpec((tm, tc), lambda i, j: (i, j))
    outs = [jax.ShapeDtypeStruct((m, width), F32)] * 7
    return pl.pallas_call(
        functools.partial(_rwkv_prep_kernel, tiles_per_seq=tps),
        grid=(m // tm, nc),
        in_specs=[cur(0), cur(1), cur(2), prev(0), prev(1), prev(2),
                  pl.BlockSpec((tm, ns), lambda i, j: (i, 0)),
                  pl.BlockSpec((8, ns), lambda i, j: (jnp.maximum(i * sub - 1, 0), 0)),
                  colv, colv, colv,
                  pl.BlockSpec((1, ns), lambda i, j: (0, 0)),
                  colv,
                  pl.BlockSpec((LANES, tc), lambda i, j: (0, j)),
                  colv,
                  pl.BlockSpec((LANES, tc), lambda i, j: (0, j)),
                  pl.BlockSpec((GATE_LORA, tc), lambda i, j: (0, j)),
                  colv, colv],
        out_specs=[out] * 7,
        out_shape=outs,
        compiler_params=_cparams(("parallel", "arbitrary")),
        name="rwkv_prep",
    )(rkv, rkv, rkv, rkv, rkv, rkv, small, small,
      mu_r, mu_k, mu_v, mu_s, w0, w_up, a0, a_up, g_up, k_k, k_a)


def _rwkv_kernel(r_ref, lw_ref, k_ref, v_ref, kk_ref, b_ref, g_ref,
                 lnw_ref, lnb_ref, rk_ref, o_ref, s_ref, *, pairs):
    ci = pl.program_id(2)
    c = CHUNK
    hd = RWKV_HEAD_DIM

    @pl.when(ci == 0)
    def _():
        s_ref[...] = jnp.zeros_like(s_ref)

    r2 = lax.broadcasted_iota(jnp.int32, (LANES, LANES), 0)
    c2 = lax.broadcasted_iota(jnp.int32, (LANES, LANES), 1)
    same = (r2 < c) == (c2 < c)
    strict = jnp.logical_and(same, r2 > c2)
    incl = jnp.logical_and(same, r2 >= c2)
    head_ones = jnp.where((r2 < hd) == (c2 < hd), 1.0, 0.0).astype(BF16)
    rc = lax.broadcasted_iota(jnp.int32, (c, c), 0)
    cc = lax.broadcasted_iota(jnp.int32, (c, c), 1)
    tri_incl = jnp.where(rc >= cc, 1.0, 0.0).astype(BF16)
    m_a = lax.broadcasted_iota(jnp.int32, (1, LANES), 1) < hd
    inv_n = 1.0 / hd
    prs = range(pairs)
    cat = lambda xs: jnp.concatenate(xs, axis=0)

    def lanes(x, p):
        return x[:, p * LANES:(p + 1) * LANES]

    def stack2(x):
        return cat([jnp.where(m_a, x, 0.0), jnp.where(m_a, 0.0, x)])

    lw = lw_ref[...]
    r, k, v = r_ref[...], k_ref[...], v_ref[...]
    kk, beta = kk_ref[...], b_ref[...]
    big_l = _dot_exact_lhs(tri_incl, lw)
    l_tot = big_l[c - 1:c, :]
    e_nl = jnp.exp(-big_l)
    e_rem = jnp.exp(l_tot - big_l)
    e_tot = jnp.exp(l_tot)
    at = -kk * jnp.exp(big_l - lw)
    rt = r * jnp.exp(big_l)
    kt = k * e_nl
    bt = beta * e_nl
    bend = beta * e_rem
    kend = k * e_rem

    s0 = [s_ref[p] for p in prs]
    lhs = [cat([stack2(lanes(at, p)), stack2(lanes(rt, p))]).astype(BF16) for p in prs]
    wbk = [cat([lanes(bt, p), lanes(kt, p)]).astype(BF16) for p in prs]
    v2 = [stack2(lanes(v, p)).astype(BF16) for p in prs]
    m = [_dg(lhs[p], wbk[p], NT) for p in prs]
    sx = [_dg(lhs[p], s0[p].astype(BF16), NT) for p in prs]

    a_ab, a_kr, a_rb = [], [], []
    for p in prs:
        mp = m[p]
        sw = pltpu.roll(mp, c, axis=1)
        a_ab.append(jnp.where(strict, cat([mp[0:c], sw[c:2 * c]]), 0.0))
        a_ak = jnp.where(strict, cat([sw[0:c], mp[c:2 * c]]), 0.0)
        a_rb.append(jnp.where(incl, cat([mp[2 * c:3 * c], sw[3 * c:4 * c]]), 0.0))
        a_rk = jnp.where(incl, cat([sw[2 * c:3 * c], mp[3 * c:4 * c]]), 0.0)
        a_kr.append(cat([a_ak, a_rk]).astype(BF16))
    xy = [_dg(a_kr[p], v2[p], NN) for p in prs]
    u = [sx[p][:2 * c] + xy[p][:2 * c] for p in prs]
    ypart = [sx[p][2 * c:] + xy[p][2 * c:] for p in prs]

    def d1(a, b):
        return _dg(a.astype(BF16), b.astype(BF16), NN)

    blk = lambda n: (r2 // n) == (c2 // n)
    eye = jnp.where(r2 == c2, 1.0, 0.0)
    base = 8
    x_in = u
    p1 = [jnp.where(blk(base), a_ab[p], 0.0) for p in prs]
    p2 = [d1(p1[p], p1[p]) for p in prs]
    ip = [eye + p1[p] for p in prs]
    p4 = [d1(p2[p], p2[p]) for p in prs]
    m1 = [ip[p] + d1(ip[p], p2[p]) for p in prs]
    t = [m1[p] + d1(m1[p], p4[p]) for p in prs]
    n = base
    while n < c:
        quad = jnp.logical_and(blk(2 * n), jnp.logical_not(blk(n)))
        et = [d1(jnp.where(quad, a_ab[p], 0.0), t[p]) for p in prs]
        t = [t[p] + d1(t[p], et[p]) for p in prs]
        n *= 2
    u1 = [d1(t[p], x_in[p]) for p in prs]
    res = [x_in[p] - u1[p] + _dot3(a_ab[p], u1[p]) for p in prs]
    u = [u1[p] + d1(t[p], res[p]) for p in prs]

    ub = [u[p].astype(BF16) for p in prs]
    y2 = [ypart[p] + _dg(a_rb[p].astype(BF16), ub[p], NN) for p in prs]
    for p in prs:
        ends = cat([stack2(lanes(bend, p)), stack2(lanes(kend, p))]).astype(BF16)
        s_ref[p] = s0[p] * lanes(e_tot, p) + _dg(cat([ub[p], v2[p]]), ends, TN)

    y_all = cat([y2[p][:c] + y2[p][c:] for p in prs])
    mean = _head_sum(y_all, head_ones) * inv_n
    yc = y_all - mean
    var = _head_sum(yc * yc, head_ones) * inv_n
    yn = yc * lax.rsqrt(var + RWKV_GN_EPS)
    rkk = r * k * rk_ref[...]
    bonus = _head_sum(cat([lanes(rkk, p) for p in prs]), head_ones)
    lnw, lnb, g = lnw_ref[...], lnb_ref[...], g_ref[...]
    for p in prs:
        rows = slice(p * c, (p + 1) * c)
        out = (yn[rows] * lanes(lnw, p) + lanes(lnb, p) + bonus[rows] * lanes(v, p)) * lanes(g, p)
        o_ref[:, p * LANES:(p + 1) * LANES] = out.astype(o_ref.dtype)


def rwkv_recurrence(r, lw, k, v, kk, beta, g, ln_w, ln_b, r_k, batch, seq, pairs=16):
    m, width = r.shape
    c = CHUNK
    assert seq % c == 0 and 2 * c == LANES and 2 * RWKV_HEAD_DIM == LANES
    tc = pairs * LANES
    nchunk = seq // c
    blk = pl.BlockSpec((c, tc), lambda b, j, ci: (b * nchunk + ci, j))
    vec = pl.BlockSpec((1, tc), lambda b, j, ci: (0, j))
    return pl.pallas_call(
        functools.partial(_rwkv_kernel, pairs=pairs),
        grid=(batch, width // tc, nchunk),
        in_specs=[blk] * 7 + [vec] * 3,
        out_specs=blk,
        out_shape=jax.ShapeDtypeStruct((m, width), BF16),
        scratch_shapes=[pltpu.VMEM((pairs, LANES, LANES), F32)],
        compiler_params=_cparams(("parallel", "parallel", "arbitrary")),
        name="rwkv_recurrence",
    )(r, lw, k, v, kk, beta, g, ln_w, ln_b, r_k)


def _xattn_kernel(q_ref, k_ref, v_ref, o_ref, *, scale):
    s = _dg(q_ref[...], k_ref[...], NT) * scale
    m = jnp.max(s, axis=-1, keepdims=True)
    p = jnp.exp(s - m)
    l = jnp.sum(p, axis=-1, keepdims=True)
    o = jnp.dot(p.astype(BF16), v_ref[...], preferred_element_type=F32)
    o_ref[...] = (o / l).astype(o_ref.dtype)


def cross_attention(q, k, v, batch, seq, n_mem, heads, tq=2048):
    m, d = q.shape
    dh = d // heads
    tq = min(tq, seq)
    nq = seq // tq
    return pl.pallas_call(
        functools.partial(_xattn_kernel, scale=dh ** -0.5),
        grid=(batch, heads, nq),
        in_specs=[pl.BlockSpec((tq, dh), lambda b, h, qi: (b * nq + qi, h)),
                  pl.BlockSpec((n_mem, dh), lambda b, h, qi: (b, h)),
                  pl.BlockSpec((n_mem, dh), lambda b, h, qi: (b, h))],
        out_specs=pl.BlockSpec((tq, dh), lambda b, h, qi: (b * nq + qi, h)),
        out_shape=jax.ShapeDtypeStruct((m, d), BF16),
        compiler_params=_cparams(("parallel", "parallel", "arbitrary")),
        name="cross_attention",
    )(q, k, v)


def _pad_cols(w, n):
    return jnp.pad(w, ((0, 0), (0, n - w.shape[1])))


def _pad_rows(w, n):
    return jnp.pad(w, ((0, n - w.shape[0]), (0, 0)))


def _bf(w):
    return w.astype(BF16)


def _row(t):
    return t.reshape(1, -1).astype(F32)


def _mixer(h, p, batch, seq):
    fox_w, rw = FOX_WIDTH, RWKV_WIDTH
    fox_heads = fox_w // FOX_HEAD_DIM
    bf, row = _bf, _row

    w_in = p['w_in']
    o_f = 3 * fox_w
    o_r = o_f + fox_heads
    o_w = o_r + 3 * rw
    o_a = o_w + DECAY_LORA
    o_g = o_a + ICLR_LORA
    wt = w_in.T
    wt_small = jnp.concatenate([
        _pad_rows(wt[o_w:o_a], LANES), _pad_rows(wt[o_a:o_g], LANES),
        wt[o_g:o_g + GATE_LORA], _pad_rows(wt[o_f:o_r], LANES)], axis=0)
    mu = p['rwkv_mu']
    mu_small = jnp.concatenate([
        jnp.pad(mu[3 * rw:3 * rw + DECAY_LORA], (0, LANES - DECAY_LORA)),
        jnp.pad(mu[3 * rw + DECAY_LORA:3 * rw + DECAY_LORA + ICLR_LORA], (0, LANES - ICLR_LORA)),
        mu[3 * rw + DECAY_LORA + ICLR_LORA:], jnp.zeros((LANES,), F32)])

    q_scale = jnp.concatenate([jnp.full((1, fox_w), FOX_Q_SCALE, F32),
                               jnp.ones((1, 2 * fox_w), F32)], axis=1)
    qkv = matmul_nt(h, wt, BF16, 0, o_f, col_scale=q_scale)
    rkv = matmul_nt(h, wt, F32, o_r, 3 * rw)
    small = matmul_nt(h, wt_small, F32, 0, wt_small.shape[0], tn=wt_small.shape[0])

    f_bias = jnp.pad(p['fox_f_bias'], (0, LANES - fox_heads)).reshape(1, LANES)
    c = fgate_cumsum(small, f_bias, f_block=(2 * LANES + GATE_LORA) // LANES, seq=seq)
    c_row = c[:, :fox_heads].reshape(batch, seq, fox_heads).transpose(0, 2, 1)[:, :, None, :]
    y_fox = fox_attention(qkv, c, c_row, batch, seq, fox_heads)

    r, lw, k, v, kk, beta, g = rwkv_prep(
        rkv, small, row(mu[:rw]), row(mu[rw:2 * rw]), row(mu[2 * rw:3 * rw]), row(mu_small),
        row(p['rwkv_w0']), _pad_rows(p['rwkv_w_up'], LANES), row(p['rwkv_a0']),
        _pad_rows(p['rwkv_a_up'], LANES), p['rwkv_g_up'], row(p['rwkv_k_k']),
        row(p['rwkv_k_a']), seq=seq)
    y_rwkv = rwkv_recurrence(r, lw, k, v, kk, beta, g, row(p['rwkv_ln_w']),
                             row(p['rwkv_ln_b']), row(p['rwkv_r_k']), batch, seq)

    return matmul2(y_fox, y_rwkv, p['w_out'], BF16, tn=F32_W_TN)


def _xattn(h, mn, p, batch, seq, n_mem):
    q = matmul(h, p['xattn_wq'], BF16, tn=F32_W_TN)
    km = matmul(mn, p['xattn_wk'], BF16, tn=F32_W_TN)
    vm = matmul(mn, p['xattn_wv'], BF16, tn=F32_W_TN)
    o = cross_attention(q, km, vm, batch, seq, n_mem, XATTN_HEADS)
    return matmul(o, p['xattn_wo'], BF16, tn=F32_W_TN)


def _layer(x, mem, p, batch, seq, n_mem):
    h = rmsnorm(x, p['ffn1_pre_g'])
    y = ffn(h, p['ffn1_w_gate'], p['ffn1_w_up'], p['ffn1_w_down'])
    x, h = resnorm(x, y, p['ffn1_post_g'], 0.5, p['mix_pre_g'])

    mixed = _mixer(h, p, batch, seq)
    x, h = resnorm(x, mixed, p['mix_post_g'], 1.0, p['xattn_pre_g'])

    mn = rmsnorm(mem, p['mem_norm_g'])
    xa = _xattn(h, mn, p, batch, seq, n_mem)
    x, h = resnorm(x, xa, p['xattn_post_g'], 1.0, p['ffn2_pre_g'])

    y = ffn(h, p['ffn2_w_gate'], p['ffn2_w_up'], p['ffn2_w_down'])
    return resnorm(x, y, p['ffn2_post_g'], 0.5)


_PARAM_NAMES = (
    'ffn1_pre_g', 'ffn1_w_gate', 'ffn1_w_up', 'ffn1_w_down', 'ffn1_post_g',
    'mix_pre_g', 'w_in', 'fox_f_bias', 'rwkv_mu', 'rwkv_w0', 'rwkv_w_up', 'rwkv_a0',
    'rwkv_a_up', 'rwkv_g_up', 'rwkv_k_k', 'rwkv_k_a', 'rwkv_r_k', 'rwkv_ln_w', 'rwkv_ln_b',
    'w_out', 'mix_post_g', 'xattn_pre_g', 'mem_norm_g', 'xattn_wq', 'xattn_wk', 'xattn_wv',
    'xattn_wo', 'xattn_post_g', 'ffn2_pre_g', 'ffn2_w_gate', 'ffn2_w_up', 'ffn2_w_down',
    'ffn2_post_g')


def kernel(x, mem, ffn1_pre_g, ffn1_w_gate, ffn1_w_up, ffn1_w_down, ffn1_post_g, mix_pre_g, w_in, fox_f_bias, rwkv_mu, rwkv_w0, rwkv_w_up, rwkv_a0, rwkv_a_up, rwkv_g_up, rwkv_k_k, rwkv_k_a, rwkv_r_k, rwkv_ln_w, rwkv_ln_b, w_out, mix_post_g, xattn_pre_g, mem_norm_g, xattn_wq, xattn_wk, xattn_wv, xattn_wo, xattn_post_g, ffn2_pre_g, ffn2_w_gate, ffn2_w_up, ffn2_w_down, ffn2_post_g):
    weights = (ffn1_pre_g, ffn1_w_gate, ffn1_w_up, ffn1_w_down, ffn1_post_g, mix_pre_g, w_in,
               fox_f_bias, rwkv_mu, rwkv_w0, rwkv_w_up, rwkv_a0, rwkv_a_up, rwkv_g_up, rwkv_k_k,
               rwkv_k_a, rwkv_r_k, rwkv_ln_w, rwkv_ln_b, w_out, mix_post_g, xattn_pre_g,
               mem_norm_g, xattn_wq, xattn_wk, xattn_wv, xattn_wo, xattn_post_g, ffn2_pre_g,
               ffn2_w_gate, ffn2_w_up, ffn2_w_down, ffn2_post_g)
    batch, seq, d = x.shape
    n_mem = mem.shape[1]
    depth = ffn1_pre_g.shape[0]
    xf = x.reshape(batch * seq, d)
    mf = mem.reshape(batch * n_mem, d)
    for l in range(depth):
        p = {name: w[l] for name, w in zip(_PARAM_NAMES, weights)}
        xf = _layer(xf, mf, p, batch, seq, n_mem)
    return xf.reshape(batch, seq, d)
```

```python
import functools

import jax
import jax.numpy as jnp
from jax import lax
from jax.experimental import pallas as pl
from jax.experimental.pallas import tpu as pltpu

F32 = jnp.float32
BF16 = jnp.bfloat16

SEQ = 2048
FOX_HEAD_DIM = 128
RWKV_HEAD_DIM = 64
RWKV_WIDTH = 2048
FOX_WIDTH = 2048
DECAY_LORA = 96
ICLR_LORA = 96
GATE_LORA = 256
XATTN_HEADS = 4
RMS_EPS = 1e-6
RWKV_GN_EPS = 64e-5
NEG_INF = -1e30

EXP_NEG_HALF = 0.6065306597126334
LANES = 128
CHUNK = 64
VMEM_LIMIT = 56 * 1024 * 1024
FFN_VMEM_LIMIT = 62 * 1024 * 1024
F32_W_TN = 512
PROJ_TN = 768


def _cparams(sem, vmem=VMEM_LIMIT):
    return pltpu.CompilerParams(dimension_semantics=sem, vmem_limit_bytes=vmem)


NN = (((1,), (0,)), ((), ()))
NT = (((1,), (1,)), ((), ()))
TN = (((0,), (0,)), ((), ()))


def _split2(x):
    hi = x.astype(BF16)
    lo = (x - hi.astype(F32)).astype(BF16)
    return hi, lo


def _split3(x):
    hi = x.astype(BF16)
    r = x - hi.astype(F32)
    mid = r.astype(BF16)
    lo = (r - mid.astype(F32)).astype(BF16)
    return hi, mid, lo


def _dg(a, b, dims):
    return lax.dot_general(a, b, dims, preferred_element_type=F32)


def _dot3(a, b, dims=NN):
    ah, al = _split2(a)
    bh, bl = _split2(b)
    return _dg(ah, bh, dims) + (_dg(ah, bl, dims) + _dg(al, bh, dims))


def _dot_exact_lhs(a_bf16, b, dims=NN):
    b0, b1, b2 = _split3(b)
    return _dg(a_bf16, b0, dims) + (_dg(a_bf16, b1, dims) + _dg(a_bf16, b2, dims))


def _dot_exact_rhs(a, b_bf16, dims=NN):
    a0, a1, a2 = _split3(a)
    return _dg(a0, b_bf16, dims) + (_dg(a1, b_bf16, dims) + _dg(a2, b_bf16, dims))


def _head_sum(a, ones_bf16):
    a0, a1 = _split2(a)
    return _dg(a0, ones_bf16, NN) + _dg(a1, ones_bf16, NN)


def _softplus(x):
    return jnp.maximum(x, 0.0) + jnp.log1p(jnp.exp(-jnp.abs(x)))


def _rmsnorm_kernel(x_ref, g_ref, o_ref):
    x = x_ref[...]
    ms = jnp.mean(x * x, axis=-1, keepdims=True)
    o_ref[...] = (x * lax.rsqrt(ms + RMS_EPS) * g_ref[...]).astype(o_ref.dtype)


def rmsnorm(x, g, tm=256):
    m, d = x.shape
    tm = min(tm, m)
    return pl.pallas_call(
        _rmsnorm_kernel,
        grid=(m // tm,),
        in_specs=[pl.BlockSpec((tm, d), lambda i: (i, 0)),
                  pl.BlockSpec((1, d), lambda i: (0, 0))],
        out_specs=pl.BlockSpec((tm, d), lambda i: (i, 0)),
        out_shape=jax.ShapeDtypeStruct((m, d), BF16),
        compiler_params=_cparams(("parallel",)),
        name="rmsnorm",
    )(x, g.reshape(1, d))


def _resnorm_kernel(x_ref, y_ref, gp_ref, gn_ref, xo_ref, h_ref, *, scale):
    y = y_ref[...].astype(F32)
    ms = jnp.mean(y * y, axis=-1, keepdims=True)
    xn = x_ref[...] + scale * (y * lax.rsqrt(ms + RMS_EPS) * gp_ref[...])
    xo_ref[...] = xn
    ms2 = jnp.mean(xn * xn, axis=-1, keepdims=True)
    h_ref[...] = (xn * lax.rsqrt(ms2 + RMS_EPS) * gn_ref[...]).astype(h_ref.dtype)


def _resnorm_last_kernel(x_ref, y_ref, gp_ref, xo_ref, *, scale):
    y = y_ref[...]
    ms = jnp.mean(y * y, axis=-1, keepdims=True)
    xo_ref[...] = x_ref[...] + scale * (y * lax.rsqrt(ms + RMS_EPS) * gp_ref[...])


def resnorm(x, y, g_post, scale, g_next=None, tm=256):
    m, d = x.shape
    row = pl.BlockSpec((tm, d), lambda i: (i, 0))
    vec = pl.BlockSpec((1, d), lambda i: (0, 0))
    if g_next is None:
        return pl.pallas_call(
            functools.partial(_resnorm_last_kernel, scale=scale),
            grid=(m // tm,),
            in_specs=[row, row, vec],
            out_specs=row,
            out_shape=jax.ShapeDtypeStruct((m, d), F32),
            compiler_params=_cparams(("parallel",)),
            name="resnorm_last",
        )(x, y, g_post.reshape(1, d))
    return pl.pallas_call(
        functools.partial(_resnorm_kernel, scale=scale),
        grid=(m // tm,),
        in_specs=[row, row, vec, vec],
        out_specs=[row, row],
        out_shape=[jax.ShapeDtypeStruct((m, d), F32),
                   jax.ShapeDtypeStruct((m, d), BF16)],
        compiler_params=_cparams(("parallel",)),
        name="resnorm",
    )(x, y, g_post.reshape(1, d), g_next.reshape(1, d))


def _mm_kernel(a_ref, w_ref, o_ref):
    o_ref[...] = jnp.dot(a_ref[...], w_ref[...].astype(BF16),
                         preferred_element_type=F32).astype(o_ref.dtype)


def _mm_scaled_kernel(a_ref, w_ref, s_ref, o_ref):
    acc = jnp.dot(a_ref[...], w_ref[...].astype(BF16), preferred_element_type=F32)
    o_ref[...] = (acc * s_ref[...]).astype(o_ref.dtype)


def _mm2_kernel(a1_ref, a2_ref, w1_ref, w2_ref, o_ref):
    acc = jnp.dot(a1_ref[...], w1_ref[...].astype(BF16), preferred_element_type=F32)
    acc += jnp.dot(a2_ref[...], w2_ref[...].astype(BF16), preferred_element_type=F32)
    o_ref[...] = acc.astype(o_ref.dtype)


def matmul(a, w, out_dtype, tm=1024, tn=1024, col_scale=None, n=None):
    m, k = a.shape
    n = w.shape[1] if n is None else n
    tm = min(tm, m)
    tn = min(tn, n)
    assert m % tm == 0 and n % tn == 0
    in_specs = [pl.BlockSpec((tm, k), lambda i, j: (i, 0)),
                pl.BlockSpec((k, tn), lambda i, j: (0, j))]
    args = (a, w)
    kern = _mm_kernel
    if col_scale is not None:
        in_specs.append(pl.BlockSpec((1, tn), lambda i, j: (0, j)))
        args = (a, w, col_scale)
        kern = _mm_scaled_kernel
    return pl.pallas_call(
        kern,
        grid=(m // tm, n // tn),
        in_specs=in_specs,
        out_specs=pl.BlockSpec((tm, tn), lambda i, j: (i, j)),
        out_shape=jax.ShapeDtypeStruct((m, n), out_dtype),
        compiler_params=_cparams(("parallel", "arbitrary")),
        name="matmul",
    )(*args)


def _mm_nt_kernel(a_ref, wt_ref, o_ref):
    o_ref[...] = _dg(a_ref[...], wt_ref[...].astype(BF16), NT).astype(o_ref.dtype)


def _mm_nt_scaled_kernel(a_ref, wt_ref, s_ref, o_ref):
    acc = _dg(a_ref[...], wt_ref[...].astype(BF16), NT)
    o_ref[...] = (acc * s_ref[...]).astype(o_ref.dtype)


def matmul_nt(a, wt, out_dtype, row0, n, tm=1024, tn=512, col_scale=None):
    m, k = a.shape
    tm = min(tm, m)
    tn = min(tn, n)
    assert m % tm == 0 and n % tn == 0 and row0 % 8 == 0 and wt.shape[1] == k
    in_specs = [pl.BlockSpec((tm, k), lambda i, j: (i, 0)),
                pl.BlockSpec((pl.Element(tn), pl.Element(k)),
                             lambda i, j: (pl.multiple_of(row0 + j * tn, 8), 0))]
    args = (a, wt)
    kern = _mm_nt_kernel
    if col_scale is not None:
        in_specs.append(pl.BlockSpec((1, tn), lambda i, j: (0, j)))
        args = (a, wt, col_scale)
        kern = _mm_nt_scaled_kernel
    return pl.pallas_call(
        kern,
        grid=(m // tm, n // tn),
        in_specs=in_specs,
        out_specs=pl.BlockSpec((tm, tn), lambda i, j: (i, j)),
        out_shape=jax.ShapeDtypeStruct((m, n), out_dtype),
        compiler_params=_cparams(("parallel", "arbitrary")),
        name="matmul_nt",
    )(*args)


def matmul2(a1, a2, w, out_dtype, tm=1024, tn=1024):
    m, k1 = a1.shape
    k2 = a2.shape[1]
    n = w.shape[1]
    assert k1 == k2 and w.shape[0] == k1 + k2
    tm = min(tm, m)
    tn = min(tn, n)
    assert m % tm == 0 and n % tn == 0
    return pl.pallas_call(
        _mm2_kernel,
        grid=(m // tm, n // tn),
        in_specs=[pl.BlockSpec((tm, k1), lambda i, j: (i, 0)),
                  pl.BlockSpec((tm, k2), lambda i, j: (i, 0)),
                  pl.BlockSpec((k1, tn), lambda i, j: (0, j)),
                  pl.BlockSpec((k2, tn), lambda i, j: (1, j))],
        out_specs=pl.BlockSpec((tm, tn), lambda i, j: (i, j)),
        out_shape=jax.ShapeDtypeStruct((m, n), out_dtype),
        compiler_params=_cparams(("parallel", "arbitrary")),
        name="matmul2",
    )(a1, a2, w, w)


def _ffn_kernel(h_ref, wg_ref, wu_ref, wd_ref, o_ref, *, n_chunk):
    f = pl.program_id(1)
    h = h_ref[...]
    g = jnp.dot(h, wg_ref[...].astype(BF16), preferred_element_type=F32)
    u = jnp.dot(h, wu_ref[...].astype(BF16), preferred_element_type=F32)
    act = (g * jax.nn.sigmoid(g) * u).astype(BF16)
    d = o_ref.shape[1]

    def down(c):
        return jnp.dot(act, wd_ref[:, c:c + n_chunk].astype(BF16), preferred_element_type=F32)

    @pl.when(f == 0)
    def _():
        for c in range(0, d, n_chunk):
            o_ref[:, c:c + n_chunk] = down(c)

    @pl.when(f > 0)
    def _():
        for c in range(0, d, n_chunk):
            o_ref[:, c:c + n_chunk] += down(c)


def ffn(h, wg, wu, wd, tm=1024, tf=256, n_chunk=512):
    m, d = h.shape
    dff = wg.shape[1]
    tm = min(tm, m)
    tf = min(tf, dff)
    n_chunk = min(n_chunk, d)
    assert m % tm == 0 and dff % tf == 0 and d % n_chunk == 0
    return pl.pallas_call(
        functools.partial(_ffn_kernel, n_chunk=n_chunk),
        grid=(m // tm, dff // tf),
        in_specs=[pl.BlockSpec((tm, d), lambda i, f: (i, 0)),
                  pl.BlockSpec((d, tf), lambda i, f: (0, f)),
                  pl.BlockSpec((d, tf), lambda i, f: (0, f)),
                  pl.BlockSpec((tf, d), lambda i, f: (f, 0))],
        out_specs=pl.BlockSpec((tm, d), lambda i, f: (i, 0), pipeline_mode=pl.Buffered(1)),
        out_shape=jax.ShapeDtypeStruct((m, d), F32),
        compiler_params=_cparams(("parallel", "arbitrary"), vmem=FFN_VMEM_LIMIT),
        name="ffn",
    )(h, wg, wu, wd)


def _fgate_kernel(f_ref, b_ref, c_ref, *, blk):
    s = f_ref.shape[0]
    row = lax.broadcasted_iota(jnp.int32, (blk, blk), 0)
    col = lax.broadcasted_iota(jnp.int32, (blk, blk), 1)
    tri = jnp.where(row >= col, 1.0, 0.0).astype(BF16)
    carry = jnp.zeros((1, f_ref.shape[1]), F32)
    for r0 in range(0, s, blk):
        x = f_ref[r0:r0 + blk, :] + b_ref[...]
        logf = -_softplus(-x)
        c = _dot_exact_lhs(tri, logf) + carry
        c_ref[r0:r0 + blk, :] = c
        carry = c[blk - 1:blk, :]


def fgate_cumsum(small, f_bias_row, f_block, seq, blk=256):
    m = small.shape[0]
    blk = min(blk, seq)
    return pl.pallas_call(
        functools.partial(_fgate_kernel, blk=blk),
        grid=(m // seq,),
        in_specs=[pl.BlockSpec((seq, LANES), lambda b: (b, f_block)),
                  pl.BlockSpec((1, LANES), lambda b: (0, 0))],
        out_specs=pl.BlockSpec((seq, LANES), lambda b: (b, 0)),
        out_shape=jax.ShapeDtypeStruct((m, LANES), F32),
        compiler_params=_cparams(("parallel",)),
        name="fgate_cumsum",
    )(small, f_bias_row)


LOG2E = 1.4426950408889634
FOX_Q_SCALE = FOX_HEAD_DIM ** -0.5 * LOG2E


def _fox_kernel(q_ref, k_ref, v_ref, cq_ref, ck_ref, o_ref, s_scr, p_scr, m_scr, cq_scr,
                acc_scr, *, tq, tk, hg, strip):
    qi = pl.program_id(2)
    dh = FOX_HEAD_DIM
    hs = range(hg)
    nl = tk // dh
    m_scr[...] = jnp.full_like(m_scr, NEG_INF)
    acc_scr[...] = jnp.zeros_like(acc_scr)
    c_blk = cq_ref[...] * LOG2E
    lane_id = lax.broadcasted_iota(jnp.int32, (dh, dh), 0)
    for g in hs:
        pick = jnp.where(lane_id == pl.program_id(1) * hg + g, 1.0, 0.0).astype(BF16)
        cq_scr[g] = _dot_exact_rhs(c_blk, pick)
    ones = jnp.ones((tk, dh), BF16)
    col = lax.broadcasted_iota(jnp.int32, (strip, tk), 1)
    row = lax.broadcasted_iota(jnp.int32, (strip, tk), 0)

    def head_cols(g):
        return slice(g * dh, (g + 1) * dh)

    def lane_tile(x, n):
        return x if n == 1 else jnp.concatenate([x] * n, axis=1)

    def block(ks, r0, mask_off):
        for g in hs:
            s_scr[g, r0:, :] = _dg(q_ref[r0:, head_cols(g)],
                                   k_ref[pl.ds(ks, tk), head_cols(g)], NT)
        for g in hs:
            ckg = ck_ref[g, :, pl.ds(ks, tk)] * LOG2E
            for r in range(r0, tq, strip):
                rows = slice(r, r + strip)
                s = s_scr[g, rows, :] - ckg
                if mask_off is not None and r - mask_off < tk - 1:
                    s = jnp.where(row + (r - mask_off) >= col, s, NEG_INF)
                cqr = cq_scr[g, rows, :]
                m_prev = m_scr[g, rows, :]
                m_new = jnp.maximum(m_prev, jnp.max(s, axis=-1, keepdims=True) + cqr)
                p_scr[g, rows, :] = jnp.exp2(s + lane_tile(cqr - m_new, nl)).astype(BF16)
                m_scr[g, rows, :] = m_new
                acc_scr[g, rows, :] = lane_tile(jnp.exp2(m_prev - m_new), 2) * acc_scr[g, rows, :]
        for g in hs:
            v_aug = jnp.concatenate([v_ref[pl.ds(ks, tk), head_cols(g)], ones], axis=1)
            acc_scr[g, r0:, :] += _dg(p_scr[g, r0:, :], v_aug, NN)

    def body(ki, carry):
        block(pl.multiple_of(ki * tk, tk), 0, None)
        return carry

    n_sub = tq // tk
    lax.fori_loop(0, qi * n_sub, body, 0)
    for j in range(n_sub):
        block(pl.multiple_of(qi * tq + j * tk, tk), j * tk, j * tk)
    for g in hs:
        acc = acc_scr[g]
        o_ref[:, head_cols(g)] = (acc[:, :dh] / acc[:, dh:]).astype(o_ref.dtype)


def fox_attention(qkv, c_tok, c_row, batch, seq, heads, tq=512, tk=256, hg=4, strip=64):
    tq = min(tq, seq)
    tk = min(tk, tq)
    strip = min(strip, tk)
    nq = seq // tq
    dh = FOX_HEAD_DIM
    ng = heads // hg
    assert heads % hg == 0 and tq % tk == 0 and tk % strip == 0 and seq % tq == 0
    return pl.pallas_call(
        functools.partial(_fox_kernel, tq=tq, tk=tk, hg=hg, strip=strip),
        grid=(batch, ng, nq),
        in_specs=[
            pl.BlockSpec((tq, hg * dh), lambda b, h, qi: (b * nq + qi, h)),
            pl.BlockSpec((seq, hg * dh), lambda b, h, qi: (b, ng + h)),
            pl.BlockSpec((seq, hg * dh), lambda b, h, qi: (b, 2 * ng + h)),
            pl.BlockSpec((tq, LANES), lambda b, h, qi: (b * nq + qi, 0)),
            pl.BlockSpec((None, hg, 1, seq), lambda b, h, qi: (b, h, 0, 0)),
        ],
        out_specs=pl.BlockSpec((tq, hg * dh), lambda b, h, qi: (b * nq + qi, h)),
        out_shape=jax.ShapeDtypeStruct((batch * seq, heads * dh), BF16),
        scratch_shapes=[pltpu.VMEM((hg, tq, tk), F32), pltpu.VMEM((hg, tq, tk), BF16),
                        pltpu.VMEM((hg, tq, dh), F32), pltpu.VMEM((hg, tq, dh), F32),
                        pltpu.VMEM((hg, tq, 2 * dh), F32)],
        compiler_params=_cparams(("parallel", "parallel", "arbitrary")),
        name="fox_attention",
    )(qkv, qkv, qkv, c_tok, c_row)


def _rwkv_prep_kernel(r_ref, k_ref, v_ref, rp_ref, kp_ref, vp_ref, sm_ref, smp_ref,
                      mu_r_ref, mu_k_ref, mu_v_ref, mu_s_ref,
                      w0_ref, wup_ref, a0_ref, aup_ref, gup_ref, kk_ref, ka_ref,
                      r_o, lw_o, k_o, v_o, kk_o, b_o, g_o, *, tiles_per_seq):
    i = pl.program_id(0)
    first = (i % tiles_per_seq) == 0
    tm = r_ref.shape[0]
    row0 = lax.broadcasted_iota(jnp.int32, (tm, 1), 0) == 0

    def mix(cur_ref, prev_ref, mu_ref):
        cur = cur_ref[...]
        last = jnp.where(first, 0.0, prev_ref[7:8, :])
        prev = jnp.where(row0, last, pltpu.roll(cur, 1, axis=0))
        return cur + mu_ref[...] * (prev - cur)

    zr = mix(r_ref, rp_ref, mu_r_ref)
    zk = mix(k_ref, kp_ref, mu_k_ref)
    zv = mix(v_ref, vp_ref, mu_v_ref)
    zs = mix(sm_ref, smp_ref, mu_s_ref)
    zw = zs[:, 0:LANES]
    za = zs[:, LANES:2 * LANES]
    zg = zs[:, 2 * LANES:2 * LANES + GATE_LORA]

    def lora(z, w_ref):
        return _dg(z.astype(BF16), w_ref[...].astype(BF16), NN)

    lw_o[...] = -EXP_NEG_HALF * jax.nn.sigmoid(w0_ref[...] + lora(jnp.tanh(zw), wup_ref))
    a = jax.nn.sigmoid(a0_ref[...] + lora(za, aup_ref))
    g_o[...] = lora(jax.nn.sigmoid(zg), gup_ref)

    tc = zk.shape[1]
    hr = lax.broadcasted_iota(jnp.int32, (LANES, LANES), 0) // RWKV_HEAD_DIM
    hc = lax.broadcasted_iota(jnp.int32, (LANES, LANES), 1) // RWKV_HEAD_DIM
    head_ones = jnp.where(hr == hc, 1.0, 0.0).astype(BF16)
    kk = zk * kk_ref[...]
    sq = kk * kk
    ss = jnp.concatenate([_head_sum(sq[:, j:j + LANES], head_ones)
                          for j in range(0, tc, LANES)], axis=1)
    kk = kk * lax.rsqrt(jnp.maximum(ss, 1e-24))
    r_o[...] = zr
    v_o[...] = zv
    kk_o[...] = kk
    b_o[...] = kk * a
    k_o[...] = zk * (1.0 + (a - 1.0) * ka_ref[...])


def rwkv_prep(rkv, small, mu_r, mu_k, mu_v, mu_s, w0, w_up, a0, a_up, g_up, k_k, k_a,
              seq, tm=512, tc=512):
    m = rkv.shape[0]
    width = RWKV_WIDTH
    tm = min(tm, seq)
    nc = width // tc
    ns = small.shape[1]
    tps = seq // tm
    sub = tm // 8

    def cur(piece):
        return pl.BlockSpec((tm, tc), lambda i, j: (i, piece * nc + j))

    def prev(piece):
        return pl.BlockSpec((8, tc), lambda i, j: (jnp.maximum(i * sub - 1, 0), piece * nc + j))

    colv = pl.BlockSpec((1, tc), lambda i, j: (0, j))
    out = pl.BlockSpec((tm, tc), lambda i, j: (i, j))
    outs = [jax.ShapeDtypeStruct((m, width), F32)] * 7
    return pl.pallas_call(
        functools.partial(_rwkv_prep_kernel, tiles_per_seq=tps),
        grid=(m // tm, nc),
        in_specs=[cur(0), cur(1), cur(2), prev(0), prev(1), prev(2),
                  pl.BlockSpec((tm, ns), lambda i, j: (i, 0)),
                  pl.BlockSpec((8, ns), lambda i, j: (jnp.maximum(i * sub - 1, 0), 0)),
                  colv, colv, colv,
                  pl.BlockSpec((1, ns), lambda i, j: (0, 0)),
                  colv,
                  pl.BlockSpec((LANES, tc), lambda i, j: (0, j)),
                  colv,
                  pl.BlockSpec((LANES, tc), lambda i, j: (0, j)),
                  pl.BlockSpec((GATE_LORA, tc), lambda i, j: (0, j)),
                  colv, colv],
        out_specs=[out] * 7,
        out_shape=outs,
        compiler_params=_cparams(("parallel", "arbitrary")),
        name="rwkv_prep",
    )(rkv, rkv, rkv, rkv, rkv, rkv, small, small,
      mu_r, mu_k, mu_v, mu_s, w0, w_up, a0, a_up, g_up, k_k, k_a)


def _rwkv_kernel(r_ref, lw_ref, k_ref, v_ref, kk_ref, b_ref, g_ref,
                 lnw_ref, lnb_ref, rk_ref, o_ref, s_ref, *, pairs):
    ci = pl.program_id(2)
    c = CHUNK
    hd = RWKV_HEAD_DIM

    @pl.when(ci == 0)
    def _():
        s_ref[...] = jnp.zeros_like(s_ref)

    r2 = lax.broadcasted_iota(jnp.int32, (LANES, LANES), 0)
    c2 = lax.broadcasted_iota(jnp.int32, (LANES, LANES), 1)
    same = (r2 < c) == (c2 < c)
    strict = jnp.logical_and(same, r2 > c2)
    incl = jnp.logical_and(same, r2 >= c2)
    head_ones = jnp.where((r2 < hd) == (c2 < hd), 1.0, 0.0).astype(BF16)
    rc = lax.broadcasted_iota(jnp.int32, (c, c), 0)
    cc = lax.broadcasted_iota(jnp.int32, (c, c), 1)
    tri_incl = jnp.where(rc >= cc, 1.0, 0.0).astype(BF16)
    m_a = lax.broadcasted_iota(jnp.int32, (1, LANES), 1) < hd
    inv_n = 1.0 / hd
    prs = range(pairs)
    cat = lambda xs: jnp.concatenate(xs, axis=0)

    def lanes(x, p):
        return x[:, p * LANES:(p + 1) * LANES]

    def stack2(x):
        return cat([jnp.where(m_a, x, 0.0), jnp.where(m_a, 0.0, x)])

    lw = lw_ref[...]
    r, k, v = r_ref[...], k_ref[...], v_ref[...]
    kk, beta = kk_ref[...], b_ref[...]
    big_l = _dot_exact_lhs(tri_incl, lw)
    l_tot = big_l[c - 1:c, :]
    e_nl = jnp.exp(-big_l)
    e_rem = jnp.exp(l_tot - big_l)
    e_tot = jnp.exp(l_tot)
    at = -kk * jnp.exp(big_l - lw)
    rt = r * jnp.exp(big_l)
    kt = k * e_nl
    bt = beta * e_nl
    bend = beta * e_rem
    kend = k * e_rem

    s0 = [s_ref[p] for p in prs]
    lhs = [cat([stack2(lanes(at, p)), stack2(lanes(rt, p))]).astype(BF16) for p in prs]
    wbk = [cat([lanes(bt, p), lanes(kt, p)]).astype(BF16) for p in prs]
    v2 = [stack2(lanes(v, p)).astype(BF16) for p in prs]
    m = [_dg(lhs[p], wbk[p], NT) for p in prs]
    sx = [_dg(lhs[p], s0[p].astype(BF16), NT) for p in prs]

    a_ab, a_kr, a_rb = [], [], []
    for p in prs:
        mp = m[p]
        sw = pltpu.roll(mp, c, axis=1)
        a_ab.append(jnp.where(strict, cat([mp[0:c], sw[c:2 * c]]), 0.0))
        a_ak = jnp.where(strict, cat([sw[0:c], mp[c:2 * c]]), 0.0)
        a_rb.append(jnp.where(incl, cat([mp[2 * c:3 * c], sw[3 * c:4 * c]]), 0.0))
        a_rk = jnp.where(incl, cat([sw[2 * c:3 * c], mp[3 * c:4 * c]]), 0.0)
        a_kr.append(cat([a_ak, a_rk]).astype(BF16))
    xy = [_dg(a_kr[p], v2[p], NN) for p in prs]
    u = [sx[p][:2 * c] + xy[p][:2 * c] for p in prs]
    ypart = [sx[p][2 * c:] + xy[p][2 * c:] for p in prs]

    def d1(a, b):
        return _dg(a.astype(BF16), b.astype(BF16), NN)

    blk = lambda n: (r2 // n) == (c2 // n)
    eye = jnp.where(r2 == c2, 1.0, 0.0)
    base = 8
    x_in = u
    p1 = [jnp.where(blk(base), a_ab[p], 0.0) for p in prs]
    p2 = [d1(p1[p], p1[p]) for p in prs]
    ip = [eye + p1[p] for p in prs]
    p4 = [d1(p2[p], p2[p]) for p in prs]
    m1 = [ip[p] + d1(ip[p], p2[p]) for p in prs]
    t = [m1[p] + d1(m1[p], p4[p]) for p in prs]
    n = base
    while n < c:
        quad = jnp.logical_and(blk(2 * n), jnp.logical_not(blk(n)))
        et = [d1(jnp.where(quad, a_ab[p], 0.0), t[p]) for p in prs]
        t = [t[p] + d1(t[p], et[p]) for p in prs]
        n *= 2
    u1 = [d1(t[p], x_in[p]) for p in prs]
    res = [x_in[p] - u1[p] + _dot3(a_ab[p], u1[p]) for p in prs]
    u = [u1[p] + d1(t[p], res[p]) for p in prs]

    ub = [u[p].astype(BF16) for p in prs]
    y2 = [ypart[p] + _dg(a_rb[p].astype(BF16), ub[p], NN) for p in prs]
    for p in prs:
        ends = cat([stack2(lanes(bend, p)), stack2(lanes(kend, p))]).astype(BF16)
        s_ref[p] = s0[p] * lanes(e_tot, p) + _dg(cat([ub[p], v2[p]]), ends, TN)

    y_all = cat([y2[p][:c] + y2[p][c:] for p in prs])
    mean = _head_sum(y_all, head_ones) * inv_n
    yc = y_all - mean
    var = _head_sum(yc * yc, head_ones) * inv_n
    yn = yc * lax.rsqrt(var + RWKV_GN_EPS)
    rkk = r * k * rk_ref[...]
    bonus = _head_sum(cat([lanes(rkk, p) for p in prs]), head_ones)
    lnw, lnb, g = lnw_ref[...], lnb_ref[...], g_ref[...]
    for p in prs:
        rows = slice(p * c, (p + 1) * c)
        out = (yn[rows] * lanes(lnw, p) + lanes(lnb, p) + bonus[rows] * lanes(v, p)) * lanes(g, p)
        o_ref[:, p * LANES:(p + 1) * LANES] = out.astype(o_ref.dtype)


def rwkv_recurrence(r, lw, k, v, kk, beta, g, ln_w, ln_b, r_k, batch, seq, pairs=16):
    m, width = r.shape
    c = CHUNK
    assert seq % c == 0 and 2 * c == LANES and 2 * RWKV_HEAD_DIM == LANES
    tc = pairs * LANES
    nchunk = seq // c
    blk = pl.BlockSpec((c, tc), lambda b, j, ci: (b * nchunk + ci, j))
    vec = pl.BlockSpec((1, tc), lambda b, j, ci: (0, j))
    return pl.pallas_call(
        functools.partial(_rwkv_kernel, pairs=pairs),
        grid=(batch, width // tc, nchunk),
        in_specs=[blk] * 7 + [vec] * 3,
        out_specs=blk,
        out_shape=jax.ShapeDtypeStruct((m, width), BF16),
        scratch_shapes=[pltpu.VMEM((pairs, LANES, LANES), F32)],
        compiler_params=_cparams(("parallel", "parallel", "arbitrary")),
        name="rwkv_recurrence",
    )(r, lw, k, v, kk, beta, g, ln_w, ln_b, r_k)


def _xattn_kernel(q_ref, k_ref, v_ref, o_ref, *, scale):
    s = _dg(q_ref[...], k_ref[...], NT) * scale
    m = jnp.max(s, axis=-1, keepdims=True)
    p = jnp.exp(s - m)
    l = jnp.sum(p, axis=-1, keepdims=True)
    o = jnp.dot(p.astype(BF16), v_ref[...], preferred_element_type=F32)
    o_ref[...] = (o / l).astype(o_ref.dtype)


def cross_attention(q, k, v, batch, seq, n_mem, heads, tq=2048):
    m, d = q.shape
    dh = d // heads
    tq = min(tq, seq)
    nq = seq // tq
    return pl.pallas_call(
        functools.partial(_xattn_kernel, scale=dh ** -0.5),
        grid=(batch, heads, nq),
        in_specs=[pl.BlockSpec((tq, dh), lambda b, h, qi: (b * nq + qi, h)),
                  pl.BlockSpec((n_mem, dh), lambda b, h, qi: (b, h)),
                  pl.BlockSpec((n_mem, dh), lambda b, h, qi: (b, h))],
        out_specs=pl.BlockSpec((tq, dh), lambda b, h, qi: (b * nq + qi, h)),
        out_shape=jax.ShapeDtypeStruct((m, d), BF16),
        compiler_params=_cparams(("parallel", "parallel", "arbitrary")),
        name="cross_attention",
    )(q, k, v)


def _pad_cols(w, n):
    return jnp.pad(w, ((0, 0), (0, n - w.shape[1])))


def _pad_rows(w, n):
    return jnp.pad(w, ((0, n - w.shape[0]), (0, 0)))


def _bf(w):
    return w.astype(BF16)


def _row(t):
    return t.reshape(1, -1).astype(F32)


def _mixer(h, p, batch, seq):
    fox_w, rw = FOX_WIDTH, RWKV_WIDTH
    fox_heads = fox_w // FOX_HEAD_DIM
    bf, row = _bf, _row

    w_in = p['w_in']
    o_f = 3 * fox_w
    o_r = o_f + fox_heads
    o_w = o_r + 3 * rw
    o_a = o_w + DECAY_LORA
    o_g = o_a + ICLR_LORA
    wt = w_in.T
    wt_small = jnp.concatenate([
        _pad_rows(wt[o_w:o_a], LANES), _pad_rows(wt[o_a:o_g], LANES),
        wt[o_g:o_g + GATE_LORA], _pad_rows(wt[o_f:o_r], LANES)], axis=0)
    mu = p['rwkv_mu']
    mu_small = jnp.concatenate([
        jnp.pad(mu[3 * rw:3 * rw + DECAY_LORA], (0, LANES - DECAY_LORA)),
        jnp.pad(mu[3 * rw + DECAY_LORA:3 * rw + DECAY_LORA + ICLR_LORA], (0, LANES - ICLR_LORA)),
        mu[3 * rw + DECAY_LORA + ICLR_LORA:], jnp.zeros((LANES,), F32)])

    q_scale = jnp.concatenate([jnp.full((1, fox_w), FOX_Q_SCALE, F32),
                               jnp.ones((1, 2 * fox_w), F32)], axis=1)
    qkv = matmul_nt(h, wt, BF16, 0, o_f, tn=PROJ_TN, col_scale=q_scale)
    rkv = matmul_nt(h, wt, F32, o_r, 3 * rw, tn=PROJ_TN)
    small = matmul_nt(h, wt_small, F32, 0, wt_small.shape[0], tn=wt_small.shape[0])

    f_bias = jnp.pad(p['fox_f_bias'], (0, LANES - fox_heads)).reshape(1, LANES)
    c = fgate_cumsum(small, f_bias, f_block=(2 * LANES + GATE_LORA) // LANES, seq=seq)
    c_row = c[:, :fox_heads].reshape(batch, seq, fox_heads).transpose(0, 2, 1)[:, :, None, :]
    y_fox = fox_attention(qkv, c, c_row, batch, seq, fox_heads)

    r, lw, k, v, kk, beta, g = rwkv_prep(
        rkv, small, row(mu[:rw]), row(mu[rw:2 * rw]), row(mu[2 * rw:3 * rw]), row(mu_small),
        row(p['rwkv_w0']), _pad_rows(p['rwkv_w_up'], LANES), row(p['rwkv_a0']),
        _pad_rows(p['rwkv_a_up'], LANES), p['rwkv_g_up'], row(p['rwkv_k_k']),
        row(p['rwkv_k_a']), seq=seq)
    y_rwkv = rwkv_recurrence(r, lw, k, v, kk, beta, g, row(p['rwkv_ln_w']),
                             row(p['rwkv_ln_b']), row(p['rwkv_r_k']), batch, seq)

    return matmul2(y_fox, y_rwkv, p['w_out'], BF16, tn=F32_W_TN)


def _xattn(h, mn, p, batch, seq, n_mem):
    q = matmul(h, p['xattn_wq'], BF16, tn=F32_W_TN)
    km = matmul(mn, p['xattn_wk'], BF16, tn=F32_W_TN)
    vm = matmul(mn, p['xattn_wv'], BF16, tn=F32_W_TN)
    o = cross_attention(q, km, vm, batch, seq, n_mem, XATTN_HEADS)
    return matmul(o, p['xattn_wo'], BF16, tn=F32_W_TN)


def _layer(x, mem, p, batch, seq, n_mem):
    h = rmsnorm(x, p['ffn1_pre_g'])
    y = ffn(h, p['ffn1_w_gate'], p['ffn1_w_up'], p['ffn1_w_down'])
    x, h = resnorm(x, y, p['ffn1_post_g'], 0.5, p['mix_pre_g'])

    mixed = _mixer(h, p, batch, seq)
    x, h = resnorm(x, mixed, p['mix_post_g'], 1.0, p['xattn_pre_g'])

    mn = rmsnorm(mem, p['mem_norm_g'])
    xa = _xattn(h, mn, p, batch, seq, n_mem)
    x, h = resnorm(x, xa, p['xattn_post_g'], 1.0, p['ffn2_pre_g'])

    y = ffn(h, p['ffn2_w_gate'], p['ffn2_w_up'], p['ffn2_w_down'])
    return resnorm(x, y, p['ffn2_post_g'], 0.5)


_PARAM_NAMES = (
    'ffn1_pre_g', 'ffn1_w_gate', 'ffn1_w_up', 'ffn1_w_down', 'ffn1_post_g',
    'mix_pre_g', 'w_in', 'fox_f_bias', 'rwkv_mu', 'rwkv_w0', 'rwkv_w_up', 'rwkv_a0',
    'rwkv_a_up', 'rwkv_g_up', 'rwkv_k_k', 'rwkv_k_a', 'rwkv_r_k', 'rwkv_ln_w', 'rwkv_ln_b',
    'w_out', 'mix_post_g', 'xattn_pre_g', 'mem_norm_g', 'xattn_wq', 'xattn_wk', 'xattn_wv',
    'xattn_wo', 'xattn_post_g', 'ffn2_pre_g', 'ffn2_w_gate', 'ffn2_w_up', 'ffn2_w_down',
    'ffn2_post_g')


def kernel(x, mem, ffn1_pre_g, ffn1_w_gate, ffn1_w_up, ffn1_w_down, ffn1_post_g, mix_pre_g, w_in, fox_f_bias, rwkv_mu, rwkv_w0, rwkv_w_up, rwkv_a0, rwkv_a_up, rwkv_g_up, rwkv_k_k, rwkv_k_a, rwkv_r_k, rwkv_ln_w, rwkv_ln_b, w_out, mix_post_g, xattn_pre_g, mem_norm_g, xattn_wq, xattn_wk, xattn_wv, xattn_wo, xattn_post_g, ffn2_pre_g, ffn2_w_gate, ffn2_w_up, ffn2_w_down, ffn2_post_g):
    weights = (ffn1_pre_g, ffn1_w_gate, ffn1_w_up, ffn1_w_down, ffn1_post_g, mix_pre_g, w_in,
               fox_f_bias, rwkv_mu, rwkv_w0, rwkv_w_up, rwkv_a0, rwkv_a_up, rwkv_g_up, rwkv_k_k,
               rwkv_k_a, rwkv_r_k, rwkv_ln_w, rwkv_ln_b, w_out, mix_post_g, xattn_pre_g,
               mem_norm_g, xattn_wq, xattn_wk, xattn_wv, xattn_wo, xattn_post_g, ffn2_pre_g,
               ffn2_w_gate, ffn2_w_up, ffn2_w_down, ffn2_post_g)
    batch, seq, d = x.shape
    n_mem = mem.shape[1]
    depth = ffn1_pre_g.shape[0]
    xf = x.reshape(batch * seq, d)
    mf = mem.reshape(batch * n_mem, d)
    for l in range(depth):
        p = {name: w[l] for name, w in zip(_PARAM_NAMES, weights)}
        xf = _layer(xf, mf, p, batch, seq, n_mem)
    return xf.reshape(batch, seq, d)
```

```python
import functools

import jax
import jax.numpy as jnp
from jax import lax
from jax.experimental import pallas as pl
from jax.experimental.pallas import tpu as pltpu

F32 = jnp.float32
BF16 = jnp.bfloat16

SEQ = 2048
FOX_HEAD_DIM = 128
RWKV_HEAD_DIM = 64
RWKV_WIDTH = 2048
FOX_WIDTH = 2048
DECAY_LORA = 96
ICLR_LORA = 96
GATE_LORA = 256
XATTN_HEADS = 4
RMS_EPS = 1e-6
RWKV_GN_EPS = 64e-5
NEG_INF = -1e30

EXP_NEG_HALF = 0.6065306597126334
LANES = 128
CHUNK = 64
VMEM_LIMIT = 56 * 1024 * 1024
FFN_VMEM_LIMIT = 62 * 1024 * 1024
F32_W_TN = 512
PROJ_TN = 768


def _cparams(sem, vmem=VMEM_LIMIT):
    return pltpu.CompilerParams(dimension_semantics=sem, vmem_limit_bytes=vmem)


NN = (((1,), (0,)), ((), ()))
NT = (((1,), (1,)), ((), ()))
TN = (((0,), (0,)), ((), ()))


def _split2(x):
    hi = x.astype(BF16)
    lo = (x - hi.astype(F32)).astype(BF16)
    return hi, lo


def _split3(x):
    hi = x.astype(BF16)
    r = x - hi.astype(F32)
    mid = r.astype(BF16)
    lo = (r - mid.astype(F32)).astype(BF16)
    return hi, mid, lo


def _dg(a, b, dims):
    return lax.dot_general(a, b, dims, preferred_element_type=F32)


def _dot3(a, b, dims=NN):
    ah, al = _split2(a)
    bh, bl = _split2(b)
    return _dg(ah, bh, dims) + (_dg(ah, bl, dims) + _dg(al, bh, dims))


def _dot_exact_lhs(a_bf16, b, dims=NN):
    b0, b1, b2 = _split3(b)
    return _dg(a_bf16, b0, dims) + (_dg(a_bf16, b1, dims) + _dg(a_bf16, b2, dims))


def _dot_exact_rhs(a, b_bf16, dims=NN):
    a0, a1, a2 = _split3(a)
    return _dg(a0, b_bf16, dims) + (_dg(a1, b_bf16, dims) + _dg(a2, b_bf16, dims))


def _head_sum(a, ones_bf16):
    a0, a1 = _split2(a)
    return _dg(a0, ones_bf16, NN) + _dg(a1, ones_bf16, NN)


def _softplus(x):
    return jnp.maximum(x, 0.0) + jnp.log1p(jnp.exp(-jnp.abs(x)))


def _rmsnorm_kernel(x_ref, g_ref, o_ref):
    x = x_ref[...]
    ms = jnp.mean(x * x, axis=-1, keepdims=True)
    o_ref[...] = (x * lax.rsqrt(ms + RMS_EPS) * g_ref[...]).astype(o_ref.dtype)


def rmsnorm(x, g, tm=256):
    m, d = x.shape
    tm = min(tm, m)
    return pl.pallas_call(
        _rmsnorm_kernel,
        grid=(m // tm,),
        in_specs=[pl.BlockSpec((tm, d), lambda i: (i, 0)),
                  pl.BlockSpec((1, d), lambda i: (0, 0))],
        out_specs=pl.BlockSpec((tm, d), lambda i: (i, 0)),
        out_shape=jax.ShapeDtypeStruct((m, d), BF16),
        compiler_params=_cparams(("parallel",)),
        name="rmsnorm",
    )(x, g.reshape(1, d))


def _resnorm_kernel(x_ref, y_ref, gp_ref, gn_ref, xo_ref, h_ref, *, scale):
    y = y_ref[...].astype(F32)
    ms = jnp.mean(y * y, axis=-1, keepdims=True)
    xn = x_ref[...] + scale * (y * lax.rsqrt(ms + RMS_EPS) * gp_ref[...])
    xo_ref[...] = xn
    ms2 = jnp.mean(xn * xn, axis=-1, keepdims=True)
    h_ref[...] = (xn * lax.rsqrt(ms2 + RMS_EPS) * gn_ref[...]).astype(h_ref.dtype)


def _resnorm_last_kernel(x_ref, y_ref, gp_ref, xo_ref, *, scale):
    y = y_ref[...]
    ms = jnp.mean(y * y, axis=-1, keepdims=True)
    xo_ref[...] = x_ref[...] + scale * (y * lax.rsqrt(ms + RMS_EPS) * gp_ref[...])


def resnorm(x, y, g_post, scale, g_next=None, tm=256):
    m, d = x.shape
    row = pl.BlockSpec((tm, d), lambda i: (i, 0))
    vec = pl.BlockSpec((1, d), lambda i: (0, 0))
    if g_next is None:
        return pl.pallas_call(
            functools.partial(_resnorm_last_kernel, scale=scale),
            grid=(m // tm,),
            in_specs=[row, row, vec],
            out_specs=row,
            out_shape=jax.ShapeDtypeStruct((m, d), F32),
            compiler_params=_cparams(("parallel",)),
            name="resnorm_last",
        )(x, y, g_post.reshape(1, d))
    return pl.pallas_call(
        functools.partial(_resnorm_kernel, scale=scale),
        grid=(m // tm,),
        in_specs=[row, row, vec, vec],
        out_specs=[row, row],
        out_shape=[jax.ShapeDtypeStruct((m, d), F32),
                   jax.ShapeDtypeStruct((m, d), BF16)],
        compiler_params=_cparams(("parallel",)),
        name="resnorm",
    )(x, y, g_post.reshape(1, d), g_next.reshape(1, d))


def _mm_kernel(a_ref, w_ref, o_ref):
    o_ref[...] = jnp.dot(a_ref[...], w_ref[...].astype(BF16),
                         preferred_element_type=F32).astype(o_ref.dtype)


def _mm_scaled_kernel(a_ref, w_ref, s_ref, o_ref):
    acc = jnp.dot(a_ref[...], w_ref[...].astype(BF16), preferred_element_type=F32)
    o_ref[...] = (acc * s_ref[...]).astype(o_ref.dtype)


def _mm2_kernel(a1_ref, a2_ref, w1_ref, w2_ref, o_ref):
    acc = jnp.dot(a1_ref[...], w1_ref[...].astype(BF16), preferred_element_type=F32)
    acc += jnp.dot(a2_ref[...], w2_ref[...].astype(BF16), preferred_element_type=F32)
    o_ref[...] = acc.astype(o_ref.dtype)


def matmul(a, w, out_dtype, tm=1024, tn=1024, col_scale=None, n=None):
    m, k = a.shape
    n = w.shape[1] if n is None else n
    tm = min(tm, m)
    tn = min(tn, n)
    assert m % tm == 0 and n % tn == 0
    in_specs = [pl.BlockSpec((tm, k), lambda i, j: (i, 0)),
                pl.BlockSpec((k, tn), lambda i, j: (0, j))]
    args = (a, w)
    kern = _mm_kernel
    if col_scale is not None:
        in_specs.append(pl.BlockSpec((1, tn), lambda i, j: (0, j)))
        args = (a, w, col_scale)
        kern = _mm_scaled_kernel
    return pl.pallas_call(
        kern,
        grid=(m // tm, n // tn),
        in_specs=in_specs,
        out_specs=pl.BlockSpec((tm, tn), lambda i, j: (i, j)),
        out_shape=jax.ShapeDtypeStruct((m, n), out_dtype),
        compiler_params=_cparams(("parallel", "arbitrary")),
        name="matmul",
    )(*args)


def _mm_nt_kernel(a_ref, wt_ref, o_ref):
    o_ref[...] = _dg(a_ref[...], wt_ref[...].astype(BF16), NT).astype(o_ref.dtype)


def _mm_nt_scaled_kernel(a_ref, wt_ref, s_ref, o_ref):
    acc = _dg(a_ref[...], wt_ref[...].astype(BF16), NT)
    o_ref[...] = (acc * s_ref[...]).astype(o_ref.dtype)


def matmul_nt(a, wt, out_dtype, row0, n, tm=1024, tn=512, col_scale=None):
    m, k = a.shape
    tm = min(tm, m)
    tn = min(tn, n)
    assert m % tm == 0 and n % tn == 0 and row0 % 8 == 0 and wt.shape[1] == k
    in_specs = [pl.BlockSpec((tm, k), lambda i, j: (i, 0)),
                pl.BlockSpec((pl.Element(tn), pl.Element(k)),
                             lambda i, j: (pl.multiple_of(row0 + j * tn, 8), 0))]
    args = (a, wt)
    kern = _mm_nt_kernel
    if col_scale is not None:
        in_specs.append(pl.BlockSpec((1, tn), lambda i, j: (0, j)))
        args = (a, wt, col_scale)
        kern = _mm_nt_scaled_kernel
    return pl.pallas_call(
        kern,
        grid=(m // tm, n // tn),
        in_specs=in_specs,
        out_specs=pl.BlockSpec((tm, tn), lambda i, j: (i, j)),
        out_shape=jax.ShapeDtypeStruct((m, n), out_dtype),
        compiler_params=_cparams(("parallel", "arbitrary")),
        name="matmul_nt",
    )(*args)


def matmul2(a1, a2, w, out_dtype, tm=1024, tn=1024):
    m, k1 = a1.shape
    k2 = a2.shape[1]
    n = w.shape[1]
    assert k1 == k2 and w.shape[0] == k1 + k2
    tm = min(tm, m)
    tn = min(tn, n)
    assert m % tm == 0 and n % tn == 0
    return pl.pallas_call(
        _mm2_kernel,
        grid=(m // tm, n // tn),
        in_specs=[pl.BlockSpec((tm, k1), lambda i, j: (i, 0)),
                  pl.BlockSpec((tm, k2), lambda i, j: (i, 0)),
                  pl.BlockSpec((k1, tn), lambda i, j: (0, j)),
                  pl.BlockSpec((k2, tn), lambda i, j: (1, j))],
        out_specs=pl.BlockSpec((tm, tn), lambda i, j: (i, j)),
        out_shape=jax.ShapeDtypeStruct((m, n), out_dtype),
        compiler_params=_cparams(("parallel", "arbitrary")),
        name="matmul2",
    )(a1, a2, w, w)


def _ffn_kernel(h_ref, wg_ref, wu_ref, wd_ref, o_ref, *, n_chunk):
    f = pl.program_id(1)

    @pl.when(f == 0)
    def _():
        o_ref[...] = jnp.zeros_like(o_ref)

    h = h_ref[...]
    g = jnp.dot(h, wg_ref[...].astype(BF16), preferred_element_type=F32)
    u = jnp.dot(h, wu_ref[...].astype(BF16), preferred_element_type=F32)
    act = (g * jax.nn.sigmoid(g) * u).astype(BF16)
    d = o_ref.shape[1]
    for c in range(0, d, n_chunk):
        o_ref[:, c:c + n_chunk] += jnp.dot(act, wd_ref[:, c:c + n_chunk].astype(BF16),
                                           preferred_element_type=F32)


def ffn(h, wg, wu, wd, tm=1024, tf=256, n_chunk=512):
    m, d = h.shape
    dff = wg.shape[1]
    tm = min(tm, m)
    tf = min(tf, dff)
    n_chunk = min(n_chunk, d)
    assert m % tm == 0 and dff % tf == 0 and d % n_chunk == 0
    return pl.pallas_call(
        functools.partial(_ffn_kernel, n_chunk=n_chunk),
        grid=(m // tm, dff // tf),
        in_specs=[pl.BlockSpec((tm, d), lambda i, f: (i, 0)),
                  pl.BlockSpec((d, tf), lambda i, f: (0, f)),
                  pl.BlockSpec((d, tf), lambda i, f: (0, f)),
                  pl.BlockSpec((tf, d), lambda i, f: (f, 0))],
        out_specs=pl.BlockSpec((tm, d), lambda i, f: (i, 0), pipeline_mode=pl.Buffered(1)),
        out_shape=jax.ShapeDtypeStruct((m, d), F32),
        compiler_params=_cparams(("parallel", "arbitrary"), vmem=FFN_VMEM_LIMIT),
        name="ffn",
    )(h, wg, wu, wd)


def _fgate_kernel(f_ref, b_ref, c_ref, *, blk):
    s = f_ref.shape[0]
    row = lax.broadcasted_iota(jnp.int32, (blk, blk), 0)
    col = lax.broadcasted_iota(jnp.int32, (blk, blk), 1)
    tri = jnp.where(row >= col, 1.0, 0.0).astype(BF16)
    carry = jnp.zeros((1, f_ref.shape[1]), F32)
    for r0 in range(0, s, blk):
        x = f_ref[r0:r0 + blk, :] + b_ref[...]
        logf = -_softplus(-x)
        c = _dot_exact_lhs(tri, logf) + carry
        c_ref[r0:r0 + blk, :] = c
        carry = c[blk - 1:blk, :]


def fgate_cumsum(small, f_bias_row, f_block, seq, blk=256):
    m = small.shape[0]
    blk = min(blk, seq)
    return pl.pallas_call(
        functools.partial(_fgate_kernel, blk=blk),
        grid=(m // seq,),
        in_specs=[pl.BlockSpec((seq, LANES), lambda b: (b, f_block)),
                  pl.BlockSpec((1, LANES), lambda b: (0, 0))],
        out_specs=pl.BlockSpec((seq, LANES), lambda b: (b, 0)),
        out_shape=jax.ShapeDtypeStruct((m, LANES), F32),
        compiler_params=_cparams(("parallel",)),
        name="fgate_cumsum",
    )(small, f_bias_row)


LOG2E = 1.4426950408889634
FOX_Q_SCALE = FOX_HEAD_DIM ** -0.5 * LOG2E


def _fox_kernel(q_ref, k_ref, v_ref, cq_ref, ck_ref, o_ref, s_scr, p_scr, m_scr, cq_scr,
                acc_scr, *, tq, tk, hg, strip):
    qi = pl.program_id(2)
    dh = FOX_HEAD_DIM
    hs = range(hg)
    nl = tk // dh
    m_scr[...] = jnp.full_like(m_scr, NEG_INF)
    acc_scr[...] = jnp.zeros_like(acc_scr)
    c_blk = cq_ref[...] * LOG2E
    lane_id = lax.broadcasted_iota(jnp.int32, (dh, dh), 0)
    for g in hs:
        pick = jnp.where(lane_id == pl.program_id(1) * hg + g, 1.0, 0.0).astype(BF16)
        cq_scr[g] = _dot_exact_rhs(c_blk, pick)
    ones = jnp.ones((tk, dh), BF16)
    col = lax.broadcasted_iota(jnp.int32, (strip, tk), 1)
    row = lax.broadcasted_iota(jnp.int32, (strip, tk), 0)

    def head_cols(g):
        return slice(g * dh, (g + 1) * dh)

    def lane_tile(x, n):
        return x if n == 1 else jnp.concatenate([x] * n, axis=1)

    def block(ks, r0, mask_off):
        for g in hs:
            s_scr[g, r0:, :] = _dg(q_ref[r0:, head_cols(g)],
                                   k_ref[pl.ds(ks, tk), head_cols(g)], NT)
        for g in hs:
            ckg = ck_ref[g, :, pl.ds(ks, tk)] * LOG2E
            for r in range(r0, tq, strip):
                rows = slice(r, r + strip)
                s = s_scr[g, rows, :] - ckg
                if mask_off is not None and r - mask_off < tk - 1:
                    s = jnp.where(row + (r - mask_off) >= col, s, NEG_INF)
                cqr = cq_scr[g, rows, :]
                m_prev = m_scr[g, rows, :]
                m_new = jnp.maximum(m_prev, jnp.max(s, axis=-1, keepdims=True) + cqr)
                p_scr[g, rows, :] = jnp.exp2(s + lane_tile(cqr - m_new, nl)).astype(BF16)
                m_scr[g, rows, :] = m_new
                acc_scr[g, rows, :] = lane_tile(jnp.exp2(m_prev - m_new), 2) * acc_scr[g, rows, :]
        for g in hs:
            v_aug = jnp.concatenate([v_ref[pl.ds(ks, tk), head_cols(g)], ones], axis=1)
            acc_scr[g, r0:, :] += _dg(p_scr[g, r0:, :], v_aug, NN)

    def body(ki, carry):
        block(pl.multiple_of(ki * tk, tk), 0, None)
        return carry

    n_sub = tq // tk
    lax.fori_loop(0, qi * n_sub, body, 0)
    for j in range(n_sub):
        block(pl.multiple_of(qi * tq + j * tk, tk), j * tk, j * tk)
    for g in hs:
        acc = acc_scr[g]
        o_ref[:, head_cols(g)] = (acc[:, :dh] / acc[:, dh:]).astype(o_ref.dtype)


def fox_attention(qkv, c_tok, c_row, batch, seq, heads, tq=512, tk=256, hg=4, strip=64):
    tq = min(tq, seq)
    tk = min(tk, tq)
    strip = min(strip, tk)
    nq = seq // tq
    dh = FOX_HEAD_DIM
    ng = heads // hg
    assert heads % hg == 0 and tq % tk == 0 and tk % strip == 0 and seq % tq == 0
    return pl.pallas_call(
        functools.partial(_fox_kernel, tq=tq, tk=tk, hg=hg, strip=strip),
        grid=(batch, ng, nq),
        in_specs=[
            pl.BlockSpec((tq, hg * dh), lambda b, h, qi: (b * nq + qi, h)),
            pl.BlockSpec((seq, hg * dh), lambda b, h, qi: (b, ng + h)),
            pl.BlockSpec((seq, hg * dh), lambda b, h, qi: (b, 2 * ng + h)),
            pl.BlockSpec((tq, LANES), lambda b, h, qi: (b * nq + qi, 0)),
            pl.BlockSpec((None, hg, 1, seq), lambda b, h, qi: (b, h, 0, 0)),
        ],
        out_specs=pl.BlockSpec((tq, hg * dh), lambda b, h, qi: (b * nq + qi, h)),
        out_shape=jax.ShapeDtypeStruct((batch * seq, heads * dh), BF16),
        scratch_shapes=[pltpu.VMEM((hg, tq, tk), F32), pltpu.VMEM((hg, tq, tk), BF16),
                        pltpu.VMEM((hg, tq, dh), F32), pltpu.VMEM((hg, tq, dh), F32),
                        pltpu.VMEM((hg, tq, 2 * dh), F32)],
        compiler_params=_cparams(("parallel", "parallel", "arbitrary")),
        name="fox_attention",
    )(qkv, qkv, qkv, c_tok, c_row)


def _rwkv_prep_kernel(r_ref, k_ref, v_ref, rp_ref, kp_ref, vp_ref, sm_ref, smp_ref,
                      mu_r_ref, mu_k_ref, mu_v_ref, mu_s_ref,
                      w0_ref, wup_ref, a0_ref, aup_ref, gup_ref, kk_ref, ka_ref,
                      r_o, lw_o, k_o, v_o, kk_o, b_o, g_o, *, tiles_per_seq):
    i = pl.program_id(0)
    first = (i % tiles_per_seq) == 0
    tm = r_ref.shape[0]
    row0 = lax.broadcasted_iota(jnp.int32, (tm, 1), 0) == 0

    def mix(cur_ref, prev_ref, mu_ref):
        cur = cur_ref[...]
        last = jnp.where(first, 0.0, prev_ref[7:8, :])
        prev = jnp.where(row0, last, pltpu.roll(cur, 1, axis=0))
        return cur + mu_ref[...] * (prev - cur)

    zr = mix(r_ref, rp_ref, mu_r_ref)
    zk = mix(k_ref, kp_ref, mu_k_ref)
    zv = mix(v_ref, vp_ref, mu_v_ref)
    zs = mix(sm_ref, smp_ref, mu_s_ref)
    zw = zs[:, 0:LANES]
    za = zs[:, LANES:2 * LANES]
    zg = zs[:, 2 * LANES:2 * LANES + GATE_LORA]

    def lora(z, w_ref):
        return _dg(z.astype(BF16), w_ref[...].astype(BF16), NN)

    lw_o[...] = -EXP_NEG_HALF * jax.nn.sigmoid(w0_ref[...] + lora(jnp.tanh(zw), wup_ref))
    a = jax.nn.sigmoid(a0_ref[...] + lora(za, aup_ref))
    g_o[...] = lora(jax.nn.sigmoid(zg), gup_ref)

    tc = zk.shape[1]
    hr = lax.broadcasted_iota(jnp.int32, (LANES, LANES), 0) // RWKV_HEAD_DIM
    hc = lax.broadcasted_iota(jnp.int32, (LANES, LANES), 1) // RWKV_HEAD_DIM
    head_ones = jnp.where(hr == hc, 1.0, 0.0).astype(BF16)
    kk = zk * kk_ref[...]
    sq = kk * kk
    ss = jnp.concatenate([_head_sum(sq[:, j:j + LANES], head_ones)
                          for j in range(0, tc, LANES)], axis=1)
    kk = kk * lax.rsqrt(jnp.maximum(ss, 1e-24))
    r_o[...] = zr
    v_o[...] = zv
    kk_o[...] = kk
    b_o[...] = kk * a
    k_o[...] = zk * (1.0 + (a - 1.0) * ka_ref[...])


def rwkv_prep(rkv, small, mu_r, mu_k, mu_v, mu_s, w0, w_up, a0, a_up, g_up, k_k, k_a,
              seq, tm=512, tc=512):
    m = rkv.shape[0]
    width = RWKV_WIDTH
    tm = min(tm, seq)
    nc = width // tc
    ns = small.shape[1]
    tps = seq // tm
    sub = tm // 8

    def cur(piece):
        return pl.BlockSpec((tm, tc), lambda i, j: (i, piece * nc + j))

    def prev(piece):
        return pl.BlockSpec((8, tc), lambda i, j: (jnp.maximum(i * sub - 1, 0), piece * nc + j))

    colv = pl.BlockSpec((1, tc), lambda i, j: (0, j))
    out = pl.BlockSpec((tm, tc), lambda i, j: (i, j))
    outs = [jax.ShapeDtypeStruct((m, width), F32)] * 7
    return pl.pallas_call(
        functools.partial(_rwkv_prep_kernel, tiles_per_seq=tps),
        grid=(m // tm, nc),
        in_specs=[cur(0), cur(1), cur(2), prev(0), prev(1), prev(2),
                  pl.BlockSpec((tm, ns), lambda i, j: (i, 0)),
                  pl.BlockSpec((8, ns), lambda i, j: (jnp.maximum(i * sub - 1, 0), 0)),
                  colv, colv, colv,
                  pl.BlockSpec((1, ns), lambda i, j: (0, 0)),
                  colv,
                  pl.BlockSpec((LANES, tc), lambda i, j: (0, j)),
                  colv,
                  pl.BlockSpec((LANES, tc), lambda i, j: (0, j)),
                  pl.BlockSpec((GATE_LORA, tc), lambda i, j: (0, j)),
                  colv, colv],
        out_specs=[out] * 7,
        out_shape=outs,
        compiler_params=_cparams(("parallel", "arbitrary")),
        name="rwkv_prep",
    )(rkv, rkv, rkv, rkv, rkv, rkv, small, small,
      mu_r, mu_k, mu_v, mu_s, w0, w_up, a0, a_up, g_up, k_k, k_a)


def _rwkv_kernel(r_ref, lw_ref, k_ref, v_ref, kk_ref, b_ref, g_ref,
                 lnw_ref, lnb_ref, rk_ref, o_ref, s_ref, *, pairs):
    ci = pl.program_id(2)
    c = CHUNK
    hd = RWKV_HEAD_DIM

    @pl.when(ci == 0)
    def _():
        s_ref[...] = jnp.zeros_like(s_ref)

    r2 = lax.broadcasted_iota(jnp.int32, (LANES, LANES), 0)
    c2 = lax.broadcasted_iota(jnp.int32, (LANES, LANES), 1)
    same = (r2 < c) == (c2 < c)
    strict = jnp.logical_and(same, r2 > c2)
    incl = jnp.logical_and(same, r2 >= c2)
    head_ones = jnp.where((r2 < hd) == (c2 < hd), 1.0, 0.0).astype(BF16)
    rc = lax.broadcasted_iota(jnp.int32, (c, c), 0)
    cc = lax.broadcasted_iota(jnp.int32, (c, c), 1)
    tri_incl = jnp.where(rc >= cc, 1.0, 0.0).astype(BF16)
    m_a = lax.broadcasted_iota(jnp.int32, (1, LANES), 1) < hd
    inv_n = 1.0 / hd
    prs = range(pairs)
    cat = lambda xs: jnp.concatenate(xs, axis=0)

    def lanes(x, p):
        return x[:, p * LANES:(p + 1) * LANES]

    def stack2(x):
        return cat([jnp.where(m_a, x, 0.0), jnp.where(m_a, 0.0, x)])

    lw = lw_ref[...]
    r, k, v = r_ref[...], k_ref[...], v_ref[...]
    kk, beta = kk_ref[...], b_ref[...]
    big_l = _dot_exact_lhs(tri_incl, lw)
    l_tot = big_l[c - 1:c, :]
    e_nl = jnp.exp(-big_l)
    e_rem = jnp.exp(l_tot - big_l)
    e_tot = jnp.exp(l_tot)
    at = -kk * jnp.exp(big_l - lw)
    rt = r * jnp.exp(big_l)
    kt = k * e_nl
    bt = beta * e_nl
    bend = beta * e_rem
    kend = k * e_rem

    s0 = [s_ref[p] for p in prs]
    lhs = [cat([stack2(lanes(at, p)), stack2(lanes(rt, p))]).astype(BF16) for p in prs]
    wbk = [cat([lanes(bt, p), lanes(kt, p)]).astype(BF16) for p in prs]
    v2 = [stack2(lanes(v, p)).astype(BF16) for p in prs]
    m = [_dg(lhs[p], wbk[p], NT) for p in prs]
    sx = [_dg(lhs[p], s0[p].astype(BF16), NT) for p in prs]

    a_ab, a_kr, a_rb = [], [], []
    for p in prs:
        mp = m[p]
        sw = pltpu.roll(mp, c, axis=1)
        a_ab.append(jnp.where(strict, cat([mp[0:c], sw[c:2 * c]]), 0.0))
        a_ak = jnp.where(strict, cat([sw[0:c], mp[c:2 * c]]), 0.0)
        a_rb.append(jnp.where(incl, cat([mp[2 * c:3 * c], sw[3 * c:4 * c]]), 0.0))
        a_rk = jnp.where(incl, cat([sw[2 * c:3 * c], mp[3 * c:4 * c]]), 0.0)
        a_kr.append(cat([a_ak, a_rk]).astype(BF16))
    xy = [_dg(a_kr[p], v2[p], NN) for p in prs]
    u = [sx[p][:2 * c] + xy[p][:2 * c] for p in prs]
    ypart = [sx[p][2 * c:] + xy[p][2 * c:] for p in prs]

    def d1(a, b):
        return _dg(a.astype(BF16), b.astype(BF16), NN)

    blk = lambda n: (r2 // n) == (c2 // n)
    eye = jnp.where(r2 == c2, 1.0, 0.0)
    base = 8
    x_in = u
    p1 = [jnp.where(blk(base), a_ab[p], 0.0) for p in prs]
    p2 = [d1(p1[p], p1[p]) for p in prs]
    ip = [eye + p1[p] for p in prs]
    p4 = [d1(p2[p], p2[p]) for p in prs]
    m1 = [ip[p] + d1(ip[p], p2[p]) for p in prs]
    t = [m1[p] + d1(m1[p], p4[p]) for p in prs]
    n = base
    while n < c:
        quad = jnp.logical_and(blk(2 * n), jnp.logical_not(blk(n)))
        et = [d1(jnp.where(quad, a_ab[p], 0.0), t[p]) for p in prs]
        t = [t[p] + d1(t[p], et[p]) for p in prs]
        n *= 2
    u1 = [d1(t[p], x_in[p]) for p in prs]
    res = [x_in[p] - u1[p] + _dot3(a_ab[p], u1[p]) for p in prs]
    u = [u1[p] + d1(t[p], res[p]) for p in prs]

    ub = [u[p].astype(BF16) for p in prs]
    y2 = [ypart[p] + _dg(a_rb[p].astype(BF16), ub[p], NN) for p in prs]
    for p in prs:
        ends = cat([stack2(lanes(bend, p)), stack2(lanes(kend, p))]).astype(BF16)
        s_ref[p] = s0[p] * lanes(e_tot, p) + _dg(cat([ub[p], v2[p]]), ends, TN)

    y_all = cat([y2[p][:c] + y2[p][c:] for p in prs])
    mean = _head_sum(y_all, head_ones) * inv_n
    yc = y_all - mean
    var = _head_sum(yc * yc, head_ones) * inv_n
    yn = yc * lax.rsqrt(var + RWKV_GN_EPS)
    rkk = r * k * rk_ref[...]
    bonus = _head_sum(cat([lanes(rkk, p) for p in prs]), head_ones)
    lnw, lnb, g = lnw_ref[...], lnb_ref[...], g_ref[...]
    for p in prs:
        rows = slice(p * c, (p + 1) * c)
        out = (yn[rows] * lanes(lnw, p) + lanes(lnb, p) + bonus[rows] * lanes(v, p)) * lanes(g, p)
        o_ref[:, p * LANES:(p + 1) * LANES] = out.astype(o_ref.dtype)


def rwkv_recurrence(r, lw, k, v, kk, beta, g, ln_w, ln_b, r_k, batch, seq, pairs=16):
    m, width = r.shape
    c = CHUNK
    assert seq % c == 0 and 2 * c == LANES and 2 * RWKV_HEAD_DIM == LANES
    tc = pairs * LANES
    nchunk = seq // c
    blk = pl.BlockSpec((c, tc), lambda b, j, ci: (b * nchunk + ci, j))
    vec = pl.BlockSpec((1, tc), lambda b, j, ci: (0, j))
    return pl.pallas_call(
        functools.partial(_rwkv_kernel, pairs=pairs),
        grid=(batch, width // tc, nchunk),
        in_specs=[blk] * 7 + [vec] * 3,
        out_specs=blk,
        out_shape=jax.ShapeDtypeStruct((m, width), BF16),
        scratch_shapes=[pltpu.VMEM((pairs, LANES, LANES), F32)],
        compiler_params=_cparams(("parallel", "parallel", "arbitrary")),
        name="rwkv_recurrence",
    )(r, lw, k, v, kk, beta, g, ln_w, ln_b, r_k)


def _xattn_kernel(q_ref, k_ref, v_ref, o_ref, *, scale):
    s = _dg(q_ref[...], k_ref[...], NT) * scale
    m = jnp.max(s, axis=-1, keepdims=True)
    p = jnp.exp(s - m)
    l = jnp.sum(p, axis=-1, keepdims=True)
    o = jnp.dot(p.astype(BF16), v_ref[...], preferred_element_type=F32)
    o_ref[...] = (o / l).astype(o_ref.dtype)


def cross_attention(q, k, v, batch, seq, n_mem, heads, tq=2048):
    m, d = q.shape
    dh = d // heads
    tq = min(tq, seq)
    nq = seq // tq
    return pl.pallas_call(
        functools.partial(_xattn_kernel, scale=dh ** -0.5),
        grid=(batch, heads, nq),
        in_specs=[pl.BlockSpec((tq, dh), lambda b, h, qi: (b * nq + qi, h)),
                  pl.BlockSpec((n_mem, dh), lambda b, h, qi: (b, h)),
                  pl.BlockSpec((n_mem, dh), lambda b, h, qi: (b, h))],
        out_specs=pl.BlockSpec((tq, dh), lambda b, h, qi: (b * nq + qi, h)),
        out_shape=jax.ShapeDtypeStruct((m, d), BF16),
        compiler_params=_cparams(("parallel", "parallel", "arbitrary")),
        name="cross_attention",
    )(q, k, v)


def _pad_cols(w, n):
    return jnp.pad(w, ((0, 0), (0, n - w.shape[1])))


def _pad_rows(w, n):
    return jnp.pad(w, ((0, n - w.shape[0]), (0, 0)))


def _bf(w):
    return w.astype(BF16)


def _row(t):
    return t.reshape(1, -1).astype(F32)


def _mixer(h, p, batch, seq):
    fox_w, rw = FOX_WIDTH, RWKV_WIDTH
    fox_heads = fox_w // FOX_HEAD_DIM
    bf, row = _bf, _row

    w_in = p['w_in']
    o_f = 3 * fox_w
    o_r = o_f + fox_heads
    o_w = o_r + 3 * rw
    o_a = o_w + DECAY_LORA
    o_g = o_a + ICLR_LORA
    wt = w_in.T
    wt_small = jnp.concatenate([
        _pad_rows(wt[o_w:o_a], LANES), _pad_rows(wt[o_a:o_g], LANES),
        wt[o_g:o_g + GATE_LORA], _pad_rows(wt[o_f:o_r], LANES)], axis=0)
    mu = p['rwkv_mu']
    mu_small = jnp.concatenate([
        jnp.pad(mu[3 * rw:3 * rw + DECAY_LORA], (0, LANES - DECAY_LORA)),
        jnp.pad(mu[3 * rw + DECAY_LORA:3 * rw + DECAY_LORA + ICLR_LORA], (0, LANES - ICLR_LORA)),
        mu[3 * rw + DECAY_LORA + ICLR_LORA:], jnp.zeros((LANES,), F32)])

    q_scale = jnp.concatenate([jnp.full((1, fox_w), FOX_Q_SCALE, F32),
                               jnp.ones((1, 2 * fox_w), F32)], axis=1)
    qkv = matmul_nt(h, wt, BF16, 0, o_f, tn=PROJ_TN, col_scale=q_scale)
    rkv = matmul_nt(h, wt, F32, o_r, 3 * rw, tn=PROJ_TN)
    small = matmul_nt(h, wt_small, F32, 0, wt_small.shape[0], tn=wt_small.shape[0])

    f_bias = jnp.pad(p['fox_f_bias'], (0, LANES - fox_heads)).reshape(1, LANES)
    c = fgate_cumsum(small, f_bias, f_block=(2 * LANES + GATE_LORA) // LANES, seq=seq)
    c_row = c[:, :fox_heads].reshape(batch, seq, fox_heads).transpose(0, 2, 1)[:, :, None, :]
    y_fox = fox_attention(qkv, c, c_row, batch, seq, fox_heads)

    r, lw, k, v, kk, beta, g = rwkv_prep(
        rkv, small, row(mu[:rw]), row(mu[rw:2 * rw]), row(mu[2 * rw:3 * rw]), row(mu_small),
        row(p['rwkv_w0']), _pad_rows(p['rwkv_w_up'], LANES), row(p['rwkv_a0']),
        _pad_rows(p['rwkv_a_up'], LANES), p['rwkv_g_up'], row(p['rwkv_k_k']),
        row(p['rwkv_k_a']), seq=seq)
    y_rwkv = rwkv_recurrence(r, lw, k, v, kk, beta, g, row(p['rwkv_ln_w']),
                             row(p['rwkv_ln_b']), row(p['rwkv_r_k']), batch, seq)

    return matmul2(y_fox, y_rwkv, p['w_out'], BF16, tn=F32_W_TN)


def _xattn(h, mn, p, batch, seq, n_mem):
    q = matmul(h, p['xattn_wq'], BF16, tn=F32_W_TN)
    km = matmul(mn, p['xattn_wk'], BF16, tn=F32_W_TN)
    vm = matmul(mn, p['xattn_wv'], BF16, tn=F32_W_TN)
    o = cross_attention(q, km, vm, batch, seq, n_mem, XATTN_HEADS)
    return matmul(o, p['xattn_wo'], BF16, tn=F32_W_TN)


def _layer(x, mem, p, batch, seq, n_mem):
    h = rmsnorm(x, p['ffn1_pre_g'])
    y = ffn(h, p['ffn1_w_gate'], p['ffn1_w_up'], p['ffn1_w_down'])
    x, h = resnorm(x, y, p['ffn1_post_g'], 0.5, p['mix_pre_g'])

    mixed = _mixer(h, p, batch, seq)
    x, h = resnorm(x, mixed, p['mix_post_g'], 1.0, p['xattn_pre_g'])

    mn = rmsnorm(mem, p['mem_norm_g'])
    xa = _xattn(h, mn, p, batch, seq, n_mem)
    x, h = resnorm(x, xa, p['xattn_post_g'], 1.0, p['ffn2_pre_g'])

    y = ffn(h, p['ffn2_w_gate'], p['ffn2_w_up'], p['ffn2_w_down'])
    return resnorm(x, y, p['ffn2_post_g'], 0.5)


_PARAM_NAMES = (
    'ffn1_pre_g', 'ffn1_w_gate', 'ffn1_w_up', 'ffn1_w_down', 'ffn1_post_g',
    'mix_pre_g', 'w_in', 'fox_f_bias', 'rwkv_mu', 'rwkv_w0', 'rwkv_w_up', 'rwkv_a0',
    'rwkv_a_up', 'rwkv_g_up', 'rwkv_k_k', 'rwkv_k_a', 'rwkv_r_k', 'rwkv_ln_w', 'rwkv_ln_b',
    'w_out', 'mix_post_g', 'xattn_pre_g', 'mem_norm_g', 'xattn_wq', 'xattn_wk', 'xattn_wv',
    'xattn_wo', 'xattn_post_g', 'ffn2_pre_g', 'ffn2_w_gate', 'ffn2_w_up', 'ffn2_w_down',
    'ffn2_post_g')


def kernel(x, mem, ffn1_pre_g, ffn1_w_gate, ffn1_w_up, ffn1_w_down, ffn1_post_g, mix_pre_g, w_in, fox_f_bias, rwkv_mu, rwkv_w0, rwkv_w_up, rwkv_a0, rwkv_a_up, rwkv_g_up, rwkv_k_k, rwkv_k_a, rwkv_r_k, rwkv_ln_w, rwkv_ln_b, w_out, mix_post_g, xattn_pre_g, mem_norm_g, xattn_wq, xattn_wk, xattn_wv, xattn_wo, xattn_post_g, ffn2_pre_g, ffn2_w_gate, ffn2_w_up, ffn2_w_down, ffn2_post_g):
    weights = (ffn1_pre_g, ffn1_w_gate, ffn1_w_up, ffn1_w_down, ffn1_post_g, mix_pre_g, w_in,
               fox_f_bias, rwkv_mu, rwkv_w0, rwkv_w_up, rwkv_a0, rwkv_a_up, rwkv_g_up, rwkv_k_k,
               rwkv_k_a, rwkv_r_k, rwkv_ln_w, rwkv_ln_b, w_out, mix_post_g, xattn_pre_g,
               mem_norm_g, xattn_wq, xattn_wk, xattn_wv, xattn_wo, xattn_post_g, ffn2_pre_g,
               ffn2_w_gate, ffn2_w_up, ffn2_w_down, ffn2_post_g)
    batch, seq, d = x.shape
    n_mem = mem.shape[1]
    depth = ffn1_pre_g.shape[0]
    xf = x.reshape(batch * seq, d)
    mf = mem.reshape(batch * n_mem, d)
    for l in range(depth):
        p = {name: w[l] for name, w in zip(_PARAM_NAMES, weights)}
        xf = _layer(xf, mf, p, batch, seq, n_mem)
    return xf.reshape(batch, seq, d)
```
